```python
import math
import jax, jax.numpy as jnp
from jax import lax
import numpy as np

D_MODEL = 1024
BATCH = 8
SEQ = 2048
DEPTH = 2
DEC_BATCH = 128
DEC_SEQ = 8
PAST_LEN = 16384
PAGE_SIZE = 128

N_META = 16
D_MIX = D_MODEL
D_A = D_MIX // 2
D_R = D_MIX - D_A
H_A = 4
DK = D_A // H_A
DV = D_A // H_A
CHUNK = 64
CONV_W = 4
NB_R = 8
BW_R = D_R // NB_R
C_RG = 8.0
D_FF = ((8 * D_MODEL // 3 + 127) // 128) * 128
D_IN = 4 * D_A + 2 * H_A + 2 * D_R
EPS = 1e-6

kernel_name = "hymba_deltanet_rglru_macaron_step"


def rmsnorm(x, w):
    xf = x.astype(jnp.float32)
    y = xf * lax.rsqrt(jnp.mean(xf * xf, axis=-1, keepdims=True) + EPS)
    return (y * w.astype(jnp.float32)).astype(x.dtype)


def l2norm(x):
    xf = x.astype(jnp.float32)
    return (xf * lax.rsqrt(jnp.sum(xf * xf, axis=-1, keepdims=True) + EPS)).astype(x.dtype)


def swiglu(h, wg, wu, wd):
    return (jax.nn.silu(h @ wg) * (h @ wu)) @ wd


def causal_conv(x, buf, w):
    t = x.shape[1]
    xp = jnp.concatenate([buf.astype(x.dtype), x], axis=1)
    y = xp[:, 0:t] * w[0]
    for j in range(1, CONV_W):
        y = y + xp[:, j:j + t] * w[j]
    return y, xp[:, -(CONV_W - 1):]


def gated_delta_chunked(q, k, v, g, beta, s0, front):
    b, t = q.shape[:2]
    back = (-(front + t)) % CHUNK
    f32 = jnp.float32

    def prep(a):
        a = a.astype(f32)
        a = jnp.pad(a, [(0, 0), (front, back)] + [(0, 0)] * (a.ndim - 2))
        n = a.shape[1] // CHUNK
        a = a.reshape((b, n, CHUNK) + a.shape[2:])
        return a.transpose((1, 0, 3, 2) + tuple(range(4, a.ndim)))

    qc, kc, vc, gc, bc = prep(q), prep(k), prep(v), prep(g), prep(beta)
    gcum = jnp.cumsum(gc, axis=-1)
    idx = jnp.arange(CHUNK)
    causal = idx[:, None] >= idx[None, :]
    strict = idx[:, None] > idx[None, :]
    decay = jnp.exp(jnp.where(causal, gcum[..., :, None] - gcum[..., None, :], -jnp.inf))
    kb = kc * bc[..., None]
    lmat = jnp.where(strict, jnp.einsum('nbhik,nbhjk->nbhij', kb, kc) * decay, 0.0)
    rhs = jnp.concatenate([vc * bc[..., None], kb * jnp.exp(gcum)[..., None]], axis=-1)
    sol = lax.linalg.triangular_solve(lmat, rhs, left_side=True, lower=True, unit_diagonal=True)
    u, w = sol[..., :DV], sol[..., DV:]
    attn = jnp.einsum('nbhik,nbhjk->nbhij', qc, kc) * decay

    def step(s, xs):
        q_c, k_c, u_c, w_c, g_c, a_c = xs
        v_new = u_c - jnp.einsum('bhck,bhkv->bhcv', w_c, s)
        o_c = (jnp.einsum('bhck,bhkv->bhcv', q_c * jnp.exp(g_c)[..., None], s)
               + jnp.einsum('bhij,bhjv->bhiv', a_c, v_new))
        g_last = g_c[..., -1]
        k_dec = k_c * jnp.exp(g_last[..., None] - g_c)[..., None]
        s = s * jnp.exp(g_last)[..., None, None] + jnp.einsum('bhck,bhcv->bhkv', k_dec, v_new)
        return s, o_c

    s_final, o = lax.scan(step, s0.astype(f32), (qc, kc, u, w, gcum, attn))
    n = o.shape[0]
    o = o.transpose(1, 0, 3, 2, 4).reshape(b, n * CHUNK, H_A, DV)[:, front:front + t]
    return o, s_final


def lru_scan(a, u, h0):
    def step(h, au):
        a_t, u_t = au
        h = a_t * h + u_t
        return h, h
    h_t, hs = lax.scan(step, h0.astype(jnp.float32), (a.swapaxes(0, 1), u.swapaxes(0, 1)))
    return hs.swapaxes(0, 1), h_t


def trunk_layer(x, s_delta, s_dconv, s_lru, s_lconv, front, p):
    bsz, t, _ = x.shape
    dt = x.dtype
    f32 = jnp.float32
    x = x + 0.5 * swiglu(rmsnorm(x, p['n1']), p['f1g'], p['f1u'], p['f1d'])
    h = rmsnorm(x, p['nm'])
    proj = h @ p['w_in']
    c1 = 3 * D_A
    c2 = c1 + D_A
    c3 = c2 + H_A
    c4 = c3 + H_A
    c5 = c4 + D_R
    qkv_raw, z, b_logit = proj[..., :c1], proj[..., c1:c2], proj[..., c2:c3]
    a_logit, xr, gate = proj[..., c3:c4], proj[..., c4:c5], proj[..., c5:]
    qkv, new_dconv = causal_conv(qkv_raw, s_dconv, p['cqkv'])
    qkv = jax.nn.silu(qkv)
    q = l2norm(qkv[..., :D_A].reshape(bsz, t, H_A, DK)) * (DK ** -0.5)
    k = l2norm(qkv[..., D_A:2 * D_A].reshape(bsz, t, H_A, DK))
    v = qkv[..., 2 * D_A:].reshape(bsz, t, H_A, DV)
    beta = jax.nn.sigmoid(b_logit.astype(f32))
    g = -jnp.exp(p['a_log'].astype(f32)) * jax.nn.softplus(a_logit.astype(f32) + p['dt_bias'].astype(f32))
    o, new_delta = gated_delta_chunked(q, k, v, g, beta, s_delta, front)
    o = rmsnorm(o.astype(dt), p['onorm']) * jax.nn.silu(z.reshape(bsz, t, H_A, DV))
    o = o.reshape(bsz, t, D_A)
    xc, new_lconv = causal_conv(xr, s_lconv, p['clw'])
    xc = xc + p['clb']
    xb = xc.reshape(bsz, t, NB_R, BW_R)
    r = jax.nn.sigmoid((jnp.einsum('btnc,ncd->btnd', xb, p['wr']).reshape(bsz, t, D_R) + p['br']).astype(f32))
    i = jax.nn.sigmoid((jnp.einsum('btnc,ncd->btnd', xb, p['wi']).reshape(bsz, t, D_R) + p['bi']).astype(f32))
    log_a = -C_RG * r * jax.nn.softplus(-p['lam'].astype(f32))
    mult = jnp.sqrt(-jnp.expm1(2.0 * log_a))
    hs, new_lru = lru_scan(jnp.exp(log_a), mult * i * xc.astype(f32), s_lru)
    y = hs.astype(dt) * jax.nn.gelu(gate, approximate=True)
    x = x + jnp.concatenate([o, y], axis=-1) @ p['w_out']
    x = x + 0.5 * swiglu(rmsnorm(x, p['n2']), p['f2g'], p['f2u'], p['f2d'])
    return x, new_delta.astype(dt), new_dconv.astype(dt), new_lru.astype(dt), new_lconv.astype(dt)


def setup_inputs(seed: int = 0) -> dict:
    key = jax.random.key(seed)
    ks = iter(jax.random.split(key, 48))
    f32 = jnp.float32

    def nrm(shape, scale):
        return jax.random.normal(next(ks), shape, f32) * scale

    def gain(shape):
        return 1.0 + nrm(shape, 0.02)

    a_log = jnp.log(jax.random.uniform(next(ks), (DEPTH, H_A), f32, 1.0, 16.0))
    dtv = jnp.exp(jax.random.uniform(next(ks), (DEPTH, H_A), f32, math.log(1e-3), math.log(1e-1)))
    dt_bias = dtv + jnp.log(-jnp.expm1(-dtv))
    a_c = jax.random.uniform(next(ks), (DEPTH, D_R), f32, 0.9, 0.999)
    a_base = a_c ** (1.0 / C_RG)
    lru_lambda = jnp.log(a_base) - jnp.log1p(-a_base)
    return {
        'x_prompt': nrm((BATCH, SEQ, D_MODEL), 1.0),
        'x_sample': nrm((DEC_BATCH, DEC_SEQ, D_MODEL), 1.0),
        'state_delta': nrm((DEPTH, DEC_BATCH, H_A, DK, DV), 0.1),
        'state_delta_conv': nrm((DEPTH, DEC_BATCH, CONV_W - 1, 3 * D_A), 1.0),
        'state_lru': nrm((DEPTH, DEC_BATCH, D_R), 0.5),
        'state_lru_conv': nrm((DEPTH, DEC_BATCH, CONV_W - 1, D_R), 1.0),
        'meta_tokens': nrm((N_META, D_MODEL), 1.0),
        'norm_ffn1': gain((DEPTH, D_MODEL)),
        'w_ffn1_gate': nrm((DEPTH, D_MODEL, D_FF), D_MODEL ** -0.5),
        'w_ffn1_up': nrm((DEPTH, D_MODEL, D_FF), D_MODEL ** -0.5),
        'w_ffn1_down': nrm((DEPTH, D_FF, D_MODEL), D_FF ** -0.5),
        'norm_mix': gain((DEPTH, D_MODEL)),
        'w_in': nrm((DEPTH, D_MODEL, D_IN), D_MODEL ** -0.5),
        'conv_qkv': nrm((DEPTH, CONV_W, 3 * D_A), CONV_W ** -0.5),
        'a_log': a_log,
        'dt_bias': dt_bias,
        'norm_delta_out': gain((DEPTH, DV)),
        'conv_lru_w': nrm((DEPTH, CONV_W, D_R), CONV_W ** -0.5),
        'conv_lru_b': nrm((DEPTH, D_R), 0.01),
        'w_rgate': nrm((DEPTH, NB_R, BW_R, BW_R), BW_R ** -0.5),
        'b_rgate': nrm((DEPTH, D_R), 0.01),
        'w_igate': nrm((DEPTH, NB_R, BW_R, BW_R), BW_R ** -0.5),
        'b_igate': nrm((DEPTH, D_R), 0.01),
        'lru_lambda': lru_lambda,
        'w_out': nrm((DEPTH, D_MIX, D_MODEL), D_MIX ** -0.5),
        'norm_ffn2': gain((DEPTH, D_MODEL)),
        'w_ffn2_gate': nrm((DEPTH, D_MODEL, D_FF), D_MODEL ** -0.5),
        'w_ffn2_up': nrm((DEPTH, D_MODEL, D_FF), D_MODEL ** -0.5),
        'w_ffn2_down': nrm((DEPTH, D_FF, D_MODEL), D_FF ** -0.5),
        'norm_final': gain((D_MODEL,)),
    }


def reference(x_prompt, x_sample, state_delta, state_delta_conv, state_lru, state_lru_conv,
              meta_tokens, norm_ffn1, w_ffn1_gate, w_ffn1_up, w_ffn1_down, norm_mix, w_in,
              conv_qkv, a_log, dt_bias, norm_delta_out, conv_lru_w, conv_lru_b, w_rgate, b_rgate,
              w_igate, b_igate, lru_lambda, w_out, norm_ffn2, w_ffn2_gate, w_ffn2_up, w_ffn2_down,
              norm_final):
    layers = [dict(n1=norm_ffn1[l], f1g=w_ffn1_gate[l], f1u=w_ffn1_up[l], f1d=w_ffn1_down[l],
                   nm=norm_mix[l], w_in=w_in[l], cqkv=conv_qkv[l], a_log=a_log[l],
                   dt_bias=dt_bias[l], onorm=norm_delta_out[l], clw=conv_lru_w[l],
                   clb=conv_lru_b[l], wr=w_rgate[l], br=b_rgate[l], wi=w_igate[l], bi=b_igate[l],
                   lam=lru_lambda[l], w_out=w_out[l], n2=norm_ffn2[l], f2g=w_ffn2_gate[l],
                   f2u=w_ffn2_up[l], f2d=w_ffn2_down[l]) for l in range(DEPTH)]

    def run(x, sd, sdc, sl, slc, front):
        nd, ndc, nl, nlc = [], [], [], []
        for l in range(DEPTH):
            x, a, b, c, d = trunk_layer(x, sd[l], sdc[l], sl[l], slc[l], front, layers[l])
            nd.append(a)
            ndc.append(b)
            nl.append(c)
            nlc.append(d)
        return rmsnorm(x, norm_final), jnp.stack(nd), jnp.stack(ndc), jnp.stack(nl), jnp.stack(nlc)

    bp = x_prompt.shape[0]
    dtp = x_prompt.dtype
    meta = jnp.broadcast_to(meta_tokens.astype(dtp)[None], (bp, N_META, D_MODEL))
    xp = jnp.concatenate([meta, x_prompt], axis=1)
    zd = jnp.zeros((DEPTH, bp, H_A, DK, DV), dtp)
    zdc = jnp.zeros((DEPTH, bp, CONV_W - 1, 3 * D_A), dtp)
    zl = jnp.zeros((DEPTH, bp, D_R), dtp)
    zlc = jnp.zeros((DEPTH, bp, CONV_W - 1, D_R), dtp)
    yp, nd_p, ndc_p, nl_p, nlc_p = run(xp, zd, zdc, zl, zlc, (-N_META) % CHUNK)
    y_prompt = yp[:, N_META:]
    y_sample, nd_s, ndc_s, nl_s, nlc_s = run(x_sample, state_delta, state_delta_conv,
                                             state_lru, state_lru_conv, 0)
    return (y_prompt, y_sample, nd_p, ndc_p, nl_p, nlc_p, nd_s, ndc_s, nl_s, nlc_s)
```

```python
import functools
import math

import jax
import jax.numpy as jnp
from jax import lax
from jax.experimental import pallas as pl
from jax.experimental.pallas import tpu as pltpu

F32 = jnp.float32
BF16 = jnp.bfloat16

D_MODEL = 1024
D_FF = 2816
D_A = 512
D_R = 512
H_A = 4
DK = 128
DV = 128
CHUNK = 64
CONV_W = 4
NB_R = 8
C_RG = 8.0
EPS = 1e-6
N_META = 16
D_QKV = 3 * D_A
D_BA = 128
D_PROJ = D_QKV + D_A + 2 * D_R + D_BA

SUBLANES = 8
TAIL = SUBLANES
VMEM_BYTES_V7X = 64 * 1024 * 1024
VMEM_LIMIT = VMEM_BYTES_V7X - 8 * 1024 * 1024
ROW_TILE = 256
LRU_TIME_TILE = 256
SAMPLE_BATCH_TILE = 8


def _rms(x, w):
    ms = jnp.mean(x * x, axis=-1, keepdims=True)
    return x * lax.rsqrt(ms + EPS) * w


def _sigmoid(x):
    return 1.0 / (1.0 + jnp.exp(-x))


def _silu(x):
    return x * _sigmoid(x)


def _softplus(x):
    return jnp.maximum(x, 0.0) + jnp.log1p(jnp.exp(-jnp.abs(x)))


def _dot(a, b):
    return jnp.dot(a.astype(BF16), b.astype(BF16), preferred_element_type=F32)


def _dot_nt(a, b):
    return lax.dot_general(a.astype(BF16), b.astype(BF16), (((1,), (1,)), ((), ())),
                           preferred_element_type=F32)


def _dot_tn(a, b):
    return lax.dot_general(a.astype(BF16), b.astype(BF16), (((0,), (0,)), ((), ())),
                           preferred_element_type=F32)


def _split3(a):
    a1 = a.astype(BF16)
    r = a - a1.astype(F32)
    a2 = r.astype(BF16)
    a3 = (r - a2.astype(F32)).astype(BF16)
    return a1, a2, a3


def _dot_hi(a, b):
    a1, a2, a3 = _split3(a)
    b1, b2, b3 = _split3(b)
    d = lambda p, q: jnp.dot(p, q, preferred_element_type=F32)
    return (d(a1, b1) + (d(a1, b2) + d(a2, b1))) + ((d(a2, b2) + d(a1, b3)) + d(a3, b1))


def _ffn(x, nw, wg_ref, wu_ref, wd_ref):
    h = _rms(x, nw).astype(BF16)
    g = jnp.dot(h, wg_ref[...], preferred_element_type=F32)
    u = jnp.dot(h, wu_ref[...], preferred_element_type=F32)
    a = (_silu(g) * u).astype(BF16)
    return x + 0.5 * jnp.dot(a, wd_ref[...], preferred_element_type=F32)


def _ffn_in_body(x_ref, n1_ref, wg_ref, wu_ref, wd_ref, nm_ref, win_ref,
                 x1_ref, qkv_ref, z_ref, xr_ref, gate_ref, ba_ref):
    x1 = _ffn(x_ref[...], n1_ref[...], wg_ref, wu_ref, wd_ref)
    x1_ref[...] = x1
    hm = _rms(x1, nm_ref[...]).astype(BF16)
    p = jnp.dot(hm, win_ref[...], preferred_element_type=F32)
    c1 = D_QKV
    c2 = c1 + D_A
    c3 = c2 + D_R
    c4 = c3 + D_R
    qkv_ref[...] = p[:, :c1]
    z_ref[...] = p[:, c1:c2]
    xr_ref[...] = p[:, c2:c3]
    gate_ref[...] = p[:, c3:c4]
    ba_ref[...] = p[:, c4:]


def _out_ffn_body(x_ref, o_ref, y_ref, wo1_ref, wo2_ref, n2_ref, wg_ref, wu_ref, wd_ref, nf_ref,
                  out_ref, *, final):
    x = x_ref[...] + (_dot(o_ref[...], wo1_ref[...]) + _dot(y_ref[...], wo2_ref[...]))
    x2 = _ffn(x, n2_ref[...], wg_ref, wu_ref, wd_ref)
    if final:
        x2 = _rms(x2, nf_ref[...])
    out_ref[...] = x2


def _resident(shape):
    nd = len(shape)
    return pl.BlockSpec(shape, lambda *_: (0,) * nd, pipeline_mode=pl.Buffered(1))


def _rows(tm, width):
    return pl.BlockSpec((tm, width), lambda i: (i, 0))


def _ffn_in(x, lw):
    r = x.shape[0]
    tm = min(ROW_TILE, r)
    widths = (D_MODEL, D_QKV, D_A, D_R, D_R, D_BA)
    return pl.pallas_call(
        _ffn_in_body,
        grid=(pl.cdiv(r, tm),),
        in_specs=[_rows(tm, D_MODEL), _resident((1, D_MODEL)),
                  _resident((D_MODEL, D_FF)), _resident((D_MODEL, D_FF)), _resident((D_FF, D_MODEL)),
                  _resident((1, D_MODEL)), _resident((D_MODEL, D_PROJ))],
        out_specs=[_rows(tm, w) for w in widths],
        out_shape=[jax.ShapeDtypeStruct((r, w), F32) for w in widths],
        compiler_params=pltpu.CompilerParams(dimension_semantics=("parallel",),
                                             vmem_limit_bytes=VMEM_LIMIT),
        name="ffn_in",
    )(x, lw["n1"], lw["f1g"], lw["f1u"], lw["f1d"], lw["nm"], lw["w_in"])


def _out_ffn(x, o, y, lw, nf, final):
    r = x.shape[0]
    tm = min(ROW_TILE, r)
    return pl.pallas_call(
        functools.partial(_out_ffn_body, final=final),
        grid=(pl.cdiv(r, tm),),
        in_specs=[_rows(tm, D_MODEL), _rows(tm, D_A), _rows(tm, D_R),
                  _resident((D_A, D_MODEL)), _resident((D_R, D_MODEL)), _resident((1, D_MODEL)),
                  _resident((D_MODEL, D_FF)), _resident((D_MODEL, D_FF)), _resident((D_FF, D_MODEL)),
                  _resident((1, D_MODEL))],
        out_specs=_rows(tm, D_MODEL),
        out_shape=jax.ShapeDtypeStruct((r, D_MODEL), F32),
        compiler_params=pltpu.CompilerParams(dimension_semantics=("parallel",),
                                             vmem_limit_bytes=VMEM_LIMIT),
        name="out_ffn",
    )(x, o, y, lw["wo1"], lw["wo2"], lw["n2"], lw["f2g"], lw["f2u"], lw["f2d"], nf)


def _delta_body(qkv_ref, z_ref, ba_ref, cs_ref, s0_ref, cw_ref, alog_ref, dtb_ref, on_ref,
                o_ref, snew_ref, csnew_ref, ext_ref, *, chunk, bb, n_t):
    c = chunk
    t = pl.program_id(1)

    @pl.when(t == 0)
    def _():
        ext_ref[:, TAIL - (CONV_W - 1):TAIL, :] = cs_ref[...]
        snew_ref[...] = s0_ref[...]

    ext_ref[:, TAIL:TAIL + c, :] = qkv_ref[...]

    row = lax.broadcasted_iota(jnp.int32, (c, c), 0)
    col = lax.broadcasted_iota(jnp.int32, (c, c), 1)
    causal = row >= col
    strict = row > col
    tri = causal.astype(BF16)
    eye = (row == col).astype(F32)
    sel_r = lax.broadcasted_iota(jnp.int32, (SUBLANES, D_BA), 0)
    sel_c = lax.broadcasted_iota(jnp.int32, (SUBLANES, D_BA), 1)
    sel = ((sel_c == sel_r + H_A) & (sel_r < H_A)).astype(BF16)
    levels = int(math.log2(c))
    assert 2 ** levels == c

    def conv_silu(b, col0):
        acc = None
        for j in range(CONV_W):
            xs = ext_ref[b, pl.ds(TAIL - (CONV_W - 1) + j, c), pl.ds(col0, DK)]
            term = xs * cw_ref[j:j + 1, col0:col0 + DK]
            acc = term if acc is None else acc + term
        return _silu(acc)

    def per_b(b, carry):
        ba = ba_ref[b]
        beta_all = _sigmoid(ba)
        g_all = -jnp.exp(alog_ref[...]) * _softplus(ba + dtb_ref[...])
        gcum = None
        for p in _split3(g_all):
            d = jnp.dot(tri, p, preferred_element_type=F32)
            gcum = d if gcum is None else gcum + d
        grow = None
        for p in _split3(gcum):
            d = lax.dot_general(sel, p, (((1,), (1,)), ((), ())), preferred_element_type=F32)
            grow = d if grow is None else grow + d
        eg_all = jnp.exp(gcum)

        for h in range(H_A):
            qh = conv_silu(b, h * DK)
            kh = conv_silu(b, D_A + h * DK)
            vh = conv_silu(b, 2 * D_A + h * DV)
            qn = qh * lax.rsqrt(jnp.sum(qh * qh, axis=-1, keepdims=True) + EPS) * (DK ** -0.5)
            kn = kh * lax.rsqrt(jnp.sum(kh * kh, axis=-1, keepdims=True) + EPS)
            beta = beta_all[:, h:h + 1]
            gcol = gcum[:, H_A + h:H_A + h + 1]
            egcol = eg_all[:, H_A + h:H_A + h + 1]
            grow_h = grow[h:h + 1, :]
            glast = grow[h:h + 1, c - 1:c]

            dmat = gcol - grow_h
            decay = jnp.where(causal, jnp.exp(jnp.where(causal, dmat, 0.0)), 0.0)
            kb = kn * beta
            lmat = jnp.where(strict, _dot_nt(kb, kn) * decay, 0.0)
            rhs = jnp.concatenate([vh * beta, kb * egcol], axis=-1)
            tinv = eye - lmat
            m = lmat
            for _ in range(levels - 1):
                m = _dot_hi(m, m)
                tinv = tinv + _dot_hi(tinv, m)
            sol = _dot_hi(tinv, rhs)
            u = sol[:, :DV]
            w = sol[:, DV:]
            attn = _dot_nt(qn, kn) * decay

            s = snew_ref[b, h]
            v_new = u - _dot(w, s)
            o = _dot(qn * egcol, s) + _dot(attn, v_new)
            k_dec = kn * jnp.exp(glast - gcol)
            snew_ref[b, h] = s * jnp.exp(glast) + _dot_tn(k_dec, v_new)

            on = _rms(o, on_ref[...])
            o_ref[b, :, h * DV:(h + 1) * DV] = on * _silu(z_ref[b, :, h * DV:(h + 1) * DV])
        return carry

    lax.fori_loop(0, bb, per_b, 0)

    tail = ext_ref[:, c:c + TAIL, :]
    ext_ref[:, 0:TAIL, :] = tail

    @pl.when(t == n_t - 1)
    def _():
        csnew_ref[...] = ext_ref[:, TAIL - (CONV_W - 1):TAIL, :]


def _delta(qkv, z, ba, cs, s0, lw, *, chunk, bb, shared_state):
    b, t, _ = qkv.shape
    n_t = t // chunk
    assert n_t * chunk == t and b % bb == 0
    st = (lambda i, j: (0, 0, 0, 0)) if shared_state else (lambda i, j: (i, 0, 0, 0))
    ct = (lambda i, j: (0, 0, 0)) if shared_state else (lambda i, j: (i, 0, 0))
    tok = lambda w: pl.BlockSpec((bb, chunk, w), lambda i, j: (i, j, 0))
    const = lambda shape: pl.BlockSpec(shape, lambda i, j: (0,) * len(shape))
    return pl.pallas_call(
        functools.partial(_delta_body, chunk=chunk, bb=bb, n_t=n_t),
        grid=(b // bb, n_t),
        in_specs=[tok(D_QKV), tok(D_A), tok(D_BA),
                  pl.BlockSpec((bb, CONV_W - 1, D_QKV), ct),
                  pl.BlockSpec((bb, H_A, DK, DV), st),
                  const((CONV_W, D_QKV)), const((1, D_BA)), const((1, D_BA)), const((1, DV))],
        out_specs=[tok(D_A),
                   pl.BlockSpec((bb, H_A, DK, DV), lambda i, j: (i, 0, 0, 0)),
                   pl.BlockSpec((bb, CONV_W - 1, D_QKV), lambda i, j: (i, 0, 0))],
        out_shape=[jax.ShapeDtypeStruct((b, t, D_A), F32),
                   jax.ShapeDtypeStruct((b, H_A, DK, DV), F32),
                   jax.ShapeDtypeStruct((b, CONV_W - 1, D_QKV), F32)],
        scratch_shapes=[pltpu.VMEM((bb, chunk + TAIL, D_QKV), F32)],
        compiler_params=pltpu.CompilerParams(dimension_semantics=("arbitrary", "arbitrary"),
                                             vmem_limit_bytes=VMEM_LIMIT),
        name="delta",
    )(qkv, z, ba, cs, s0, lw["cqkv"], lw["alog"], lw["dtb"], lw["onorm"])


def _lru_body(xr_ref, gate_ref, cs_ref, h0_ref, cw_ref, cb_ref, wg_ref, bg_ref, lam_ref,
              y_ref, hnew_ref, csnew_ref, ext_ref, *, tt, bb, n_t):
    t = pl.program_id(1)

    @pl.when(t == 0)
    def _():
        ext_ref[:, TAIL - (CONV_W - 1):TAIL, :] = cs_ref[...]
        hnew_ref[...] = h0_ref[...]

    ext_ref[:, TAIL:TAIL + tt, :] = xr_ref[...]

    xc = None
    for j in range(CONV_W):
        off = TAIL - (CONV_W - 1) + j
        term = ext_ref[:, off:off + tt, :] * cw_ref[j:j + 1, :]
        xc = term if xc is None else xc + term
    xc = xc + cb_ref[...]
    n = bb * tt
    xc = xc.reshape(n, D_R)
    gates = jnp.dot(xc.astype(BF16), wg_ref[...], preferred_element_type=F32) + bg_ref[...]
    r = _sigmoid(gates[:, :D_R])
    i = _sigmoid(gates[:, D_R:])
    log_a = -C_RG * r * _softplus(-lam_ref[...])
    a = jnp.exp(log_a)
    u = jnp.sqrt(-jnp.tanh(log_a) * (a * a + 1.0)) * i * xc

    tpos = lax.broadcasted_iota(jnp.int32, (bb, tt, D_R), 1).reshape(n, D_R)
    s = 1
    while s < tt:
        a_sh = pltpu.roll(a, s, 0)
        u_sh = pltpu.roll(u, s, 0)
        m = tpos >= s
        u = jnp.where(m, u + a * u_sh, u)
        a = jnp.where(m, a * a_sh, a)
        s *= 2
    h0 = jnp.broadcast_to(hnew_ref[...], (bb, tt, D_R)).reshape(n, D_R)
    hs = u + a * h0
    y = hs * jax.nn.gelu(gate_ref[...].reshape(n, D_R), approximate=True)
    y_ref[...] = y.reshape(bb, tt, D_R)
    hnew_ref[...] = hs.reshape(bb, tt, D_R)[:, tt - 1:tt, :]

    tail = ext_ref[:, tt:tt + TAIL, :]
    ext_ref[:, 0:TAIL, :] = tail

    @pl.when(t == n_t - 1)
    def _():
        csnew_ref[...] = ext_ref[:, TAIL - (CONV_W - 1):TAIL, :]


def _lru(xr, gate, cs, h0, lw, *, tt, bb, shared_state):
    b, t, _ = xr.shape
    n_t = t // tt
    assert n_t * tt == t and b % bb == 0
    ct = (lambda i, j: (0, 0, 0)) if shared_state else (lambda i, j: (i, 0, 0))
    tok = pl.BlockSpec((bb, tt, D_R), lambda i, j: (i, j, 0))
    const = lambda shape: pl.BlockSpec(shape, lambda i, j: (0,) * len(shape))
    return pl.pallas_call(
        functools.partial(_lru_body, tt=tt, bb=bb, n_t=n_t),
        grid=(b // bb, n_t),
        in_specs=[tok, tok,
                  pl.BlockSpec((bb, CONV_W - 1, D_R), ct),
                  pl.BlockSpec((bb, 1, D_R), ct),
                  const((CONV_W, D_R)), const((1, D_R)), const((D_R, 2 * D_R)), const((1, 2 * D_R)),
                  const((1, D_R))],
        out_specs=[tok,
                   pl.BlockSpec((bb, 1, D_R), lambda i, j: (i, 0, 0)),
                   pl.BlockSpec((bb, CONV_W - 1, D_R), lambda i, j: (i, 0, 0))],
        out_shape=[jax.ShapeDtypeStruct((b, t, D_R), F32),
                   jax.ShapeDtypeStruct((b, 1, D_R), F32),
                   jax.ShapeDtypeStruct((b, CONV_W - 1, D_R), F32)],
        scratch_shapes=[pltpu.VMEM((bb, tt + TAIL, D_R), F32)],
        compiler_params=pltpu.CompilerParams(dimension_semantics=("arbitrary", "arbitrary"),
                                             vmem_limit_bytes=VMEM_LIMIT),
        name="lru",
    )(xr, gate, cs, h0, lw["clw"], lw["clb"], lw["wgate"], lw["bgate"], lw["lam"])


def _prep_layer(l, w):
    row = lambda v: v.reshape(1, -1).astype(F32)
    c1 = D_QKV
    c2 = c1 + D_A
    c3 = c2 + H_A
    c4 = c3 + H_A
    c5 = c4 + D_R
    w_in = w["w_in"][l]
    pad = jnp.zeros((D_MODEL, D_BA - 2 * H_A), F32)
    w_in_r = jnp.concatenate([w_in[:, :c2], w_in[:, c4:c5], w_in[:, c5:], w_in[:, c2:c4], pad], axis=1)
    lane_pad = lambda v: jnp.zeros((1, D_BA), F32).at[0, H_A:2 * H_A].set(v)
    eye = jnp.eye(NB_R, dtype=F32)
    bd = lambda wb: jnp.einsum("ncd,nm->ncmd", wb, eye).reshape(D_R, D_R)
    return dict(
        n1=row(w["norm_ffn1"][l]), f1g=w["w_ffn1_gate"][l].astype(BF16),
        f1u=w["w_ffn1_up"][l].astype(BF16), f1d=w["w_ffn1_down"][l].astype(BF16),
        nm=row(w["norm_mix"][l]), w_in=w_in_r.astype(BF16),
        cqkv=w["conv_qkv"][l], alog=lane_pad(w["a_log"][l]), dtb=lane_pad(w["dt_bias"][l]),
        onorm=row(w["norm_delta_out"][l]),
        clw=w["conv_lru_w"][l], clb=row(w["conv_lru_b"][l]),
        wgate=jnp.concatenate([bd(w["w_rgate"][l]), bd(w["w_igate"][l])], axis=1).astype(BF16),
        bgate=jnp.concatenate([row(w["b_rgate"][l]), row(w["b_igate"][l])], axis=1),
        lam=row(w["lru_lambda"][l]),
        wo1=w["w_out"][l][:D_A].astype(BF16), wo2=w["w_out"][l][D_A:].astype(BF16),
        n2=row(w["norm_ffn2"][l]), f2g=w["w_ffn2_gate"][l].astype(BF16),
        f2u=w["w_ffn2_up"][l].astype(BF16), f2d=w["w_ffn2_down"][l].astype(BF16),
    )


def _run_group(x, sd, sdc, sl, slc, layers, nf, *, chunk, delta_bb, lru_tt, lru_bb, shared_state):
    b, t, _ = x.shape
    xf = x.reshape(b * t, D_MODEL)
    nd, ndc, nl, nlc = [], [], [], []
    n_layers = len(layers)
    for l, lw in enumerate(layers):
        x1, qkv, z, xr, gate, ba = _ffn_in(xf, lw)
        r3 = lambda v: v.reshape(b, t, v.shape[-1])
        o, s_new, cs_new = _delta(r3(qkv), r3(z), r3(ba), sdc[l], sd[l], lw,
                                  chunk=chunk, bb=delta_bb, shared_state=shared_state)
        y, h_new, lcs_new = _lru(r3(xr), r3(gate), slc[l], sl[l], lw,
                                 tt=lru_tt, bb=lru_bb, shared_state=shared_state)
        xf = _out_ffn(x1, o.reshape(b * t, D_A), y.reshape(b * t, D_R), lw, nf, l == n_layers - 1)
        nd.append(s_new)
        ndc.append(cs_new)
        nl.append(h_new)
        nlc.append(lcs_new)
    return xf.reshape(b, t, D_MODEL), nd, ndc, nl, nlc


def kernel(x_prompt, x_sample, state_delta, state_delta_conv, state_lru, state_lru_conv, meta_tokens, norm_ffn1, w_ffn1_gate, w_ffn1_up, w_ffn1_down, norm_mix, w_in, conv_qkv, a_log, dt_bias, norm_delta_out, conv_lru_w, conv_lru_b, w_rgate, b_rgate, w_igate, b_igate, lru_lambda, w_out, norm_ffn2, w_ffn2_gate, w_ffn2_up, w_ffn2_down, norm_final):
    w = dict(norm_ffn1=norm_ffn1, w_ffn1_gate=w_ffn1_gate, w_ffn1_up=w_ffn1_up, w_ffn1_down=w_ffn1_down,
             norm_mix=norm_mix, w_in=w_in, conv_qkv=conv_qkv, a_log=a_log, dt_bias=dt_bias,
             norm_delta_out=norm_delta_out, conv_lru_w=conv_lru_w, conv_lru_b=conv_lru_b,
             w_rgate=w_rgate, b_rgate=b_rgate, w_igate=w_igate, b_igate=b_igate, lru_lambda=lru_lambda,
             w_out=w_out, norm_ffn2=norm_ffn2, w_ffn2_gate=w_ffn2_gate, w_ffn2_up=w_ffn2_up,
             w_ffn2_down=w_ffn2_down)
    depth = norm_ffn1.shape[0]
    layers = [_prep_layer(l, w) for l in range(depth)]
    nf = norm_final.reshape(1, D_MODEL).astype(F32)
    bp, seq, _ = x_prompt.shape
    bs, dseq, _ = x_sample.shape

    zeros = lambda *s: [jnp.zeros(s, F32)] * depth
    _, md, mdc, ml, mlc = _run_group(
        meta_tokens.astype(F32)[None], zeros(1, H_A, DK, DV), zeros(1, CONV_W - 1, D_QKV),
        zeros(1, 1, D_R), zeros(1, CONV_W - 1, D_R), layers, nf,
        chunk=N_META, delta_bb=1, lru_tt=N_META, lru_bb=1, shared_state=False)
    y_prompt, pd, pdc, plr, plc = _run_group(
        x_prompt, md, mdc, ml, mlc, layers, nf,
        chunk=CHUNK, delta_bb=1, lru_tt=LRU_TIME_TILE, lru_bb=1, shared_state=True)
    y_sample, sdn, sdcn, sln, slcn = _run_group(
        x_sample, state_delta, state_delta_conv, state_lru.reshape(depth, bs, 1, D_R), state_lru_conv,
        layers, nf, chunk=dseq, delta_bb=SAMPLE_BATCH_TILE, lru_tt=dseq, lru_bb=2 * SAMPLE_BATCH_TILE,
        shared_state=False)

    stack = jnp.stack
    return (y_prompt, y_sample,
            stack(pd), stack(pdc), stack(plr).reshape(depth, bp, D_R), stack(plc),
            stack(sdn), stack(sdcn), stack(sln).reshape(depth, bs, D_R), stack(slcn))
```

```python
import functools
import math

import jax
import jax.numpy as jnp
from jax import lax
from jax.experimental import pallas as pl
from jax.experimental.pallas import tpu as pltpu

F32 = jnp.float32
BF16 = jnp.bfloat16

D_MODEL = 1024
D_FF = 2816
D_A = 512
D_R = 512
H_A = 4
DK = 128
DV = 128
CHUNK = 64
CONV_W = 4
NB_R = 8
C_RG = 8.0
EPS = 1e-6
N_META = 16
D_QKV = 3 * D_A
D_BA = 128
D_PROJ = D_QKV + D_A + 2 * D_R + D_BA

SUBLANES = 8
TAIL = SUBLANES
VMEM_BYTES_V7X = 64 * 1024 * 1024
VMEM_LIMIT = VMEM_BYTES_V7X - 8 * 1024 * 1024
ROW_TILE = 256
LRU_TIME_TILE = 256
PROMPT_CHUNKS_PER_STEP = 2
SAMPLE_TILES_PER_STEP = 2


def _rms(x, w):
    ms = jnp.mean(x * x, axis=-1, keepdims=True)
    return x * lax.rsqrt(ms + EPS) * w


def _sigmoid(x):
    return 1.0 / (1.0 + jnp.exp(-x))


def _silu(x):
    return x * _sigmoid(x)


def _softplus(x):
    return jnp.maximum(x, 0.0) + jnp.log1p(jnp.exp(-jnp.abs(x)))


def _dot(a, b):
    return jnp.dot(a.astype(BF16), b.astype(BF16), preferred_element_type=F32)


def _dot_nt(a, b):
    return lax.dot_general(a.astype(BF16), b.astype(BF16), (((1,), (1,)), ((), ())),
                           preferred_element_type=F32)


def _dot_tn(a, b):
    return lax.dot_general(a.astype(BF16), b.astype(BF16), (((0,), (0,)), ((), ())),
                           preferred_element_type=F32)


def _split3(a):
    a1 = a.astype(BF16)
    r = a - a1.astype(F32)
    a2 = r.astype(BF16)
    a3 = (r - a2.astype(F32)).astype(BF16)
    return a1, a2, a3


def _ffn(x, nw, wg_ref, wu_ref, wd_ref):
    h = _rms(x, nw).astype(BF16)
    g = jnp.dot(h, wg_ref[...], preferred_element_type=F32)
    u = jnp.dot(h, wu_ref[...], preferred_element_type=F32)
    a = (_silu(g) * u).astype(BF16)
    return x + 0.5 * jnp.dot(a, wd_ref[...], preferred_element_type=F32)


def _ffn_in_body(x_ref, n1_ref, wg_ref, wu_ref, wd_ref, nm_ref, win_ref,
                 x1_ref, qkv_ref, z_ref, xr_ref, gate_ref, ba_ref):
    x1 = _ffn(x_ref[...], n1_ref[...], wg_ref, wu_ref, wd_ref)
    x1_ref[...] = x1
    hm = _rms(x1, nm_ref[...]).astype(BF16)
    p = jnp.dot(hm, win_ref[...], preferred_element_type=F32)
    c1 = D_QKV
    c2 = c1 + D_A
    c3 = c2 + D_R
    c4 = c3 + D_R
    qkv_ref[...] = p[:, :c1]
    z_ref[...] = p[:, c1:c2]
    xr_ref[...] = p[:, c2:c3]
    gate_ref[...] = p[:, c3:c4]
    ba_ref[...] = p[:, c4:]


def _out_ffn_body(x_ref, o_ref, y_ref, wo1_ref, wo2_ref, n2_ref, wg_ref, wu_ref, wd_ref, nf_ref,
                  out_ref, *, final):
    x = x_ref[...] + (_dot(o_ref[...], wo1_ref[...]) + _dot(y_ref[...], wo2_ref[...]))
    x2 = _ffn(x, n2_ref[...], wg_ref, wu_ref, wd_ref)
    if final:
        x2 = _rms(x2, nf_ref[...])
    out_ref[...] = x2


def _resident(shape):
    nd = len(shape)
    return pl.BlockSpec(shape, lambda *_: (0,) * nd, pipeline_mode=pl.Buffered(1))


def _rows(tm, width):
    return pl.BlockSpec((tm, width), lambda i: (i, 0))


def _ffn_in(x, lw):
    r = x.shape[0]
    tm = min(ROW_TILE, r)
    widths = (D_MODEL, D_QKV, D_A, D_R, D_R, D_BA)
    return pl.pallas_call(
        _ffn_in_body,
        grid=(pl.cdiv(r, tm),),
        in_specs=[_rows(tm, D_MODEL), _resident((1, D_MODEL)),
                  _resident((D_MODEL, D_FF)), _resident((D_MODEL, D_FF)), _resident((D_FF, D_MODEL)),
                  _resident((1, D_MODEL)), _resident((D_MODEL, D_PROJ))],
        out_specs=[_rows(tm, w) for w in widths],
        out_shape=[jax.ShapeDtypeStruct((r, w), F32) for w in widths],
        compiler_params=pltpu.CompilerParams(dimension_semantics=("parallel",),
                                             vmem_limit_bytes=VMEM_LIMIT),
        name="ffn_in",
    )(x, lw["n1"], lw["f1g"], lw["f1u"], lw["f1d"], lw["nm"], lw["w_in"])


def _out_ffn(x, o, y, lw, nf, final):
    r = x.shape[0]
    tm = min(ROW_TILE, r)
    return pl.pallas_call(
        functools.partial(_out_ffn_body, final=final),
        grid=(pl.cdiv(r, tm),),
        in_specs=[_rows(tm, D_MODEL), _rows(tm, D_A), _rows(tm, D_R),
                  _resident((D_A, D_MODEL)), _resident((D_R, D_MODEL)), _resident((1, D_MODEL)),
                  _resident((D_MODEL, D_FF)), _resident((D_MODEL, D_FF)), _resident((D_FF, D_MODEL)),
                  _resident((1, D_MODEL))],
        out_specs=_rows(tm, D_MODEL),
        out_shape=jax.ShapeDtypeStruct((r, D_MODEL), F32),
        compiler_params=pltpu.CompilerParams(dimension_semantics=("parallel",),
                                             vmem_limit_bytes=VMEM_LIMIT),
        name="out_ffn",
    )(x, o, y, lw["wo1"], lw["wo2"], lw["n2"], lw["f2g"], lw["f2u"], lw["f2d"], nf)


def _delta_body(qkv_ref, z_ref, ba_ref, cs_ref, s0_ref, cw_ref, alog_ref, dtb_ref, on_ref,
                o_ref, snew_ref, csnew_ref, ext_ref, *, tiles, seg, tt, n_t):
    c = seg
    nb = tiles[0][1]
    r = nb * c
    t = pl.program_id(1)

    @pl.when(t == 0)
    def _():
        ext_ref[:, TAIL - (CONV_W - 1):TAIL, :] = cs_ref[...]
        snew_ref[...] = s0_ref[...]

    ext_ref[:, TAIL:TAIL + tt, :] = qkv_ref[...]

    levels = int(math.log2(c))
    assert 2 ** levels == c
    row = lax.broadcasted_iota(jnp.int32, (r, r), 0)
    col = lax.broadcasted_iota(jnp.int32, (r, r), 1)
    same = lax.shift_right_logical(row, levels) == lax.shift_right_logical(col, levels)
    causal = same & (row >= col)
    strict = same & (row > col)
    tri = causal.astype(BF16)
    ones_seg = same.astype(BF16)
    eye = (row == col).astype(F32)
    sel_r = lax.broadcasted_iota(jnp.int32, (SUBLANES, D_BA), 0)
    sel_c = lax.broadcasted_iota(jnp.int32, (SUBLANES, D_BA), 1)
    sel = ((sel_c == sel_r + H_A) & (sel_r < H_A)).astype(BF16)

    def rows(ref, tile, off, col0, width):
        b0, _, t0 = tile
        x = ref[b0:b0 + nb, off + t0:off + t0 + c, col0:col0 + width]
        return x.reshape(r, width)

    def conv_silu(tile, col0):
        acc = None
        for j in range(CONV_W):
            term = rows(ext_ref, tile, TAIL - (CONV_W - 1) + j, col0, DK) * cw_ref[j:j + 1, col0:col0 + DK]
            acc = term if acc is None else acc + term
        return _silu(acc)

    def exact_dot(lhs01, x, nt=False):
        out = None
        for p in _split3(x):
            if nt:
                d = lax.dot_general(lhs01, p, (((1,), (1,)), ((), ())), preferred_element_type=F32)
            else:
                d = jnp.dot(lhs01, p, preferred_element_type=F32)
            out = d if out is None else out + d
        return out

    items = [(j, h) for j in range(len(tiles)) for h in range(H_A)]

    gcum, gtot, grow, eg_all, beta_all = [], [], [], [], []
    for tile in tiles:
        ba = rows(ba_ref, tile, 0, 0, D_BA)
        g_all = -jnp.exp(alog_ref[...]) * _softplus(ba + dtb_ref[...])
        beta_all.append(_sigmoid(ba))
        gcum.append(exact_dot(tri, g_all))
        gtot.append(exact_dot(ones_seg, g_all))
    for j in range(len(tiles)):
        grow.append(exact_dot(sel, gcum[j], nt=True))
        eg_all.append(jnp.exp(gcum[j]))

    qg, kn, kb, kdec, decay, rhs = {}, {}, {}, {}, {}, {}
    for (j, h) in items:
        tile = tiles[j]
        qh = conv_silu(tile, h * DK)
        kh = conv_silu(tile, D_A + h * DK)
        vh = conv_silu(tile, 2 * D_A + h * DV)
        qn = qh * lax.rsqrt(jnp.sum(qh * qh, axis=-1, keepdims=True) + EPS) * (DK ** -0.5)
        k_n = kh * lax.rsqrt(jnp.sum(kh * kh, axis=-1, keepdims=True) + EPS)
        beta = beta_all[j][:, h:h + 1]
        gcol = gcum[j][:, H_A + h:H_A + h + 1]
        egcol = eg_all[j][:, H_A + h:H_A + h + 1]
        glast = gtot[j][:, H_A + h:H_A + h + 1]
        dmat = gcol - grow[j][h:h + 1, :]
        decay[j, h] = jnp.where(causal, jnp.exp(jnp.where(causal, dmat, 0.0)), 0.0)
        kn[j, h] = k_n
        kb[j, h] = k_n * beta
        qg[j, h] = (qn, qn * egcol)
        kdec[j, h] = k_n * jnp.exp(glast - gcol)
        rhs[j, h] = jnp.concatenate([vh * beta, kb[j, h] * egcol], axis=-1)

    kq = {it: _dot_nt(jnp.concatenate([kb[it], qg[it][0]], axis=0), kn[it]) for it in items}
    lmat = {it: jnp.where(strict, kq[it][:r] * decay[it], 0.0) for it in items}
    attn = {it: kq[it][r:] * decay[it] for it in items}

    tinv = {it: eye - lmat[it] for it in items}
    m = lmat
    for _ in range(levels - 1):
        m = {it: _dot(m[it], m[it]) for it in items}
        tinv = {it: tinv[it] + _dot(tinv[it], m[it]) for it in items}
    sol = {it: _dot(tinv[it], rhs[it]) for it in items}

    state = {}

    def get_state(b, h):
        if (b, h) not in state:
            state[b, h] = snew_ref[b, h]
        return state[b, h]

    for j, tile in enumerate(tiles):
        b0 = tile[0]
        segs = [(s, slice(s * c, (s + 1) * c)) for s in range(nb)]
        ws = {}
        for h in range(H_A):
            for s, rs in segs:
                lhs = jnp.concatenate([sol[j, h][rs, DV:], qg[j, h][1][rs]], axis=0)
                ws[h, s] = _dot(lhs, get_state(b0 + s, h))
        v_new, av = {}, {}
        for h in range(H_A):
            w_s = jnp.concatenate([ws[h, s][:c] for s, _ in segs], axis=0) if nb > 1 else ws[h, 0][:c]
            v_new[h] = sol[j, h][:, :DV] - w_s
        for h in range(H_A):
            av[h] = _dot(attn[j, h], v_new[h])
        for h in range(H_A):
            for s, rs in segs:
                scale = jnp.exp(gtot[j][rs.start:rs.start + 1, H_A + h:H_A + h + 1])
                state[b0 + s, h] = state[b0 + s, h] * scale + _dot_tn(kdec[j, h][rs], v_new[h][rs])
        for h in range(H_A):
            q_s = jnp.concatenate([ws[h, s][c:] for s, _ in segs], axis=0) if nb > 1 else ws[h, 0][c:]
            on = _rms(q_s + av[h], on_ref[...])
            out = on * _silu(rows(z_ref, tile, 0, h * DV, DV))
            o_ref[b0:b0 + nb, tile[2]:tile[2] + c, h * DV:(h + 1) * DV] = out.reshape(nb, c, DV)

    for (b, h), val in state.items():
        snew_ref[b, h] = val

    tail = ext_ref[:, tt:tt + TAIL, :]
    ext_ref[:, 0:TAIL, :] = tail

    @pl.when(t == n_t - 1)
    def _():
        csnew_ref[...] = ext_ref[:, TAIL - (CONV_W - 1):TAIL, :]


def _delta(qkv, z, ba, cs, s0, lw, *, seg, tt, bb, tiles, shared_state):
    b, t, _ = qkv.shape
    n_t = t // tt
    assert n_t * tt == t and b % bb == 0
    st = (lambda i, j: (0, 0, 0, 0)) if shared_state else (lambda i, j: (i, 0, 0, 0))
    ct = (lambda i, j: (0, 0, 0)) if shared_state else (lambda i, j: (i, 0, 0))
    tok = lambda w: pl.BlockSpec((bb, tt, w), lambda i, j: (i, j, 0))
    const = lambda shape: pl.BlockSpec(shape, lambda i, j: (0,) * len(shape))
    return pl.pallas_call(
        functools.partial(_delta_body, tiles=tiles, seg=seg, tt=tt, n_t=n_t),
        grid=(b // bb, n_t),
        in_specs=[tok(D_QKV), tok(D_A), tok(D_BA),
                  pl.BlockSpec((bb, CONV_W - 1, D_QKV), ct),
                  pl.BlockSpec((bb, H_A, DK, DV), st),
                  const((CONV_W, D_QKV)), const((1, D_BA)), const((1, D_BA)), const((1, DV))],
        out_specs=[tok(D_A),
                   pl.BlockSpec((bb, H_A, DK, DV), lambda i, j: (i, 0, 0, 0)),
                   pl.BlockSpec((bb, CONV_W - 1, D_QKV), lambda i, j: (i, 0, 0))],
        out_shape=[jax.ShapeDtypeStruct((b, t, D_A), F32),
                   jax.ShapeDtypeStruct((b, H_A, DK, DV), F32),
                   jax.ShapeDtypeStruct((b, CONV_W - 1, D_QKV), F32)],
        scratch_shapes=[pltpu.VMEM((bb, tt + TAIL, D_QKV), F32)],
        compiler_params=pltpu.CompilerParams(dimension_semantics=("arbitrary", "arbitrary"),
                                             vmem_limit_bytes=VMEM_LIMIT),
        name="delta",
    )(qkv, z, ba, cs, s0, lw["cqkv"], lw["alog"], lw["dtb"], lw["onorm"])


def _lru_body(xr_ref, gate_ref, cs_ref, h0_ref, cw_ref, cb_ref, wg_ref, bg_ref, lam_ref,
              y_ref, hnew_ref, csnew_ref, ext_ref, *, tt, bb, n_t):
    t = pl.program_id(1)

    @pl.when(t == 0)
    def _():
        ext_ref[:, TAIL - (CONV_W - 1):TAIL, :] = cs_ref[...]
        hnew_ref[...] = h0_ref[...]

    ext_ref[:, TAIL:TAIL + tt, :] = xr_ref[...]

    xc = None
    for j in range(CONV_W):
        off = TAIL - (CONV_W - 1) + j
        term = ext_ref[:, off:off + tt, :] * cw_ref[j:j + 1, :]
        xc = term if xc is None else xc + term
    xc = xc + cb_ref[...]
    n = bb * tt
    xc = xc.reshape(n, D_R)
    gates = jnp.dot(xc.astype(BF16), wg_ref[...], preferred_element_type=F32) + bg_ref[...]
    r = _sigmoid(gates[:, :D_R])
    i = _sigmoid(gates[:, D_R:])
    log_a = -C_RG * r * _softplus(-lam_ref[...])
    a = jnp.exp(log_a)
    u = jnp.sqrt(-jnp.tanh(log_a) * (a * a + 1.0)) * i * xc

    tpos = lax.broadcasted_iota(jnp.int32, (bb, tt, D_R), 1).reshape(n, D_R)
    s = 1
    while s < tt:
        a_sh = pltpu.roll(a, s, 0)
        u_sh = pltpu.roll(u, s, 0)
        m = tpos >= s
        u = jnp.where(m, u + a * u_sh, u)
        a = jnp.where(m, a * a_sh, a)
        s *= 2
    h0 = jnp.broadcast_to(hnew_ref[...], (bb, tt, D_R)).reshape(n, D_R)
    hs = u + a * h0
    y = hs * jax.nn.gelu(gate_ref[...].reshape(n, D_R), approximate=True)
    y_ref[...] = y.reshape(bb, tt, D_R)
    hnew_ref[...] = hs.reshape(bb, tt, D_R)[:, tt - 1:tt, :]

    tail = ext_ref[:, tt:tt + TAIL, :]
    ext_ref[:, 0:TAIL, :] = tail

    @pl.when(t == n_t - 1)
    def _():
        csnew_ref[...] = ext_ref[:, TAIL - (CONV_W - 1):TAIL, :]


def _lru(xr, gate, cs, h0, lw, *, tt, bb, shared_state):
    b, t, _ = xr.shape
    n_t = t // tt
    assert n_t * tt == t and b % bb == 0
    ct = (lambda i, j: (0, 0, 0)) if shared_state else (lambda i, j: (i, 0, 0))
    tok = pl.BlockSpec((bb, tt, D_R), lambda i, j: (i, j, 0))
    const = lambda shape: pl.BlockSpec(shape, lambda i, j: (0,) * len(shape))
    return pl.pallas_call(
        functools.partial(_lru_body, tt=tt, bb=bb, n_t=n_t),
        grid=(b // bb, n_t),
        in_specs=[tok, tok,
                  pl.BlockSpec((bb, CONV_W - 1, D_R), ct),
                  pl.BlockSpec((bb, 1, D_R), ct),
                  const((CONV_W, D_R)), const((1, D_R)), const((D_R, 2 * D_R)), const((1, 2 * D_R)),
                  const((1, D_R))],
        out_specs=[tok,
                   pl.BlockSpec((bb, 1, D_R), lambda i, j: (i, 0, 0)),
                   pl.BlockSpec((bb, CONV_W - 1, D_R), lambda i, j: (i, 0, 0))],
        out_shape=[jax.ShapeDtypeStruct((b, t, D_R), F32),
                   jax.ShapeDtypeStruct((b, 1, D_R), F32),
                   jax.ShapeDtypeStruct((b, CONV_W - 1, D_R), F32)],
        scratch_shapes=[pltpu.VMEM((bb, tt + TAIL, D_R), F32)],
        compiler_params=pltpu.CompilerParams(dimension_semantics=("arbitrary", "arbitrary"),
                                             vmem_limit_bytes=VMEM_LIMIT),
        name="lru",
    )(xr, gate, cs, h0, lw["clw"], lw["clb"], lw["wgate"], lw["bgate"], lw["lam"])


def _prep_layer(l, w):
    row = lambda v: v.reshape(1, -1).astype(F32)
    c1 = D_QKV
    c2 = c1 + D_A
    c3 = c2 + H_A
    c4 = c3 + H_A
    c5 = c4 + D_R
    w_in = w["w_in"][l]
    pad = jnp.zeros((D_MODEL, D_BA - 2 * H_A), F32)
    w_in_r = jnp.concatenate([w_in[:, :c2], w_in[:, c4:c5], w_in[:, c5:], w_in[:, c2:c4], pad], axis=1)
    lane_pad = lambda v: jnp.zeros((1, D_BA), F32).at[0, H_A:2 * H_A].set(v)
    eye = jnp.eye(NB_R, dtype=F32)
    bd = lambda wb: jnp.einsum("ncd,nm->ncmd", wb, eye).reshape(D_R, D_R)
    return dict(
        n1=row(w["norm_ffn1"][l]), f1g=w["w_ffn1_gate"][l].astype(BF16),
        f1u=w["w_ffn1_up"][l].astype(BF16), f1d=w["w_ffn1_down"][l].astype(BF16),
        nm=row(w["norm_mix"][l]), w_in=w_in_r.astype(BF16),
        cqkv=w["conv_qkv"][l], alog=lane_pad(w["a_log"][l]), dtb=lane_pad(w["dt_bias"][l]),
        onorm=row(w["norm_delta_out"][l]),
        clw=w["conv_lru_w"][l], clb=row(w["conv_lru_b"][l]),
        wgate=jnp.concatenate([bd(w["w_rgate"][l]), bd(w["w_igate"][l])], axis=1).astype(BF16),
        bgate=jnp.concatenate([row(w["b_rgate"][l]), row(w["b_igate"][l])], axis=1),
        lam=row(w["lru_lambda"][l]),
        wo1=w["w_out"][l][:D_A].astype(BF16), wo2=w["w_out"][l][D_A:].astype(BF16),
        n2=row(w["norm_ffn2"][l]), f2g=w["w_ffn2_gate"][l].astype(BF16),
        f2u=w["w_ffn2_up"][l].astype(BF16), f2d=w["w_ffn2_down"][l].astype(BF16),
    )


def _run_group(x, sd, sdc, sl, slc, layers, nf, *, seg, delta_tt, delta_bb, tiles, lru_tt, lru_bb,
               shared_state):
    b, t, _ = x.shape
    xf = x.reshape(b * t, D_MODEL)
    nd, ndc, nl, nlc = [], [], [], []
    n_layers = len(layers)
    for l, lw in enumerate(layers):
        x1, qkv, z, xr, gate, ba = _ffn_in(xf, lw)
        r3 = lambda v: v.reshape(b, t, v.shape[-1])
        o, s_new, cs_new = _delta(r3(qkv), r3(z), r3(ba), sdc[l], sd[l], lw,
                                  seg=seg, tt=delta_tt, bb=delta_bb, tiles=tiles, shared_state=shared_state)
        y, h_new, lcs_new = _lru(r3(xr), r3(gate), slc[l], sl[l], lw,
                                 tt=lru_tt, bb=lru_bb, shared_state=shared_state)
        xf = _out_ffn(x1, o.reshape(b * t, D_A), y.reshape(b * t, D_R), lw, nf, l == n_layers - 1)
        nd.append(s_new)
        ndc.append(cs_new)
        nl.append(h_new)
        nlc.append(lcs_new)
    return xf.reshape(b, t, D_MODEL), nd, ndc, nl, nlc


def kernel(x_prompt, x_sample, state_delta, state_delta_conv, state_lru, state_lru_conv, meta_tokens, norm_ffn1, w_ffn1_gate, w_ffn1_up, w_ffn1_down, norm_mix, w_in, conv_qkv, a_log, dt_bias, norm_delta_out, conv_lru_w, conv_lru_b, w_rgate, b_rgate, w_igate, b_igate, lru_lambda, w_out, norm_ffn2, w_ffn2_gate, w_ffn2_up, w_ffn2_down, norm_final):
    w = dict(norm_ffn1=norm_ffn1, w_ffn1_gate=w_ffn1_gate, w_ffn1_up=w_ffn1_up, w_ffn1_down=w_ffn1_down,
             norm_mix=norm_mix, w_in=w_in, conv_qkv=conv_qkv, a_log=a_log, dt_bias=dt_bias,
             norm_delta_out=norm_delta_out, conv_lru_w=conv_lru_w, conv_lru_b=conv_lru_b,
             w_rgate=w_rgate, b_rgate=b_rgate, w_igate=w_igate, b_igate=b_igate, lru_lambda=lru_lambda,
             w_out=w_out, norm_ffn2=norm_ffn2, w_ffn2_gate=w_ffn2_gate, w_ffn2_up=w_ffn2_up,
             w_ffn2_down=w_ffn2_down)
    depth = norm_ffn1.shape[0]
    layers = [_prep_layer(l, w) for l in range(depth)]
    nf = norm_final.reshape(1, D_MODEL).astype(F32)
    bp, seq, _ = x_prompt.shape
    bs, dseq, _ = x_sample.shape
    seg_b = CHUNK // dseq

    zeros = lambda *s: [jnp.zeros(s, F32)] * depth
    _, md, mdc, ml, mlc = _run_group(
        meta_tokens.astype(F32)[None], zeros(1, H_A, DK, DV), zeros(1, CONV_W - 1, D_QKV),
        zeros(1, 1, D_R), zeros(1, CONV_W - 1, D_R), layers, nf,
        seg=N_META, delta_tt=N_META, delta_bb=1, tiles=((0, 1, 0),), lru_tt=N_META, lru_bb=1,
        shared_state=False)
    y_prompt, pd, pdc, plr, plc = _run_group(
        x_prompt, md, mdc, ml, mlc, layers, nf,
        seg=CHUNK, delta_tt=PROMPT_CHUNKS_PER_STEP * CHUNK, delta_bb=1,
        tiles=tuple((0, 1, j * CHUNK) for j in range(PROMPT_CHUNKS_PER_STEP)),
        lru_tt=LRU_TIME_TILE, lru_bb=1, shared_state=True)
    y_sample, sdn, sdcn, sln, slcn = _run_group(
        x_sample, state_delta, state_delta_conv, state_lru.reshape(depth, bs, 1, D_R), state_lru_conv,
        layers, nf, seg=dseq, delta_tt=dseq, delta_bb=SAMPLE_TILES_PER_STEP * seg_b,
        tiles=tuple((j * seg_b, seg_b, 0) for j in range(SAMPLE_TILES_PER_STEP)),
        lru_tt=dseq, lru_bb=2 * seg_b, shared_state=False)

    stack = jnp.stack
    return (y_prompt, y_sample,
            stack(pd), stack(pdc), stack(plr).reshape(depth, bp, D_R), stack(plc),
            stack(sdn), stack(sdcn), stack(sln).reshape(depth, bs, D_R), stack(slcn))
```

```python
import functools
import math

import jax
import jax.numpy as jnp
from jax import lax
from jax.experimental import pallas as pl
from jax.experimental.pallas import tpu as pltpu

F32 = jnp.float32
BF16 = jnp.bfloat16

D_MODEL = 1024
D_FF = 2816
D_A = 512
D_R = 512
H_A = 4
DK = 128
DV = 128
CHUNK = 64
CONV_W = 4
NB_R = 8
C_RG = 8.0
EPS = 1e-6
N_META = 16
D_QKV = 3 * D_A
D_BA = 128
D_PROJ = D_QKV + D_A + 2 * D_R + D_BA

SUBLANES = 8
TAIL = SUBLANES
VMEM_BYTES_V7X = 64 * 1024 * 1024
VMEM_LIMIT = VMEM_BYTES_V7X - 8 * 1024 * 1024
ROW_TILE = 256
LRU_TIME_TILE = 256
PROMPT_CHUNKS_PER_STEP = 4
SAMPLE_TILES_PER_STEP = 2


def _rms(x, w):
    ms = jnp.mean(x * x, axis=-1, keepdims=True)
    return x * lax.rsqrt(ms + EPS) * w


def _sigmoid(x):
    return 1.0 / (1.0 + jnp.exp(-x))


def _silu(x):
    return x * _sigmoid(x)


def _softplus(x):
    return jnp.maximum(x, 0.0) + jnp.log1p(jnp.exp(-jnp.abs(x)))


def _dot(a, b):
    return jnp.dot(a.astype(BF16), b.astype(BF16), preferred_element_type=F32)


def _dot_nt(a, b):
    return lax.dot_general(a.astype(BF16), b.astype(BF16), (((1,), (1,)), ((), ())),
                           preferred_element_type=F32)


def _dot_tn(a, b):
    return lax.dot_general(a.astype(BF16), b.astype(BF16), (((0,), (0,)), ((), ())),
                           preferred_element_type=F32)


def _split3(a):
    a1 = a.astype(BF16)
    r = a - a1.astype(F32)
    a2 = r.astype(BF16)
    a3 = (r - a2.astype(F32)).astype(BF16)
    return a1, a2, a3


def _ffn(x, nw, wg_ref, wu_ref, wd_ref):
    h = _rms(x, nw).astype(BF16)
    g = jnp.dot(h, wg_ref[...], preferred_element_type=F32)
    u = jnp.dot(h, wu_ref[...], preferred_element_type=F32)
    a = (_silu(g) * u).astype(BF16)
    return x + 0.5 * jnp.dot(a, wd_ref[...], preferred_element_type=F32)


def _ffn_in_body(x_ref, n1_ref, wg_ref, wu_ref, wd_ref, nm_ref, win_ref,
                 x1_ref, qkv_ref, z_ref, xr_ref, gate_ref, ba_ref):
    x1 = _ffn(x_ref[...], n1_ref[...], wg_ref, wu_ref, wd_ref)
    x1_ref[...] = x1
    hm = _rms(x1, nm_ref[...]).astype(BF16)
    p = jnp.dot(hm, win_ref[...], preferred_element_type=F32)
    c1 = D_QKV
    c2 = c1 + D_A
    c3 = c2 + D_R
    c4 = c3 + D_R
    qkv_ref[...] = p[:, :c1]
    z_ref[...] = p[:, c1:c2]
    xr_ref[...] = p[:, c2:c3]
    gate_ref[...] = p[:, c3:c4]
    ba_ref[...] = p[:, c4:]


def _out_ffn_body(x_ref, o_ref, y_ref, wo1_ref, wo2_ref, n2_ref, wg_ref, wu_ref, wd_ref, nf_ref,
                  out_ref, *, final):
    x = x_ref[...] + (_dot(o_ref[...], wo1_ref[...]) + _dot(y_ref[...], wo2_ref[...]))
    x2 = _ffn(x, n2_ref[...], wg_ref, wu_ref, wd_ref)
    if final:
        x2 = _rms(x2, nf_ref[...])
    out_ref[...] = x2


def _resident(shape):
    nd = len(shape)
    return pl.BlockSpec(shape, lambda *_: (0,) * nd, pipeline_mode=pl.Buffered(1))


def _rows(tm, width):
    return pl.BlockSpec((tm, width), lambda i: (i, 0))


def _ffn_in(x, lw):
    r = x.shape[0]
    tm = min(ROW_TILE, r)
    widths = (D_MODEL, D_QKV, D_A, D_R, D_R, D_BA)
    return pl.pallas_call(
        _ffn_in_body,
        grid=(pl.cdiv(r, tm),),
        in_specs=[_rows(tm, D_MODEL), _resident((1, D_MODEL)),
                  _resident((D_MODEL, D_FF)), _resident((D_MODEL, D_FF)), _resident((D_FF, D_MODEL)),
                  _resident((1, D_MODEL)), _resident((D_MODEL, D_PROJ))],
        out_specs=[_rows(tm, w) for w in widths],
        out_shape=[jax.ShapeDtypeStruct((r, w), F32) for w in widths],
        compiler_params=pltpu.CompilerParams(dimension_semantics=("parallel",),
                                             vmem_limit_bytes=VMEM_LIMIT),
        name="ffn_in",
    )(x, lw["n1"], lw["f1g"], lw["f1u"], lw["f1d"], lw["nm"], lw["w_in"])


def _out_ffn(x, o, y, lw, nf, final):
    r = x.shape[0]
    tm = min(ROW_TILE, r)
    return pl.pallas_call(
        functools.partial(_out_ffn_body, final=final),
        grid=(pl.cdiv(r, tm),),
        in_specs=[_rows(tm, D_MODEL), _rows(tm, D_A), _rows(tm, D_R),
                  _resident((D_A, D_MODEL)), _resident((D_R, D_MODEL)), _resident((1, D_MODEL)),
                  _resident((D_MODEL, D_FF)), _resident((D_MODEL, D_FF)), _resident((D_FF, D_MODEL)),
                  _resident((1, D_MODEL))],
        out_specs=_rows(tm, D_MODEL),
        out_shape=jax.ShapeDtypeStruct((r, D_MODEL), F32),
        compiler_params=pltpu.CompilerParams(dimension_semantics=("parallel",),
                                             vmem_limit_bytes=VMEM_LIMIT),
        name="out_ffn",
    )(x, o, y, lw["wo1"], lw["wo2"], lw["n2"], lw["f2g"], lw["f2u"], lw["f2d"], nf)


def _delta_body(qkv_ref, z_ref, ba_ref, cs_ref, s0_ref, cw_ref, alog_ref, dtb_ref, on_ref,
                o_ref, snew_ref, csnew_ref, ext_ref, *, tiles, seg, tt, n_t):
    c = seg
    nb = tiles[0][1]
    r = nb * c
    t = pl.program_id(1)

    @pl.when(t == 0)
    def _():
        ext_ref[:, TAIL - (CONV_W - 1):TAIL, :] = cs_ref[...]
        snew_ref[...] = s0_ref[...]

    ext_ref[:, TAIL:TAIL + tt, :] = qkv_ref[...]

    levels = int(math.log2(c))
    assert 2 ** levels == c
    row = lax.broadcasted_iota(jnp.int32, (r, r), 0)
    col = lax.broadcasted_iota(jnp.int32, (r, r), 1)
    same = lax.shift_right_logical(row, levels) == lax.shift_right_logical(col, levels)
    causal = same & (row >= col)
    strict = same & (row > col)
    tri = causal.astype(BF16)
    ones_seg = same.astype(BF16)
    eye = (row == col).astype(F32)
    sel_r = lax.broadcasted_iota(jnp.int32, (SUBLANES, D_BA), 0)
    sel_c = lax.broadcasted_iota(jnp.int32, (SUBLANES, D_BA), 1)
    sel = ((sel_c == sel_r + H_A) & (sel_r < H_A)).astype(BF16)

    def rows(ref, tile, off, col0, width):
        b0, _, t0 = tile
        x = ref[b0:b0 + nb, off + t0:off + t0 + c, col0:col0 + width]
        return x.reshape(r, width)

    def conv_silu(tile, col0):
        acc = None
        for j in range(CONV_W):
            term = rows(ext_ref, tile, TAIL - (CONV_W - 1) + j, col0, DK) * cw_ref[j:j + 1, col0:col0 + DK]
            acc = term if acc is None else acc + term
        return _silu(acc)

    def exact_dot(lhs01, x, nt=False):
        out = None
        for p in _split3(x):
            if nt:
                d = lax.dot_general(lhs01, p, (((1,), (1,)), ((), ())), preferred_element_type=F32)
            else:
                d = jnp.dot(lhs01, p, preferred_element_type=F32)
            out = d if out is None else out + d
        return out

    items = [(j, h) for j in range(len(tiles)) for h in range(H_A)]

    gcum, gtot, grow, eg_all, beta_all = [], [], [], [], []
    for tile in tiles:
        ba = rows(ba_ref, tile, 0, 0, D_BA)
        g_all = -jnp.exp(alog_ref[...]) * _softplus(ba + dtb_ref[...])
        beta_all.append(_sigmoid(ba))
        gcum.append(exact_dot(tri, g_all))
        gtot.append(exact_dot(ones_seg, g_all) if nb > 1 else gcum[-1][r - 1:r, :])
    for j in range(len(tiles)):
        grow.append(exact_dot(sel, gcum[j], nt=True))
        eg_all.append(jnp.exp(gcum[j]))

    qg, kn, kb, kdec, decay, rhs = {}, {}, {}, {}, {}, {}
    for (j, h) in items:
        tile = tiles[j]
        qh = conv_silu(tile, h * DK)
        kh = conv_silu(tile, D_A + h * DK)
        vh = conv_silu(tile, 2 * D_A + h * DV)
        qn = qh * lax.rsqrt(jnp.sum(qh * qh, axis=-1, keepdims=True) + EPS) * (DK ** -0.5)
        k_n = kh * lax.rsqrt(jnp.sum(kh * kh, axis=-1, keepdims=True) + EPS)
        beta = beta_all[j][:, h:h + 1]
        gcol = gcum[j][:, H_A + h:H_A + h + 1]
        egcol = eg_all[j][:, H_A + h:H_A + h + 1]
        glast = gtot[j][:, H_A + h:H_A + h + 1]
        dmat = gcol - grow[j][h:h + 1, :]
        decay[j, h] = jnp.where(causal, jnp.exp(jnp.where(causal, dmat, 0.0)), 0.0)
        kn[j, h] = k_n
        kb[j, h] = k_n * beta
        qg[j, h] = (qn, qn * egcol)
        kdec[j, h] = k_n * jnp.exp(glast - gcol)
        rhs[j, h] = jnp.concatenate([vh * beta, kb[j, h] * egcol], axis=-1)

    kq = {it: _dot_nt(jnp.concatenate([kb[it], qg[it][0]], axis=0), kn[it]) for it in items}
    lmat = {it: jnp.where(strict, kq[it][:r] * decay[it], 0.0) for it in items}
    attn = {it: kq[it][r:] * decay[it] for it in items}

    tinv = {it: eye - lmat[it] for it in items}
    m = lmat
    for _ in range(levels - 1):
        m = {it: _dot(m[it], m[it]) for it in items}
        tinv = {it: tinv[it] + _dot(tinv[it], m[it]) for it in items}
    sol = {it: _dot(tinv[it], rhs[it]) for it in items}

    state = {}

    def get_state(b, h):
        if (b, h) not in state:
            state[b, h] = snew_ref[b, h]
        return state[b, h]

    for j, tile in enumerate(tiles):
        b0 = tile[0]
        segs = [(s, slice(s * c, (s + 1) * c)) for s in range(nb)]
        ws = {}
        for h in range(H_A):
            for s, rs in segs:
                lhs = jnp.concatenate([sol[j, h][rs, DV:], qg[j, h][1][rs]], axis=0)
                ws[h, s] = _dot(lhs, get_state(b0 + s, h))
        v_new, av = {}, {}
        for h in range(H_A):
            w_s = jnp.concatenate([ws[h, s][:c] for s, _ in segs], axis=0) if nb > 1 else ws[h, 0][:c]
            v_new[h] = sol[j, h][:, :DV] - w_s
        for h in range(H_A):
            av[h] = _dot(attn[j, h], v_new[h])
        for h in range(H_A):
            for s, rs in segs:
                g0 = rs.start if nb > 1 else 0
                scale = jnp.exp(gtot[j][g0:g0 + 1, H_A + h:H_A + h + 1])
                state[b0 + s, h] = state[b0 + s, h] * scale + _dot_tn(kdec[j, h][rs], v_new[h][rs])
        for h in range(H_A):
            q_s = jnp.concatenate([ws[h, s][c:] for s, _ in segs], axis=0) if nb > 1 else ws[h, 0][c:]
            on = _rms(q_s + av[h], on_ref[...])
            out = on * _silu(rows(z_ref, tile, 0, h * DV, DV))
            o_ref[b0:b0 + nb, tile[2]:tile[2] + c, h * DV:(h + 1) * DV] = out.reshape(nb, c, DV)

    for (b, h), val in state.items():
        snew_ref[b, h] = val

    tail = ext_ref[:, tt:tt + TAIL, :]
    ext_ref[:, 0:TAIL, :] = tail

    @pl.when(t == n_t - 1)
    def _():
        csnew_ref[...] = ext_ref[:, TAIL - (CONV_W - 1):TAIL, :]


def _delta(qkv, z, ba, cs, s0, lw, l, *, seg, tt, bb, tiles, shared_state):
    b, t, _ = qkv.shape
    n_t = t // tt
    assert n_t * tt == t and b % bb == 0
    st = (lambda i, j: (l, 0, 0, 0, 0)) if shared_state else (lambda i, j: (l, i, 0, 0, 0))
    ct = (lambda i, j: (l, 0, 0, 0)) if shared_state else (lambda i, j: (l, i, 0, 0))
    tok = lambda w: pl.BlockSpec((bb, tt, w), lambda i, j: (i, j, 0))
    const = lambda shape: pl.BlockSpec(shape, lambda i, j: (0,) * len(shape))
    return pl.pallas_call(
        functools.partial(_delta_body, tiles=tiles, seg=seg, tt=tt, n_t=n_t),
        grid=(b // bb, n_t),
        in_specs=[tok(D_QKV), tok(D_A), tok(D_BA),
                  pl.BlockSpec((None, bb, CONV_W - 1, D_QKV), ct),
                  pl.BlockSpec((None, bb, H_A, DK, DV), st),
                  const((CONV_W, D_QKV)), const((1, D_BA)), const((1, D_BA)), const((1, DV))],
        out_specs=[tok(D_A),
                   pl.BlockSpec((bb, H_A, DK, DV), lambda i, j: (i, 0, 0, 0)),
                   pl.BlockSpec((bb, CONV_W - 1, D_QKV), lambda i, j: (i, 0, 0))],
        out_shape=[jax.ShapeDtypeStruct((b, t, D_A), F32),
                   jax.ShapeDtypeStruct((b, H_A, DK, DV), F32),
                   jax.ShapeDtypeStruct((b, CONV_W - 1, D_QKV), F32)],
        scratch_shapes=[pltpu.VMEM((bb, tt + TAIL, D_QKV), F32)],
        compiler_params=pltpu.CompilerParams(dimension_semantics=("arbitrary", "arbitrary"),
                                             vmem_limit_bytes=VMEM_LIMIT),
        name="delta",
    )(qkv, z, ba, cs, s0, lw["cqkv"], lw["alog"], lw["dtb"], lw["onorm"])


def _lru_body(xr_ref, gate_ref, cs_ref, h0_ref, cw_ref, cb_ref, wg_ref, bg_ref, lam_ref,
              y_ref, hnew_ref, csnew_ref, ext_ref, *, tt, bb, n_t):
    t = pl.program_id(1)

    @pl.when(t == 0)
    def _():
        ext_ref[:, TAIL - (CONV_W - 1):TAIL, :] = cs_ref[...]
        hnew_ref[...] = h0_ref[...]

    ext_ref[:, TAIL:TAIL + tt, :] = xr_ref[...]

    xc = None
    for j in range(CONV_W):
        off = TAIL - (CONV_W - 1) + j
        term = ext_ref[:, off:off + tt, :] * cw_ref[j:j + 1, :]
        xc = term if xc is None else xc + term
    xc = xc + cb_ref[...]
    n = bb * tt
    xc = xc.reshape(n, D_R)
    gates = jnp.dot(xc.astype(BF16), wg_ref[...], preferred_element_type=F32) + bg_ref[...]
    r = _sigmoid(gates[:, :D_R])
    i = _sigmoid(gates[:, D_R:])
    log_a = -C_RG * r * _softplus(-lam_ref[...])
    a = jnp.exp(log_a)
    m2 = -jnp.tanh(log_a) * (a * a + 1.0)
    u = jnp.where(m2 > 0.0, m2 * lax.rsqrt(m2), 0.0) * i * xc

    ng = n // SUBLANES
    a = a.reshape(ng, SUBLANES, D_R)
    u = u.reshape(ng, SUBLANES, D_R)
    tpos = lax.broadcasted_iota(jnp.int32, (ng, SUBLANES, D_R), 1)
    s = 1
    while s < SUBLANES:
        a_sh = pltpu.roll(a, s, 1)
        u_sh = pltpu.roll(u, s, 1)
        m = tpos >= s
        u = jnp.where(m, u + a * u_sh, u)
        a = jnp.where(m, a * a_sh, a)
        s *= 2
    gpb = tt // SUBLANES
    if gpb == 1:
        hs = u + a * hnew_ref[...]
        h_last = hs[:, SUBLANES - 1:SUBLANES, :]
    else:
        assert bb == 1
        h_in = hnew_ref[0]
        groups = []
        for g in range(gpb):
            hg = u[g] + a[g] * h_in
            groups.append(hg)
            h_in = hg[SUBLANES - 1:SUBLANES, :]
        hs = jnp.concatenate(groups, axis=0)
        h_last = h_in.reshape(1, 1, D_R)
    y = hs.reshape(n, D_R) * jax.nn.gelu(gate_ref[...].reshape(n, D_R), approximate=True)
    y_ref[...] = y.reshape(bb, tt, D_R)
    hnew_ref[...] = h_last

    tail = ext_ref[:, tt:tt + TAIL, :]
    ext_ref[:, 0:TAIL, :] = tail

    @pl.when(t == n_t - 1)
    def _():
        csnew_ref[...] = ext_ref[:, TAIL - (CONV_W - 1):TAIL, :]


def _lru(xr, gate, cs, h0, lw, l, *, tt, bb, shared_state):
    b, t, _ = xr.shape
    n_t = t // tt
    assert n_t * tt == t and b % bb == 0 and tt % SUBLANES == 0
    ct = (lambda i, j: (l, 0, 0, 0)) if shared_state else (lambda i, j: (l, i, 0, 0))
    tok = pl.BlockSpec((bb, tt, D_R), lambda i, j: (i, j, 0))
    const = lambda shape: pl.BlockSpec(shape, lambda i, j: (0,) * len(shape))
    return pl.pallas_call(
        functools.partial(_lru_body, tt=tt, bb=bb, n_t=n_t),
        grid=(b // bb, n_t),
        in_specs=[tok, tok,
                  pl.BlockSpec((None, bb, CONV_W - 1, D_R), ct),
                  pl.BlockSpec((None, bb, 1, D_R), ct),
                  const((CONV_W, D_R)), const((1, D_R)), const((D_R, 2 * D_R)), const((1, 2 * D_R)),
                  const((1, D_R))],
        out_specs=[tok,
                   pl.BlockSpec((bb, 1, D_R), lambda i, j: (i, 0, 0)),
                   pl.BlockSpec((bb, CONV_W - 1, D_R), lambda i, j: (i, 0, 0))],
        out_shape=[jax.ShapeDtypeStruct((b, t, D_R), F32),
                   jax.ShapeDtypeStruct((b, 1, D_R), F32),
                   jax.ShapeDtypeStruct((b, CONV_W - 1, D_R), F32)],
        scratch_shapes=[pltpu.VMEM((bb, tt + TAIL, D_R), F32)],
        compiler_params=pltpu.CompilerParams(dimension_semantics=("arbitrary", "arbitrary"),
                                             vmem_limit_bytes=VMEM_LIMIT),
        name="lru",
    )(xr, gate, cs, h0, lw["clw"], lw["clb"], lw["wgate"], lw["bgate"], lw["lam"])


def _prep_layer(l, w):
    row = lambda v: v.reshape(1, -1).astype(F32)
    c1 = D_QKV
    c2 = c1 + D_A
    c3 = c2 + H_A
    c4 = c3 + H_A
    c5 = c4 + D_R
    w_in = w["w_in"][l]
    pad = jnp.zeros((D_MODEL, D_BA - 2 * H_A), F32)
    w_in_r = jnp.concatenate([w_in[:, :c2], w_in[:, c4:c5], w_in[:, c5:], w_in[:, c2:c4], pad], axis=1)
    lane_pad = lambda v: jnp.zeros((1, D_BA), F32).at[0, H_A:2 * H_A].set(v)
    eye = jnp.eye(NB_R, dtype=F32)
    bd = lambda wb: jnp.einsum("ncd,nm->ncmd", wb, eye).reshape(D_R, D_R)
    return dict(
        n1=row(w["norm_ffn1"][l]), f1g=w["w_ffn1_gate"][l].astype(BF16),
        f1u=w["w_ffn1_up"][l].astype(BF16), f1d=w["w_ffn1_down"][l].astype(BF16),
        nm=row(w["norm_mix"][l]), w_in=w_in_r.astype(BF16),
        cqkv=w["conv_qkv"][l], alog=lane_pad(w["a_log"][l]), dtb=lane_pad(w["dt_bias"][l]),
        onorm=row(w["norm_delta_out"][l]),
        clw=w["conv_lru_w"][l], clb=row(w["conv_lru_b"][l]),
        wgate=jnp.concatenate([bd(w["w_rgate"][l]), bd(w["w_igate"][l])], axis=1).astype(BF16),
        bgate=jnp.concatenate([row(w["b_rgate"][l]), row(w["b_igate"][l])], axis=1),
        lam=row(w["lru_lambda"][l]),
        wo1=w["w_out"][l][:D_A].astype(BF16), wo2=w["w_out"][l][D_A:].astype(BF16),
        n2=row(w["norm_ffn2"][l]), f2g=w["w_ffn2_gate"][l].astype(BF16),
        f2u=w["w_ffn2_up"][l].astype(BF16), f2d=w["w_ffn2_down"][l].astype(BF16),
    )


def _run_group(x, sd, sdc, sl, slc, layers, nf, *, seg, delta_tt, delta_bb, tiles, lru_tt, lru_bb,
               shared_state):
    b, t, _ = x.shape
    xf = x.reshape(b * t, D_MODEL)
    nd, ndc, nl, nlc = [], [], [], []
    n_layers = len(layers)
    for l, lw in enumerate(layers):
        x1, qkv, z, xr, gate, ba = _ffn_in(xf, lw)
        r3 = lambda v: v.reshape(b, t, v.shape[-1])
        o, s_new, cs_new = _delta(r3(qkv), r3(z), r3(ba), sdc, sd, lw, l,
                                  seg=seg, tt=delta_tt, bb=delta_bb, tiles=tiles, shared_state=shared_state)
        y, h_new, lcs_new = _lru(r3(xr), r3(gate), slc, sl, lw, l,
                                 tt=lru_tt, bb=lru_bb, shared_state=shared_state)
        xf = _out_ffn(x1, o.reshape(b * t, D_A), y.reshape(b * t, D_R), lw, nf, l == n_layers - 1)
        nd.append(s_new)
        ndc.append(cs_new)
        nl.append(h_new)
        nlc.append(lcs_new)
    stack = jnp.stack
    return xf.reshape(b, t, D_MODEL), stack(nd), stack(ndc), stack(nl), stack(nlc)


def kernel(x_prompt, x_sample, state_delta, state_delta_conv, state_lru, state_lru_conv, meta_tokens, norm_ffn1, w_ffn1_gate, w_ffn1_up, w_ffn1_down, norm_mix, w_in, conv_qkv, a_log, dt_bias, norm_delta_out, conv_lru_w, conv_lru_b, w_rgate, b_rgate, w_igate, b_igate, lru_lambda, w_out, norm_ffn2, w_ffn2_gate, w_ffn2_up, w_ffn2_down, norm_final):
    w = dict(norm_ffn1=norm_ffn1, w_ffn1_gate=w_ffn1_gate, w_ffn1_up=w_ffn1_up, w_ffn1_down=w_ffn1_down,
             norm_mix=norm_mix, w_in=w_in, conv_qkv=conv_qkv, a_log=a_log, dt_bias=dt_bias,
             norm_delta_out=norm_delta_out, conv_lru_w=conv_lru_w, conv_lru_b=conv_lru_b,
             w_rgate=w_rgate, b_rgate=b_rgate, w_igate=w_igate, b_igate=b_igate, lru_lambda=lru_lambda,
             w_out=w_out, norm_ffn2=norm_ffn2, w_ffn2_gate=w_ffn2_gate, w_ffn2_up=w_ffn2_up,
             w_ffn2_down=w_ffn2_down)
    depth = norm_ffn1.shape[0]
    layers = [_prep_layer(l, w) for l in range(depth)]
    nf = norm_final.reshape(1, D_MODEL).astype(F32)
    bp, seq, _ = x_prompt.shape
    bs, dseq, _ = x_sample.shape
    seg_b = CHUNK // dseq

    zeros = lambda *s: jnp.zeros((depth,) + s, F32)
    _, md, mdc, ml, mlc = _run_group(
        meta_tokens.astype(F32)[None], zeros(1, H_A, DK, DV), zeros(1, CONV_W - 1, D_QKV),
        zeros(1, 1, D_R), zeros(1, CONV_W - 1, D_R), layers, nf,
        seg=N_META, delta_tt=N_META, delta_bb=1, tiles=((0, 1, 0),), lru_tt=N_META, lru_bb=1,
        shared_state=False)
    y_prompt, pd, pdc, plr, plc = _run_group(
        x_prompt, md, mdc, ml, mlc, layers, nf,
        seg=CHUNK, delta_tt=PROMPT_CHUNKS_PER_STEP * CHUNK, delta_bb=1,
        tiles=tuple((0, 1, j * CHUNK) for j in range(PROMPT_CHUNKS_PER_STEP)),
        lru_tt=LRU_TIME_TILE, lru_bb=1, shared_state=True)
    y_sample, sdn, sdcn, sln, slcn = _run_group(
        x_sample, state_delta, state_delta_conv, state_lru.reshape(depth, bs, 1, D_R), state_lru_conv,
        layers, nf, seg=dseq, delta_tt=dseq, delta_bb=SAMPLE_TILES_PER_STEP * seg_b,
        tiles=tuple((j * seg_b, seg_b, 0) for j in range(SAMPLE_TILES_PER_STEP)),
        lru_tt=dseq, lru_bb=2 * seg_b, shared_state=False)

    return (y_prompt, y_sample, pd, pdc, plr.reshape(depth, bp, D_R), plc,
            sdn, sdcn, sln.reshape(depth, bs, D_R), slcn)
```

```python
import functools
import math

import jax
import jax.numpy as jnp
from jax import lax
from jax.experimental import pallas as pl
from jax.experimental.pallas import tpu as pltpu

F32 = jnp.float32
BF16 = jnp.bfloat16

D_MODEL = 1024
D_FF = 2816
D_A = 512
D_R = 512
H_A = 4
DK = 128
DV = 128
CHUNK = 64
CONV_W = 4
NB_R = 8
C_RG = 8.0
EPS = 1e-6
N_META = 16
D_QKV = 3 * D_A
D_BA = 128
D_PROJ = D_QKV + D_A + 2 * D_R + D_BA

SUBLANES = 8
TAIL = SUBLANES
VMEM_BYTES_V7X = 64 * 1024 * 1024
VMEM_LIMIT = VMEM_BYTES_V7X - 8 * 1024 * 1024
ROW_TILE = 256
MIX_TILE = 256
FF_PIECE = 256
PROJ_PIECE = 256
MIX_PLAN = "ad" * 7 + ("adl" + "ad") * 3 + "adl" + "ad" * 8
SAMPLE_TILES_PER_STEP = 2


def _rms(x, w):
    ms = jnp.mean(x * x, axis=-1, keepdims=True)
    return x * lax.rsqrt(ms + EPS) * w


def _sigmoid(x):
    return 1.0 / (1.0 + jnp.exp(-x))


def _silu(x):
    return x * _sigmoid(x)


def _softplus(x):
    return jnp.maximum(x, 0.0) + jnp.log1p(jnp.exp(-jnp.abs(x)))


def _dot(a, b):
    return jnp.dot(a.astype(BF16), b.astype(BF16), preferred_element_type=F32)


def _dot_nt(a, b):
    return lax.dot_general(a.astype(BF16), b.astype(BF16), (((1,), (1,)), ((), ())),
                           preferred_element_type=F32)


def _dot_tn(a, b):
    return lax.dot_general(a.astype(BF16), b.astype(BF16), (((0,), (0,)), ((), ())),
                           preferred_element_type=F32)


def _split3(a):
    a1 = a.astype(BF16)
    r = a - a1.astype(F32)
    a2 = r.astype(BF16)
    a3 = (r - a2.astype(F32)).astype(BF16)
    return a1, a2, a3


def _ffn(x, nw, wg_ref, wu_ref, wd_ref):
    h = _rms(x, nw).astype(BF16)
    g = jnp.dot(h, wg_ref[...], preferred_element_type=F32)
    u = jnp.dot(h, wu_ref[...], preferred_element_type=F32)
    a = (_silu(g) * u).astype(BF16)
    return x + 0.5 * jnp.dot(a, wd_ref[...], preferred_element_type=F32)


def _ffn_in_body(x_ref, n1_ref, wg_ref, wu_ref, wd_ref, nm_ref, win_ref,
                 x1_ref, qkv_ref, z_ref, xr_ref, gate_ref, ba_ref):
    x1 = _ffn(x_ref[...], n1_ref[...], wg_ref, wu_ref, wd_ref)
    x1_ref[...] = x1
    hm = _rms(x1, nm_ref[...]).astype(BF16)
    p = jnp.dot(hm, win_ref[...], preferred_element_type=F32)
    c1 = D_QKV
    c2 = c1 + D_A
    c3 = c2 + D_R
    c4 = c3 + D_R
    qkv_ref[...] = p[:, :c1]
    z_ref[...] = p[:, c1:c2]
    xr_ref[...] = p[:, c2:c3]
    gate_ref[...] = p[:, c3:c4]
    ba_ref[...] = p[:, c4:]


def _out_ffn_body(x_ref, o_ref, y_ref, wo1_ref, wo2_ref, n2_ref, wg_ref, wu_ref, wd_ref, nf_ref,
                  out_ref, *, final):
    x = x_ref[...] + (_dot(o_ref[...], wo1_ref[...]) + _dot(y_ref[...], wo2_ref[...]))
    x2 = _ffn(x, n2_ref[...], wg_ref, wu_ref, wd_ref)
    if final:
        x2 = _rms(x2, nf_ref[...])
    out_ref[...] = x2


def _resident(shape):
    nd = len(shape)
    return pl.BlockSpec(shape, lambda *_: (0,) * nd, pipeline_mode=pl.Buffered(1))


def _rows(tm, width):
    return pl.BlockSpec((tm, width), lambda i: (i, 0))


def _ffn_in(x, lw):
    r = x.shape[0]
    tm = min(ROW_TILE, r)
    widths = (D_MODEL, D_QKV, D_A, D_R, D_R, D_BA)
    return pl.pallas_call(
        _ffn_in_body,
        grid=(pl.cdiv(r, tm),),
        in_specs=[_rows(tm, D_MODEL), _resident((1, D_MODEL)),
                  _resident((D_MODEL, D_FF)), _resident((D_MODEL, D_FF)), _resident((D_FF, D_MODEL)),
                  _resident((1, D_MODEL)), _resident((D_MODEL, D_PROJ))],
        out_specs=[_rows(tm, w) for w in widths],
        out_shape=[jax.ShapeDtypeStruct((r, w), F32) for w in widths],
        compiler_params=pltpu.CompilerParams(dimension_semantics=("parallel",),
                                             vmem_limit_bytes=VMEM_LIMIT),
        name="ffn_in",
    )(x, lw["n1"], lw["f1g"], lw["f1u"], lw["f1d"], lw["nm"], lw["w_in"])


def _out_ffn(x, o, y, lw, nf, final):
    r = x.shape[0]
    tm = min(ROW_TILE, r)
    return pl.pallas_call(
        functools.partial(_out_ffn_body, final=final),
        grid=(pl.cdiv(r, tm),),
        in_specs=[_rows(tm, D_MODEL), _rows(tm, D_A), _rows(tm, D_R),
                  _resident((D_A, D_MODEL)), _resident((D_R, D_MODEL)), _resident((1, D_MODEL)),
                  _resident((D_MODEL, D_FF)), _resident((D_MODEL, D_FF)), _resident((D_FF, D_MODEL)),
                  _resident((1, D_MODEL))],
        out_specs=_rows(tm, D_MODEL),
        out_shape=jax.ShapeDtypeStruct((r, D_MODEL), F32),
        compiler_params=pltpu.CompilerParams(dimension_semantics=("parallel",),
                                             vmem_limit_bytes=VMEM_LIMIT),
        name="out_ffn",
    )(x, o, y, lw["wo1"], lw["wo2"], lw["n2"], lw["f2g"], lw["f2u"], lw["f2d"], nf)


class _Io:
    def __init__(self, **fns):
        self.after = lambda x: x
        self.__dict__.update(fns)


def _delta_stream(io, cw_ref, alog_ref, dtb_ref, on_ref, *, tiles, seg):
    c = seg
    nb = tiles[0][1]
    r = nb * c
    levels = int(math.log2(c))
    assert 2 ** levels == c
    row = lax.broadcasted_iota(jnp.int32, (r, r), 0)
    col = lax.broadcasted_iota(jnp.int32, (r, r), 1)
    same = lax.shift_right_logical(row, levels) == lax.shift_right_logical(col, levels)
    causal = same & (row >= col)
    strict = same & (row > col)
    tri = causal.astype(BF16)
    ones_seg = same.astype(BF16)
    eye = (row == col).astype(F32)
    sel_r = lax.broadcasted_iota(jnp.int32, (SUBLANES, D_BA), 0)
    sel_c = lax.broadcasted_iota(jnp.int32, (SUBLANES, D_BA), 1)
    sel = ((sel_c == sel_r + H_A) & (sel_r < H_A)).astype(BF16)

    def conv_silu(tile, col0):
        acc = None
        for j in range(CONV_W):
            term = io.ext(tile, TAIL - (CONV_W - 1) + j, col0, DK) * io.after(cw_ref[j:j + 1, col0:col0 + DK])
            acc = term if acc is None else acc + term
        return _silu(acc)

    def exact_dot(lhs01, x, nt=False):
        out = None
        for p in _split3(x):
            if nt:
                d = lax.dot_general(lhs01, p, (((1,), (1,)), ((), ())), preferred_element_type=F32)
            else:
                d = jnp.dot(lhs01, p, preferred_element_type=F32)
            out = d if out is None else out + d
        return out

    items = [(j, h) for j in range(len(tiles)) for h in range(H_A)]

    gcum, gtot, grow, eg_all, beta_all = [], [], [], [], []
    for tile in tiles:
        ba = io.ba(tile)
        g_all = -jnp.exp(io.after(alog_ref[...])) * _softplus(ba + dtb_ref[...])
        beta_all.append(_sigmoid(ba))
        gcum.append(exact_dot(tri, g_all))
        gtot.append(exact_dot(ones_seg, g_all) if nb > 1 else gcum[-1][r - 1:r, :])
    yield
    for j in range(len(tiles)):
        grow.append(exact_dot(sel, gcum[j], nt=True))
        eg_all.append(jnp.exp(gcum[j]))
    yield

    qg, kn, kb, kdec, decay, rhs = {}, {}, {}, {}, {}, {}
    for (j, h) in items:
        tile = tiles[j]
        qh = conv_silu(tile, h * DK)
        kh = conv_silu(tile, D_A + h * DK)
        vh = conv_silu(tile, 2 * D_A + h * DV)
        qn = qh * lax.rsqrt(jnp.sum(qh * qh, axis=-1, keepdims=True) + EPS) * (DK ** -0.5)
        k_n = kh * lax.rsqrt(jnp.sum(kh * kh, axis=-1, keepdims=True) + EPS)
        beta = beta_all[j][:, h:h + 1]
        gcol = gcum[j][:, H_A + h:H_A + h + 1]
        egcol = eg_all[j][:, H_A + h:H_A + h + 1]
        glast = gtot[j][:, H_A + h:H_A + h + 1]
        dmat = gcol - grow[j][h:h + 1, :]
        decay[j, h] = jnp.where(causal, jnp.exp(jnp.where(causal, dmat, 0.0)), 0.0)
        kn[j, h] = k_n
        kb[j, h] = k_n * beta
        qg[j, h] = (qn, qn * egcol)
        kdec[j, h] = k_n * jnp.exp(glast - gcol)
        rhs[j, h] = jnp.concatenate([vh * beta, kb[j, h] * egcol], axis=-1)
        if h == H_A - 1:
            yield

    kq = {it: _dot_nt(jnp.concatenate([kb[it], qg[it][0]], axis=0), kn[it]) for it in items}
    lmat = {it: jnp.where(strict, kq[it][:r] * decay[it], 0.0) for it in items}
    attn = {it: kq[it][r:] * decay[it] for it in items}
    yield

    tinv = {it: eye - lmat[it] for it in items}
    m = {it: _dot(lmat[it], lmat[it]) for it in items}
    yield
    for _ in range(levels - 2):
        tinv = {it: tinv[it] + _dot(tinv[it], m[it]) for it in items}
        m = {it: _dot(m[it], m[it]) for it in items}
        yield
    tinv = {it: tinv[it] + _dot(tinv[it], m[it]) for it in items}
    yield
    sol = {it: _dot(tinv[it], rhs[it]) for it in items}
    yield

    state = {}

    def get_state(b, h):
        if (b, h) not in state:
            state[b, h] = io.get_state(b, h)
        return state[b, h]

    for j, tile in enumerate(tiles):
        b0 = tile[0]
        segs = [(s, slice(s * c, (s + 1) * c)) for s in range(nb)]
        ws = {}
        for h in range(H_A):
            for s, rs in segs:
                lhs = jnp.concatenate([sol[j, h][rs, DV:], qg[j, h][1][rs]], axis=0)
                ws[h, s] = _dot(lhs, get_state(b0 + s, h))
        yield
        v_new, av = {}, {}
        for h in range(H_A):
            w_s = jnp.concatenate([ws[h, s][:c] for s, _ in segs], axis=0) if nb > 1 else ws[h, 0][:c]
            v_new[h] = sol[j, h][:, :DV] - w_s
        for h in range(H_A):
            av[h] = _dot(attn[j, h], v_new[h])
        for h in range(H_A):
            for s, rs in segs:
                g0 = rs.start if nb > 1 else 0
                scale = jnp.exp(gtot[j][g0:g0 + 1, H_A + h:H_A + h + 1])
                state[b0 + s, h] = state[b0 + s, h] * scale + _dot_tn(kdec[j, h][rs], v_new[h][rs])
        for h in range(H_A):
            q_s = jnp.concatenate([ws[h, s][c:] for s, _ in segs], axis=0) if nb > 1 else ws[h, 0][c:]
            on = _rms(q_s + av[h], io.after(on_ref[...]))
            io.put_o(tile, h, on * _silu(io.z(tile, h * DV)))
        if j == len(tiles) - 1:
            for (b, h), val in state.items():
                io.put_state(b, h, val)
        yield


def _delta_body(qkv_ref, z_ref, ba_ref, cs_ref, s0_ref, cw_ref, alog_ref, dtb_ref, on_ref,
                o_ref, snew_ref, csnew_ref, ext_ref, *, tiles, seg, tt, n_t):
    c = seg
    nb = tiles[0][1]
    r = nb * c
    t = pl.program_id(1)

    @pl.when(t == 0)
    def _():
        ext_ref[:, TAIL - (CONV_W - 1):TAIL, :] = cs_ref[...]
        snew_ref[...] = s0_ref[...]

    ext_ref[:, TAIL:TAIL + tt, :] = qkv_ref[...]

    def rows(ref, tile, off, col0, width):
        b0, _, t0 = tile
        return ref[b0:b0 + nb, off + t0:off + t0 + c, col0:col0 + width].reshape(r, width)

    def put_o(tile, h, val):
        b0, _, t0 = tile
        o_ref[b0:b0 + nb, t0:t0 + c, h * DV:(h + 1) * DV] = val.reshape(nb, c, DV)

    def put_state(b, h, val):
        snew_ref[b, h] = val

    io = _Io(ext=lambda tile, off, col0, w: rows(ext_ref, tile, off, col0, w),
             ba=lambda tile: rows(ba_ref, tile, 0, 0, D_BA),
             z=lambda tile, col0: rows(z_ref, tile, 0, col0, DV),
             put_o=put_o, get_state=lambda b, h: snew_ref[b, h], put_state=put_state)
    for _ in _delta_stream(io, cw_ref, alog_ref, dtb_ref, on_ref, tiles=tiles, seg=seg):
        pass

    tail = ext_ref[:, tt:tt + TAIL, :]
    ext_ref[:, 0:TAIL, :] = tail

    @pl.when(t == n_t - 1)
    def _():
        csnew_ref[...] = ext_ref[:, TAIL - (CONV_W - 1):TAIL, :]


def _delta(qkv, z, ba, cs, s0, lw, l, *, seg, tt, bb, tiles):
    b, t, _ = qkv.shape
    n_t = t // tt
    assert n_t * tt == t and b % bb == 0
    st = lambda i, j: (l, i, 0, 0, 0)
    ct = lambda i, j: (l, i, 0, 0)
    tok = lambda w: pl.BlockSpec((bb, tt, w), lambda i, j: (i, j, 0))
    const = lambda shape: pl.BlockSpec(shape, lambda i, j: (0,) * len(shape))
    return pl.pallas_call(
        functools.partial(_delta_body, tiles=tiles, seg=seg, tt=tt, n_t=n_t),
        grid=(b // bb, n_t),
        in_specs=[tok(D_QKV), tok(D_A), tok(D_BA),
                  pl.BlockSpec((None, bb, CONV_W - 1, D_QKV), ct),
                  pl.BlockSpec((None, bb, H_A, DK, DV), st),
                  const((CONV_W, D_QKV)), const((1, D_BA)), const((1, D_BA)), const((1, DV))],
        out_specs=[tok(D_A),
                   pl.BlockSpec((bb, H_A, DK, DV), lambda i, j: (i, 0, 0, 0)),
                   pl.BlockSpec((bb, CONV_W - 1, D_QKV), lambda i, j: (i, 0, 0))],
        out_shape=[jax.ShapeDtypeStruct((b, t, D_A), F32),
                   jax.ShapeDtypeStruct((b, H_A, DK, DV), F32),
                   jax.ShapeDtypeStruct((b, CONV_W - 1, D_QKV), F32)],
        scratch_shapes=[pltpu.VMEM((bb, tt + TAIL, D_QKV), F32)],
        compiler_params=pltpu.CompilerParams(dimension_semantics=("arbitrary", "arbitrary"),
                                             vmem_limit_bytes=VMEM_LIMIT),
        name="delta",
    )(qkv, z, ba, cs, s0, lw["cqkv"], lw["alog"], lw["dtb"], lw["onorm"])


def _lru_stream(io, cw_ref, cb_ref, wg_ref, bg_ref, lam_ref, *, tt, bb):
    xc = None
    for j in range(CONV_W):
        term = io.ext(TAIL - (CONV_W - 1) + j) * io.after(cw_ref[j:j + 1, :])
        xc = term if xc is None else xc + term
    xc = xc + cb_ref[...]
    n = bb * tt
    xc = xc.reshape(n, D_R)
    yield
    gates = jnp.dot(xc.astype(BF16), wg_ref[...], preferred_element_type=F32) + bg_ref[...]
    r = _sigmoid(gates[:, :D_R])
    i = _sigmoid(gates[:, D_R:])
    log_a = -C_RG * r * _softplus(-io.after(lam_ref[...]))
    a = jnp.exp(log_a)
    m2 = -jnp.tanh(log_a) * (a * a + 1.0)
    u = jnp.where(m2 > 0.0, m2 * lax.rsqrt(m2), 0.0) * i * xc
    yield

    ng = n // SUBLANES
    a = a.reshape(ng, SUBLANES, D_R)
    u = u.reshape(ng, SUBLANES, D_R)
    tpos = lax.broadcasted_iota(jnp.int32, (ng, SUBLANES, D_R), 1)
    s = 1
    while s < SUBLANES:
        a_sh = pltpu.roll(a, s, 1)
        u_sh = pltpu.roll(u, s, 1)
        m = tpos >= s
        u = jnp.where(m, u + a * u_sh, u)
        a = jnp.where(m, a * a_sh, a)
        s *= 2
    yield
    gpb = tt // SUBLANES
    if gpb == 1:
        hs = u + a * io.h0()
        h_last = hs[:, SUBLANES - 1:SUBLANES, :]
    else:
        assert bb == 1
        h_in = io.h0()[0]
        groups = []
        for g in range(gpb):
            hg = u[g] + a[g] * h_in
            groups.append(hg)
            h_in = hg[SUBLANES - 1:SUBLANES, :]
        hs = jnp.concatenate(groups, axis=0)
        h_last = h_in.reshape(1, 1, D_R)
    y = hs.reshape(n, D_R) * jax.nn.gelu(io.gate(), approximate=True)
    io.put_y(y.reshape(bb, tt, D_R))
    io.put_h(h_last)
    yield


def _lru_body(xr_ref, gate_ref, cs_ref, h0_ref, cw_ref, cb_ref, wg_ref, bg_ref, lam_ref,
              y_ref, hnew_ref, csnew_ref, ext_ref, *, tt, bb, n_t):
    t = pl.program_id(1)

    @pl.when(t == 0)
    def _():
        ext_ref[:, TAIL - (CONV_W - 1):TAIL, :] = cs_ref[...]
        hnew_ref[...] = h0_ref[...]

    ext_ref[:, TAIL:TAIL + tt, :] = xr_ref[...]

    def put_y(val):
        y_ref[...] = val

    def put_h(val):
        hnew_ref[...] = val

    io = _Io(ext=lambda off: ext_ref[:, off:off + tt, :], gate=lambda: gate_ref[...].reshape(bb * tt, D_R),
             h0=lambda: hnew_ref[...], put_y=put_y, put_h=put_h)
    for _ in _lru_stream(io, cw_ref, cb_ref, wg_ref, bg_ref, lam_ref, tt=tt, bb=bb):
        pass

    tail = ext_ref[:, tt:tt + TAIL, :]
    ext_ref[:, 0:TAIL, :] = tail

    @pl.when(t == n_t - 1)
    def _():
        csnew_ref[...] = ext_ref[:, TAIL - (CONV_W - 1):TAIL, :]


def _lru(xr, gate, cs, h0, lw, l, *, tt, bb):
    b, t, _ = xr.shape
    n_t = t // tt
    assert n_t * tt == t and b % bb == 0 and tt % SUBLANES == 0
    ct = lambda i, j: (l, i, 0, 0)
    tok = pl.BlockSpec((bb, tt, D_R), lambda i, j: (i, j, 0))
    const = lambda shape: pl.BlockSpec(shape, lambda i, j: (0,) * len(shape))
    return pl.pallas_call(
        functools.partial(_lru_body, tt=tt, bb=bb, n_t=n_t),
        grid=(b // bb, n_t),
        in_specs=[tok, tok,
                  pl.BlockSpec((None, bb, CONV_W - 1, D_R), ct),
                  pl.BlockSpec((None, bb, 1, D_R), ct),
                  const((CONV_W, D_R)), const((1, D_R)), const((D_R, 2 * D_R)), const((1, 2 * D_R)),
                  const((1, D_R))],
        out_specs=[tok,
                   pl.BlockSpec((bb, 1, D_R), lambda i, j: (i, 0, 0)),
                   pl.BlockSpec((bb, CONV_W - 1, D_R), lambda i, j: (i, 0, 0))],
        out_shape=[jax.ShapeDtypeStruct((b, t, D_R), F32),
                   jax.ShapeDtypeStruct((b, 1, D_R), F32),
                   jax.ShapeDtypeStruct((b, CONV_W - 1, D_R), F32)],
        scratch_shapes=[pltpu.VMEM((bb, tt + TAIL, D_R), F32)],
        compiler_params=pltpu.CompilerParams(dimension_semantics=("arbitrary", "arbitrary"),
                                             vmem_limit_bytes=VMEM_LIMIT),
        name="lru",
    )(xr, gate, cs, h0, lw["clw"], lw["clb"], lw["wgate"], lw["bgate"], lw["lam"])


def _interleave(streams, plan, lead, set_dep):
    live = dict(streams)

    def step(key):
        if key not in live:
            return
        try:
            val = next(live[key])
            if key == lead:
                set_dep(val)
        except StopIteration:
            del live[key]

    for key in plan:
        step(key)
    while live:
        for key in list(live):
            step(key)


def _in_mix_body(flag_ref, x_ref, n1_ref, wg_ref, wu_ref, wd_ref, nm_ref, win_ref,
                 dcs_ref, ds0_ref, cw_ref, alog_ref, dtb_ref, on_ref,
                 lcs_ref, lh0_ref, lcw_ref, lcb_ref, lwg_ref, lbg_ref, lam_ref,
                 x1_ref, o_ref, y_ref, snew_ref, dcsnew_ref, hnew_ref, lcsnew_ref,
                 qkv_scr, z_scr, ba_scr, xr_scr, gate_scr, dtail_scr, ltail_scr, s_scr, h_scr,
                 *, tiles_per_seq):
    tt = MIX_TILE
    s = pl.program_id(0)
    slot_a = lax.rem(s, 2)
    slot_b = 1 - slot_a
    first = lax.rem(s + tiles_per_seq - 1, tiles_per_seq) == 0
    tails = slice(TAIL - (CONV_W - 1), TAIL)
    keep = flag_ref[0] == 1
    dep = [None]

    def after(x):
        if dep[0] is None:
            return x
        reps = x.shape[-1] // DK
        d = dep[0] if reps == 1 else jnp.concatenate([dep[0]] * reps, axis=-1)
        return jnp.where(keep, x, d)

    def set_dep(val):
        dep[0] = val

    @pl.when(s == 0)
    def _():
        for ref in (qkv_scr, z_scr, ba_scr, xr_scr, gate_scr, dtail_scr, ltail_scr, s_scr, h_scr):
            ref[...] = jnp.zeros(ref.shape, F32)

    def stream_a():
        x = x_ref[...]
        h = _rms(x, n1_ref[...]).astype(BF16)

        def down(gu, acc):
            g, u, c0 = gu
            a = (_silu(g) * u).astype(BF16)
            d = jnp.dot(a, wd_ref[c0:c0 + FF_PIECE, :], preferred_element_type=F32)
            return d if acc is None else acc + d

        acc, pending = None, None
        for c0 in range(0, D_FF, FF_PIECE):
            g = jnp.dot(h, wg_ref[:, c0:c0 + FF_PIECE], preferred_element_type=F32)
            u = jnp.dot(h, wu_ref[:, c0:c0 + FF_PIECE], preferred_element_type=F32)
            if pending is not None:
                acc = down(pending, acc)
            pending = (g, u, c0)
            yield g[0:1, 0:DK]
        acc = down(pending, acc)
        x1 = x + 0.5 * acc
        x1_ref[...] = x1
        hm = _rms(x1, nm_ref[...]).astype(BF16)
        yield acc[0:1, 0:DK]
        dsts = ((qkv_scr, TAIL, 0, D_QKV), (z_scr, 0, D_QKV, D_A), (xr_scr, TAIL, D_QKV + D_A, D_R),
                (gate_scr, 0, D_QKV + D_A + D_R, D_R), (ba_scr, 0, D_QKV + D_A + 2 * D_R, D_BA))
        for dst, row0, col0, width in dsts:
            for d0 in range(0, width, PROJ_PIECE):
                w = min(PROJ_PIECE, width - d0)
                p = jnp.dot(hm, win_ref[:, col0 + d0:col0 + d0 + w], preferred_element_type=F32)
                dst[slot_a, row0:row0 + tt, d0:d0 + w] = p
                yield p[0:1, 0:DK]

    qkv_scr[slot_b, tails, :] = jnp.where(first, dcs_ref[0], dtail_scr[tails, :])
    xr_scr[slot_b, tails, :] = jnp.where(first, lcs_ref[0], ltail_scr[tails, :])

    def put_o(tile, h, val):
        o_ref[tile[2]:tile[2] + CHUNK, h * DV:(h + 1) * DV] = val

    def put_state(b, h, val):
        s_scr[h] = val
        snew_ref[0, h] = val

    def put_y(val):
        y_ref[...] = val.reshape(tt, D_R)

    def put_h(val):
        h_scr[...] = val
        hnew_ref[...] = val

    dio = _Io(ext=lambda tile, off, col0, w: qkv_scr[slot_b, off + tile[2]:off + tile[2] + CHUNK, col0:col0 + w],
              ba=lambda tile: ba_scr[slot_b, tile[2]:tile[2] + CHUNK, :],
              z=lambda tile, col0: z_scr[slot_b, tile[2]:tile[2] + CHUNK, col0:col0 + DV],
              put_o=put_o, get_state=lambda b, h: jnp.where(first, ds0_ref[0, h], s_scr[h]),
              put_state=put_state, after=after)
    lio = _Io(ext=lambda off: xr_scr[slot_b, off:off + tt, :].reshape(1, tt, D_R),
              gate=lambda: gate_scr[slot_b], h0=lambda: jnp.where(first, lh0_ref[...], h_scr[...]),
              put_y=put_y, put_h=put_h, after=after)
    tiles = tuple((0, 1, j * CHUNK) for j in range(tt // CHUNK))
    _interleave({"a": stream_a(),
                 "d": _delta_stream(dio, cw_ref, alog_ref, dtb_ref, on_ref, tiles=tiles, seg=CHUNK),
                 "l": _lru_stream(lio, lcw_ref, lcb_ref, lwg_ref, lbg_ref, lam_ref, tt=tt, bb=1)},
                MIX_PLAN, "a", set_dep)

    dtail_scr[...] = qkv_scr[slot_b, tt:tt + TAIL, :]
    ltail_scr[...] = xr_scr[slot_b, tt:tt + TAIL, :]
    dcsnew_ref[0] = qkv_scr[slot_b, tt + TAIL - (CONV_W - 1):tt + TAIL, :]
    lcsnew_ref[0] = xr_scr[slot_b, tt + TAIL - (CONV_W - 1):tt + TAIL, :]


def _in_mix(x, sdc, sd, slc, sl, lw, l, *, batch):
    r = x.shape[0]
    tt = MIX_TILE
    n_tiles = r // tt
    tiles_per_seq = n_tiles // batch
    assert n_tiles * tt == r and tiles_per_seq * batch == n_tiles
    cur = lambda s: (jnp.minimum(s, n_tiles - 1), 0)
    prev = lambda s: (jnp.maximum(s - 1, 0), 0)
    seq3 = lambda s: (jnp.maximum(s - 1, 0) // tiles_per_seq, 0, 0)
    seq4 = lambda s: (jnp.maximum(s - 1, 0) // tiles_per_seq, 0, 0, 0)
    layer = lambda shape: pl.BlockSpec((None,) + shape, lambda s: (l,) + (0,) * len(shape),
                                       pipeline_mode=pl.Buffered(1))
    return pl.pallas_call(
        functools.partial(_in_mix_body, tiles_per_seq=tiles_per_seq),
        grid=(n_tiles + 1,),
        in_specs=[pl.BlockSpec(memory_space=pltpu.SMEM),
                  pl.BlockSpec((tt, D_MODEL), cur), _resident((1, D_MODEL)),
                  _resident((D_MODEL, D_FF)), _resident((D_MODEL, D_FF)), _resident((D_FF, D_MODEL)),
                  _resident((1, D_MODEL)), _resident((D_MODEL, D_PROJ)),
                  layer((1, CONV_W - 1, D_QKV)), layer((1, H_A, DK, DV)),
                  _resident((CONV_W, D_QKV)), _resident((1, D_BA)), _resident((1, D_BA)), _resident((1, DV)),
                  layer((1, CONV_W - 1, D_R)), layer((1, 1, D_R)),
                  _resident((CONV_W, D_R)), _resident((1, D_R)), _resident((D_R, 2 * D_R)),
                  _resident((1, 2 * D_R)), _resident((1, D_R))],
        out_specs=[pl.BlockSpec((tt, D_MODEL), cur), pl.BlockSpec((tt, D_A), prev), pl.BlockSpec((tt, D_R), prev),
                   pl.BlockSpec((1, H_A, DK, DV), seq4), pl.BlockSpec((1, CONV_W - 1, D_QKV), seq3),
                   pl.BlockSpec((1, 1, D_R), seq3), pl.BlockSpec((1, CONV_W - 1, D_R), seq3)],
        out_shape=[jax.ShapeDtypeStruct((r, D_MODEL), F32), jax.ShapeDtypeStruct((r, D_A), F32),
                   jax.ShapeDtypeStruct((r, D_R), F32),
                   jax.ShapeDtypeStruct((batch, H_A, DK, DV), F32),
                   jax.ShapeDtypeStruct((batch, CONV_W - 1, D_QKV), F32),
                   jax.ShapeDtypeStruct((batch, 1, D_R), F32),
                   jax.ShapeDtypeStruct((batch, CONV_W - 1, D_R), F32)],
        scratch_shapes=[pltpu.VMEM((2, TAIL + tt, D_QKV), F32), pltpu.VMEM((2, tt, D_A), F32),
                        pltpu.VMEM((2, tt, D_BA), F32), pltpu.VMEM((2, TAIL + tt, D_R), F32),
                        pltpu.VMEM((2, tt, D_R), F32), pltpu.VMEM((TAIL, D_QKV), F32),
                        pltpu.VMEM((TAIL, D_R), F32), pltpu.VMEM((H_A, DK, DV), F32),
                        pltpu.VMEM((1, 1, D_R), F32)],
        compiler_params=pltpu.CompilerParams(dimension_semantics=("arbitrary",),
                                             vmem_limit_bytes=VMEM_LIMIT),
        name="in_mix",
    )(jnp.ones((1,), jnp.int32), x, lw["n1"], lw["f1g"], lw["f1u"], lw["f1d"], lw["nm"], lw["w_in"],
      sdc, sd, lw["cqkv"], lw["alog"], lw["dtb"], lw["onorm"],
      slc, sl, lw["clw"], lw["clb"], lw["wgate"], lw["bgate"], lw["lam"])


def _prep_layer(l, w):
    row = lambda v: v.reshape(1, -1).astype(F32)
    c1 = D_QKV
    c2 = c1 + D_A
    c3 = c2 + H_A
    c4 = c3 + H_A
    c5 = c4 + D_R
    w_in = w["w_in"][l]
    pad = jnp.zeros((D_MODEL, D_BA - 2 * H_A), F32)
    w_in_r = jnp.concatenate([w_in[:, :c2], w_in[:, c4:c5], w_in[:, c5:], w_in[:, c2:c4], pad], axis=1)
    lane_pad = lambda v: jnp.zeros((1, D_BA), F32).at[0, H_A:2 * H_A].set(v)
    eye = jnp.eye(NB_R, dtype=F32)
    bd = lambda wb: jnp.einsum("ncd,nm->ncmd", wb, eye).reshape(D_R, D_R)
    return dict(
        n1=row(w["norm_ffn1"][l]), f1g=w["w_ffn1_gate"][l].astype(BF16),
        f1u=w["w_ffn1_up"][l].astype(BF16), f1d=w["w_ffn1_down"][l].astype(BF16),
        nm=row(w["norm_mix"][l]), w_in=w_in_r.astype(BF16),
        cqkv=w["conv_qkv"][l], alog=lane_pad(w["a_log"][l]), dtb=lane_pad(w["dt_bias"][l]),
        onorm=row(w["norm_delta_out"][l]),
        clw=w["conv_lru_w"][l], clb=row(w["conv_lru_b"][l]),
        wgate=jnp.concatenate([bd(w["w_rgate"][l]), bd(w["w_igate"][l])], axis=1).astype(BF16),
        bgate=jnp.concatenate([row(w["b_rgate"][l]), row(w["b_igate"][l])], axis=1),
        lam=row(w["lru_lambda"][l]),
        wo1=w["w_out"][l][:D_A].astype(BF16), wo2=w["w_out"][l][D_A:].astype(BF16),
        n2=row(w["norm_ffn2"][l]), f2g=w["w_ffn2_gate"][l].astype(BF16),
        f2u=w["w_ffn2_up"][l].astype(BF16), f2d=w["w_ffn2_down"][l].astype(BF16),
    )


def _run_group(x, sd, sdc, sl, slc, layers, nf, *, seg=None, delta_tt=None, delta_bb=None, tiles=None,
               lru_tt=None, lru_bb=None, fused=False):
    b, t, _ = x.shape
    xf = x.reshape(b * t, D_MODEL)
    nd, ndc, nl, nlc = [], [], [], []
    n_layers = len(layers)
    for l, lw in enumerate(layers):
        if fused:
            x1, o, y, s_new, cs_new, h_new, lcs_new = _in_mix(xf, sdc, sd, slc, sl, lw, l, batch=b)
        else:
            x1, qkv, z, xr, gate, ba = _ffn_in(xf, lw)
            r3 = lambda v: v.reshape(b, t, v.shape[-1])
            o, s_new, cs_new = _delta(r3(qkv), r3(z), r3(ba), sdc, sd, lw, l,
                                      seg=seg, tt=delta_tt, bb=delta_bb, tiles=tiles)
            y, h_new, lcs_new = _lru(r3(xr), r3(gate), slc, sl, lw, l, tt=lru_tt, bb=lru_bb)
        xf = _out_ffn(x1, o.reshape(b * t, D_A), y.reshape(b * t, D_R), lw, nf, l == n_layers - 1)
        nd.append(s_new)
        ndc.append(cs_new)
        nl.append(h_new)
        nlc.append(lcs_new)
    stack = jnp.stack
    return xf.reshape(b, t, D_MODEL), stack(nd), stack(ndc), stack(nl), stack(nlc)


def kernel(x_prompt, x_sample, state_delta, state_delta_conv, state_lru, state_lru_conv, meta_tokens, norm_ffn1, w_ffn1_gate, w_ffn1_up, w_ffn1_down, norm_mix, w_in, conv_qkv, a_log, dt_bias, norm_delta_out, conv_lru_w, conv_lru_b, w_rgate, b_rgate, w_igate, b_igate, lru_lambda, w_out, norm_ffn2, w_ffn2_gate, w_ffn2_up, w_ffn2_down, norm_final):
    w = dict(norm_ffn1=norm_ffn1, w_ffn1_gate=w_ffn1_gate, w_ffn1_up=w_ffn1_up, w_ffn1_down=w_ffn1_down,
             norm_mix=norm_mix, w_in=w_in, conv_qkv=conv_qkv, a_log=a_log, dt_bias=dt_bias,
             norm_delta_out=norm_delta_out, conv_lru_w=conv_lru_w, conv_lru_b=conv_lru_b,
             w_rgate=w_rgate, b_rgate=b_rgate, w_igate=w_igate, b_igate=b_igate, lru_lambda=lru_lambda,
             w_out=w_out, norm_ffn2=norm_ffn2, w_ffn2_gate=w_ffn2_gate, w_ffn2_up=w_ffn2_up,
             w_ffn2_down=w_ffn2_down)
    depth = norm_ffn1.shape[0]
    layers = [_prep_layer(l, w) for l in range(depth)]
    nf = norm_final.reshape(1, D_MODEL).astype(F32)
    bp, seq, _ = x_prompt.shape
    bs, dseq, _ = x_sample.shape
    seg_b = CHUNK // dseq

    zeros = lambda *s: jnp.zeros((depth,) + s, F32)
    _, md, mdc, ml, mlc = _run_group(
        meta_tokens.astype(F32)[None], zeros(1, H_A, DK, DV), zeros(1, CONV_W - 1, D_QKV),
        zeros(1, 1, D_R), zeros(1, CONV_W - 1, D_R), layers, nf,
        seg=N_META, delta_tt=N_META, delta_bb=1, tiles=((0, 1, 0),), lru_tt=N_META, lru_bb=1)
    y_prompt, pd, pdc, plr, plc = _run_group(x_prompt, md, mdc, ml, mlc, layers, nf, fused=True)
    y_sample, sdn, sdcn, sln, slcn = _run_group(
        x_sample, state_delta, state_delta_conv, state_lru.reshape(depth, bs, 1, D_R), state_lru_conv,
        layers, nf, seg=dseq, delta_tt=dseq, delta_bb=SAMPLE_TILES_PER_STEP * seg_b,
        tiles=tuple((j * seg_b, seg_b, 0) for j in range(SAMPLE_TILES_PER_STEP)),
        lru_tt=dseq, lru_bb=2 * seg_b)

    return (y_prompt, y_sample, pd, pdc, plr.reshape(depth, bp, D_R), plc,
            sdn, sdcn, sln.reshape(depth, bs, D_R), slcn)
```

```python
import functools
import math

import jax
import jax.numpy as jnp
from jax import lax
from jax.experimental import pallas as pl
from jax.experimental.pallas import tpu as pltpu

F32 = jnp.float32
BF16 = jnp.bfloat16

D_MODEL = 1024
D_FF = 2816
D_A = 512
D_R = 512
H_A = 4
DK = 128
DV = 128
CHUNK = 64
CONV_W = 4
NB_R = 8
C_RG = 8.0
EPS = 1e-6
N_META = 16
D_QKV = 3 * D_A
D_BA = 128
D_PROJ = D_QKV + D_A + 2 * D_R + D_BA

SUBLANES = 8
TAIL = SUBLANES
VMEM_BYTES_V7X = 64 * 1024 * 1024
VMEM_LIMIT = VMEM_BYTES_V7X - 8 * 1024 * 1024
ROW_TILE = 256
MIX_TILE = 256
FF_PIECE = 256
PROJ_PIECE = 256
MIX_PLAN = "ad" * 7 + ("adl" + "ad") * 3 + "adl" + "ad" * 8
SAMPLE_TILES_PER_STEP = 2


def _rms(x, w):
    ms = jnp.mean(x * x, axis=-1, keepdims=True)
    return x * lax.rsqrt(ms + EPS) * w


def _sigmoid(x):
    return 1.0 / (1.0 + jnp.exp(-x))


def _silu(x):
    return x * _sigmoid(x)


def _softplus(x):
    t = jnp.exp(-jnp.abs(x))
    u = 1.0 + t
    d = u - 1.0
    log1p_t = jnp.where(d == 0.0, t, jnp.log(u) * (t / jnp.where(d == 0.0, 1.0, d)))
    return jnp.maximum(x, 0.0) + log1p_t


def _dot(a, b):
    return jnp.dot(a.astype(BF16), b.astype(BF16), preferred_element_type=F32)


def _dot_nt(a, b):
    return lax.dot_general(a.astype(BF16), b.astype(BF16), (((1,), (1,)), ((), ())),
                           preferred_element_type=F32)


def _dot_tn(a, b):
    return lax.dot_general(a.astype(BF16), b.astype(BF16), (((0,), (0,)), ((), ())),
                           preferred_element_type=F32)


def _split3(a):
    a1 = a.astype(BF16)
    r = a - a1.astype(F32)
    a2 = r.astype(BF16)
    a3 = (r - a2.astype(F32)).astype(BF16)
    return a1, a2, a3


def _ffn(x, nw, wg_ref, wu_ref, wd_ref):
    h = _rms(x, nw).astype(BF16)
    g = jnp.dot(h, wg_ref[...], preferred_element_type=F32)
    u = jnp.dot(h, wu_ref[...], preferred_element_type=F32)
    a = (_silu(g) * u).astype(BF16)
    return x + 0.5 * jnp.dot(a, wd_ref[...], preferred_element_type=F32)


def _ffn_in_body(x_ref, n1_ref, wg_ref, wu_ref, wd_ref, nm_ref, win_ref,
                 x1_ref, qkv_ref, z_ref, xr_ref, gate_ref, ba_ref):
    x1 = _ffn(x_ref[...], n1_ref[...], wg_ref, wu_ref, wd_ref)
    x1_ref[...] = x1
    hm = _rms(x1, nm_ref[...]).astype(BF16)
    p = jnp.dot(hm, win_ref[...], preferred_element_type=F32)
    c1 = D_QKV
    c2 = c1 + D_A
    c3 = c2 + D_R
    c4 = c3 + D_R
    qkv_ref[...] = p[:, :c1]
    z_ref[...] = p[:, c1:c2]
    xr_ref[...] = p[:, c2:c3]
    gate_ref[...] = p[:, c3:c4]
    ba_ref[...] = p[:, c4:]


def _out_ffn_body(x_ref, o_ref, y_ref, wo1_ref, wo2_ref, n2_ref, wg_ref, wu_ref, wd_ref, nf_ref,
                  out_ref, *, final):
    x = x_ref[...] + (_dot(o_ref[...], wo1_ref[...]) + _dot(y_ref[...], wo2_ref[...]))
    x2 = _ffn(x, n2_ref[...], wg_ref, wu_ref, wd_ref)
    if final:
        x2 = _rms(x2, nf_ref[...])
    out_ref[...] = x2


def _resident(shape):
    nd = len(shape)
    return pl.BlockSpec(shape, lambda *_: (0,) * nd, pipeline_mode=pl.Buffered(1))


def _rows(tm, width):
    return pl.BlockSpec((tm, width), lambda i: (i, 0))


def _ffn_in(x, lw):
    r = x.shape[0]
    tm = min(ROW_TILE, r)
    widths = (D_MODEL, D_QKV, D_A, D_R, D_R, D_BA)
    return pl.pallas_call(
        _ffn_in_body,
        grid=(pl.cdiv(r, tm),),
        in_specs=[_rows(tm, D_MODEL), _resident((1, D_MODEL)),
                  _resident((D_MODEL, D_FF)), _resident((D_MODEL, D_FF)), _resident((D_FF, D_MODEL)),
                  _resident((1, D_MODEL)), _resident((D_MODEL, D_PROJ))],
        out_specs=[_rows(tm, w) for w in widths],
        out_shape=[jax.ShapeDtypeStruct((r, w), F32) for w in widths],
        compiler_params=pltpu.CompilerParams(dimension_semantics=("parallel",),
                                             vmem_limit_bytes=VMEM_LIMIT),
        name="ffn_in",
    )(x, lw["n1"], lw["f1g"], lw["f1u"], lw["f1d"], lw["nm"], lw["w_in"])


def _out_ffn(x, o, y, lw, nf, final):
    r = x.shape[0]
    tm = min(ROW_TILE, r)
    return pl.pallas_call(
        functools.partial(_out_ffn_body, final=final),
        grid=(pl.cdiv(r, tm),),
        in_specs=[_rows(tm, D_MODEL), _rows(tm, D_A), _rows(tm, D_R),
                  _resident((D_A, D_MODEL)), _resident((D_R, D_MODEL)), _resident((1, D_MODEL)),
                  _resident((D_MODEL, D_FF)), _resident((D_MODEL, D_FF)), _resident((D_FF, D_MODEL)),
                  _resident((1, D_MODEL))],
        out_specs=_rows(tm, D_MODEL),
        out_shape=jax.ShapeDtypeStruct((r, D_MODEL), F32),
        compiler_params=pltpu.CompilerParams(dimension_semantics=("parallel",),
                                             vmem_limit_bytes=VMEM_LIMIT),
        name="out_ffn",
    )(x, o, y, lw["wo1"], lw["wo2"], lw["n2"], lw["f2g"], lw["f2u"], lw["f2d"], nf)


class _Io:
    def __init__(self, **fns):
        self.after = lambda x: x
        self.__dict__.update(fns)


def _delta_stream(io, cw_ref, alog_ref, dtb_ref, on_ref, *, tiles, seg):
    c = seg
    nb = tiles[0][1]
    r = nb * c
    levels = int(math.log2(c))
    assert 2 ** levels == c
    row = lax.broadcasted_iota(jnp.int32, (r, r), 0)
    col = lax.broadcasted_iota(jnp.int32, (r, r), 1)
    same = lax.shift_right_logical(row, levels) == lax.shift_right_logical(col, levels)
    causal = same & (row >= col)
    strict = same & (row > col)
    tri = causal.astype(BF16)
    ones_seg = same.astype(BF16)
    eye = (row == col).astype(F32)
    sel_r = lax.broadcasted_iota(jnp.int32, (SUBLANES, D_BA), 0)
    sel_c = lax.broadcasted_iota(jnp.int32, (SUBLANES, D_BA), 1)
    sel = ((sel_c == sel_r + H_A) & (sel_r < H_A)).astype(BF16)

    def conv_silu(tile, col0):
        win = io.ext(tile, col0, DK)
        acc = None
        for j in range(CONV_W):
            k = CONV_W - 1 - j
            x = win if k == 0 else pltpu.roll(win, k, 1)
            term = x[:, TAIL:, :].reshape(r, DK) * io.after(cw_ref[j:j + 1, col0:col0 + DK])
            acc = term if acc is None else acc + term
        return _silu(acc)

    def exact_dot(lhs01, x, nt=False):
        out = None
        for p in _split3(x):
            if nt:
                d = lax.dot_general(lhs01, p, (((1,), (1,)), ((), ())), preferred_element_type=F32)
            else:
                d = jnp.dot(lhs01, p, preferred_element_type=F32)
            out = d if out is None else out + d
        return out

    items = [(j, h) for j in range(len(tiles)) for h in range(H_A)]

    gcum, gtot, grow, eg_all, beta_all = [], [], [], [], []
    for tile in tiles:
        ba = io.ba(tile)
        g_all = -jnp.exp(io.after(alog_ref[...])) * _softplus(ba + dtb_ref[...])
        beta_all.append(_sigmoid(ba))
        gcum.append(exact_dot(tri, g_all))
        gtot.append(exact_dot(ones_seg, g_all) if nb > 1 else gcum[-1][r - 1:r, :])
    yield
    for j in range(len(tiles)):
        grow.append(exact_dot(sel, gcum[j], nt=True))
        eg_all.append(jnp.exp(gcum[j]))
    yield

    qg, kn, kb, kdec, decay, rhs = {}, {}, {}, {}, {}, {}
    for (j, h) in items:
        tile = tiles[j]
        qh = conv_silu(tile, h * DK)
        kh = conv_silu(tile, D_A + h * DK)
        vh = conv_silu(tile, 2 * D_A + h * DV)
        qn = qh * lax.rsqrt(jnp.sum(qh * qh, axis=-1, keepdims=True) + EPS) * (DK ** -0.5)
        k_n = kh * lax.rsqrt(jnp.sum(kh * kh, axis=-1, keepdims=True) + EPS)
        beta = beta_all[j][:, h:h + 1]
        gcol = gcum[j][:, H_A + h:H_A + h + 1]
        egcol = eg_all[j][:, H_A + h:H_A + h + 1]
        glast = gtot[j][:, H_A + h:H_A + h + 1]
        dmat = gcol - grow[j][h:h + 1, :]
        decay[j, h] = jnp.where(causal, jnp.exp(jnp.where(causal, dmat, 0.0)), 0.0)
        kn[j, h] = k_n
        kb[j, h] = k_n * beta
        qg[j, h] = (qn, qn * egcol)
        kdec[j, h] = k_n * jnp.exp(glast - gcol)
        rhs[j, h] = jnp.concatenate([vh * beta, kb[j, h] * egcol], axis=-1)
        if h == H_A - 1:
            yield

    kq = {it: _dot_nt(jnp.concatenate([kb[it], qg[it][0]], axis=0), kn[it]) for it in items}
    lmat = {it: jnp.where(strict, kq[it][:r] * decay[it], 0.0) for it in items}
    attn = {it: kq[it][r:] * decay[it] for it in items}
    yield

    tinv = {it: eye - lmat[it] for it in items}
    m = {it: _dot(lmat[it], lmat[it]) for it in items}
    yield
    for _ in range(levels - 2):
        tinv = {it: tinv[it] + _dot(tinv[it], m[it]) for it in items}
        m = {it: _dot(m[it], m[it]) for it in items}
        yield
    tinv = {it: tinv[it] + _dot(tinv[it], m[it]) for it in items}
    yield
    sol = {it: _dot(tinv[it], rhs[it]) for it in items}
    yield

    state = {}

    def get_state(b, h):
        if (b, h) not in state:
            state[b, h] = io.get_state(b, h)
        return state[b, h]

    for j, tile in enumerate(tiles):
        b0 = tile[0]
        segs = [(s, slice(s * c, (s + 1) * c)) for s in range(nb)]
        ws = {}
        for h in range(H_A):
            for s, rs in segs:
                lhs = jnp.concatenate([sol[j, h][rs, DV:], qg[j, h][1][rs]], axis=0)
                ws[h, s] = _dot(lhs, get_state(b0 + s, h))
        yield
        v_new, av = {}, {}
        for h in range(H_A):
            w_s = jnp.concatenate([ws[h, s][:c] for s, _ in segs], axis=0) if nb > 1 else ws[h, 0][:c]
            v_new[h] = sol[j, h][:, :DV] - w_s
        for h in range(H_A):
            av[h] = _dot(attn[j, h], v_new[h])
        for h in range(H_A):
            for s, rs in segs:
                g0 = rs.start if nb > 1 else 0
                scale = jnp.exp(gtot[j][g0:g0 + 1, H_A + h:H_A + h + 1])
                state[b0 + s, h] = state[b0 + s, h] * scale + _dot_tn(kdec[j, h][rs], v_new[h][rs])
        for h in range(H_A):
            q_s = jnp.concatenate([ws[h, s][c:] for s, _ in segs], axis=0) if nb > 1 else ws[h, 0][c:]
            on = _rms(q_s + av[h], io.after(on_ref[...]))
            io.put_o(tile, h, on * _silu(io.z(tile, h * DV)))
        if j == len(tiles) - 1:
            for (b, h), val in state.items():
                io.put_state(b, h, val)
        yield


def _delta_body(qkv_ref, z_ref, ba_ref, cs_ref, s0_ref, cw_ref, alog_ref, dtb_ref, on_ref,
                o_ref, snew_ref, csnew_ref, ext_ref, *, tiles, seg, tt, n_t):
    c = seg
    nb = tiles[0][1]
    r = nb * c
    t = pl.program_id(1)

    @pl.when(t == 0)
    def _():
        ext_ref[:, TAIL - (CONV_W - 1):TAIL, :] = cs_ref[...]
        snew_ref[...] = s0_ref[...]

    ext_ref[:, TAIL:TAIL + tt, :] = qkv_ref[...]

    def rows(ref, tile, off, col0, width):
        b0, _, t0 = tile
        return ref[b0:b0 + nb, off + t0:off + t0 + c, col0:col0 + width].reshape(r, width)

    def put_o(tile, h, val):
        b0, _, t0 = tile
        o_ref[b0:b0 + nb, t0:t0 + c, h * DV:(h + 1) * DV] = val.reshape(nb, c, DV).astype(BF16)

    def put_state(b, h, val):
        snew_ref[b, h] = val

    def ext(tile, col0, w):
        b0, _, t0 = tile
        return ext_ref[b0:b0 + nb, t0:t0 + TAIL + c, col0:col0 + w]

    io = _Io(ext=ext, ba=lambda tile: rows(ba_ref, tile, 0, 0, D_BA),
             z=lambda tile, col0: rows(z_ref, tile, 0, col0, DV),
             put_o=put_o, get_state=lambda b, h: snew_ref[b, h], put_state=put_state)
    for _ in _delta_stream(io, cw_ref, alog_ref, dtb_ref, on_ref, tiles=tiles, seg=seg):
        pass

    tail = ext_ref[:, tt:tt + TAIL, :]
    ext_ref[:, 0:TAIL, :] = tail

    @pl.when(t == n_t - 1)
    def _():
        csnew_ref[...] = ext_ref[:, TAIL - (CONV_W - 1):TAIL, :]


def _delta(qkv, z, ba, cs, s0, lw, l, *, seg, tt, bb, tiles):
    b, t, _ = qkv.shape
    n_t = t // tt
    assert n_t * tt == t and b % bb == 0
    st = lambda i, j: (l, i, 0, 0, 0)
    ct = lambda i, j: (l, i, 0, 0)
    tok = lambda w: pl.BlockSpec((bb, tt, w), lambda i, j: (i, j, 0))
    const = lambda shape: pl.BlockSpec(shape, lambda i, j: (0,) * len(shape))
    return pl.pallas_call(
        functools.partial(_delta_body, tiles=tiles, seg=seg, tt=tt, n_t=n_t),
        grid=(b // bb, n_t),
        in_specs=[tok(D_QKV), tok(D_A), tok(D_BA),
                  pl.BlockSpec((None, bb, CONV_W - 1, D_QKV), ct),
                  pl.BlockSpec((None, bb, H_A, DK, DV), st),
                  const((CONV_W, D_QKV)), const((1, D_BA)), const((1, D_BA)), const((1, DV))],
        out_specs=[tok(D_A),
                   pl.BlockSpec((bb, H_A, DK, DV), lambda i, j: (i, 0, 0, 0)),
                   pl.BlockSpec((bb, CONV_W - 1, D_QKV), lambda i, j: (i, 0, 0))],
        out_shape=[jax.ShapeDtypeStruct((b, t, D_A), BF16),
                   jax.ShapeDtypeStruct((b, H_A, DK, DV), F32),
                   jax.ShapeDtypeStruct((b, CONV_W - 1, D_QKV), F32)],
        scratch_shapes=[pltpu.VMEM((bb, tt + TAIL, D_QKV), F32)],
        compiler_params=pltpu.CompilerParams(dimension_semantics=("arbitrary", "arbitrary"),
                                             vmem_limit_bytes=VMEM_LIMIT),
        name="delta",
    )(qkv, z, ba, cs, s0, lw["cqkv"], lw["alog"], lw["dtb"], lw["onorm"])


def _lru_stream(io, cw_ref, cb_ref, wg_ref, bg_ref, lam_ref, *, tt, bb):
    win = io.ext()
    xc = None
    for j in range(CONV_W):
        k = CONV_W - 1 - j
        x = win if k == 0 else pltpu.roll(win, k, 1)
        term = x[:, TAIL:, :] * io.after(cw_ref[j:j + 1, :])
        xc = term if xc is None else xc + term
    xc = xc + cb_ref[...]
    n = bb * tt
    xc = xc.reshape(n, D_R)
    yield
    gates = jnp.dot(xc.astype(BF16), wg_ref[...], preferred_element_type=F32) + bg_ref[...]
    r = _sigmoid(gates[:, :D_R])
    i = _sigmoid(gates[:, D_R:])
    log_a = -C_RG * r * _softplus(-io.after(lam_ref[...]))
    a = jnp.exp(log_a)
    m2 = -jnp.tanh(log_a) * (a * a + 1.0)
    u = jnp.where(m2 > 0.0, m2 * lax.rsqrt(m2), 0.0) * i * xc
    yield

    ng = n // SUBLANES
    a = a.reshape(ng, SUBLANES, D_R)
    u = u.reshape(ng, SUBLANES, D_R)
    tpos = lax.broadcasted_iota(jnp.int32, (ng, SUBLANES, D_R), 1)
    s = 1
    while s < SUBLANES:
        a_sh = pltpu.roll(a, s, 1)
        u_sh = pltpu.roll(u, s, 1)
        m = tpos >= s
        u = jnp.where(m, u + a * u_sh, u)
        a = jnp.where(m, a * a_sh, a)
        s *= 2
    yield
    gpb = tt // SUBLANES
    if gpb == 1:
        hs = u + a * io.h0()
        h_last = hs[:, SUBLANES - 1:SUBLANES, :]
    else:
        assert bb == 1
        h_in = io.h0()[0]
        groups = []
        for g in range(gpb):
            hg = u[g] + a[g] * h_in
            groups.append(hg)
            h_in = hg[SUBLANES - 1:SUBLANES, :]
        hs = jnp.concatenate(groups, axis=0)
        h_last = h_in.reshape(1, 1, D_R)
    y = hs.reshape(n, D_R) * jax.nn.gelu(io.gate(), approximate=True)
    io.put_y(y.reshape(bb, tt, D_R))
    io.put_h(h_last)
    yield


def _lru_body(xr_ref, gate_ref, cs_ref, h0_ref, cw_ref, cb_ref, wg_ref, bg_ref, lam_ref,
              y_ref, hnew_ref, csnew_ref, ext_ref, *, tt, bb, n_t):
    t = pl.program_id(1)

    @pl.when(t == 0)
    def _():
        ext_ref[:, TAIL - (CONV_W - 1):TAIL, :] = cs_ref[...]
        hnew_ref[...] = h0_ref[...]

    ext_ref[:, TAIL:TAIL + tt, :] = xr_ref[...]

    def put_y(val):
        y_ref[...] = val.astype(BF16)

    def put_h(val):
        hnew_ref[...] = val

    io = _Io(ext=lambda: ext_ref[...], gate=lambda: gate_ref[...].reshape(bb * tt, D_R),
             h0=lambda: hnew_ref[...], put_y=put_y, put_h=put_h)
    for _ in _lru_stream(io, cw_ref, cb_ref, wg_ref, bg_ref, lam_ref, tt=tt, bb=bb):
        pass

    tail = ext_ref[:, tt:tt + TAIL, :]
    ext_ref[:, 0:TAIL, :] = tail

    @pl.when(t == n_t - 1)
    def _():
        csnew_ref[...] = ext_ref[:, TAIL - (CONV_W - 1):TAIL, :]


def _lru(xr, gate, cs, h0, lw, l, *, tt, bb):
    b, t, _ = xr.shape
    n_t = t // tt
    assert n_t * tt == t and b % bb == 0 and tt % SUBLANES == 0
    ct = lambda i, j: (l, i, 0, 0)
    tok = pl.BlockSpec((bb, tt, D_R), lambda i, j: (i, j, 0))
    const = lambda shape: pl.BlockSpec(shape, lambda i, j: (0,) * len(shape))
    return pl.pallas_call(
        functools.partial(_lru_body, tt=tt, bb=bb, n_t=n_t),
        grid=(b // bb, n_t),
        in_specs=[tok, tok,
                  pl.BlockSpec((None, bb, CONV_W - 1, D_R), ct),
                  pl.BlockSpec((None, bb, 1, D_R), ct),
                  const((CONV_W, D_R)), const((1, D_R)), const((D_R, 2 * D_R)), const((1, 2 * D_R)),
                  const((1, D_R))],
        out_specs=[tok,
                   pl.BlockSpec((bb, 1, D_R), lambda i, j: (i, 0, 0)),
                   pl.BlockSpec((bb, CONV_W - 1, D_R), lambda i, j: (i, 0, 0))],
        out_shape=[jax.ShapeDtypeStruct((b, t, D_R), BF16),
                   jax.ShapeDtypeStruct((b, 1, D_R), F32),
                   jax.ShapeDtypeStruct((b, CONV_W - 1, D_R), F32)],
        scratch_shapes=[pltpu.VMEM((bb, tt + TAIL, D_R), F32)],
        compiler_params=pltpu.CompilerParams(dimension_semantics=("arbitrary", "arbitrary"),
                                             vmem_limit_bytes=VMEM_LIMIT),
        name="lru",
    )(xr, gate, cs, h0, lw["clw"], lw["clb"], lw["wgate"], lw["bgate"], lw["lam"])


def _interleave(streams, plan, lead, set_dep):
    live = dict(streams)

    def step(key):
        if key not in live:
            return
        try:
            val = next(live[key])
            if key == lead:
                set_dep(val)
        except StopIteration:
            del live[key]

    for key in plan:
        step(key)
    while live:
        for key in list(live):
            step(key)


def _in_mix_body(flag_ref, x_ref, n1_ref, wg_ref, wu_ref, wd_ref, nm_ref, win_ref,
                 dcs_ref, ds0_ref, cw_ref, alog_ref, dtb_ref, on_ref,
                 lcs_ref, lh0_ref, lcw_ref, lcb_ref, lwg_ref, lbg_ref, lam_ref,
                 x1_ref, o_ref, y_ref, snew_ref, dcsnew_ref, hnew_ref, lcsnew_ref,
                 qkv_scr, z_scr, ba_scr, xr_scr, gate_scr, dtail_scr, ltail_scr, s_scr, h_scr,
                 *, tiles_per_seq):
    tt = MIX_TILE
    s = pl.program_id(0)
    slot_a = lax.rem(s, 2)
    slot_b = 1 - slot_a
    first = lax.rem(s + tiles_per_seq - 1, tiles_per_seq) == 0
    tails = slice(TAIL - (CONV_W - 1), TAIL)
    keep = flag_ref[0] == 1
    dep = [None]

    def after(x):
        if dep[0] is None:
            return x
        reps = x.shape[-1] // DK
        d = dep[0] if reps == 1 else jnp.concatenate([dep[0]] * reps, axis=-1)
        return jnp.where(keep, x, d)

    def set_dep(val):
        dep[0] = val

    @pl.when(s == 0)
    def _():
        for ref in (qkv_scr, z_scr, ba_scr, xr_scr, gate_scr, dtail_scr, ltail_scr, s_scr, h_scr):
            ref[...] = jnp.zeros(ref.shape, F32)

    def stream_a():
        x = x_ref[...]
        h = _rms(x, n1_ref[...]).astype(BF16)

        def down(gu, acc):
            g, u, c0 = gu
            a = (_silu(g) * u).astype(BF16)
            d = jnp.dot(a, wd_ref[c0:c0 + FF_PIECE, :], preferred_element_type=F32)
            return d if acc is None else acc + d

        acc, pending = None, None
        for c0 in range(0, D_FF, FF_PIECE):
            g = jnp.dot(h, wg_ref[:, c0:c0 + FF_PIECE], preferred_element_type=F32)
            u = jnp.dot(h, wu_ref[:, c0:c0 + FF_PIECE], preferred_element_type=F32)
            if pending is not None:
                acc = down(pending, acc)
            pending = (g, u, c0)
            yield g[0:1, 0:DK]
        acc = down(pending, acc)
        x1 = x + 0.5 * acc
        x1_ref[...] = x1
        hm = _rms(x1, nm_ref[...]).astype(BF16)
        yield acc[0:1, 0:DK]
        dsts = ((qkv_scr, TAIL, 0, D_QKV), (z_scr, 0, D_QKV, D_A), (xr_scr, TAIL, D_QKV + D_A, D_R),
                (gate_scr, 0, D_QKV + D_A + D_R, D_R), (ba_scr, 0, D_QKV + D_A + 2 * D_R, D_BA))
        for dst, row0, col0, width in dsts:
            for d0 in range(0, width, PROJ_PIECE):
                w = min(PROJ_PIECE, width - d0)
                p = jnp.dot(hm, win_ref[:, col0 + d0:col0 + d0 + w], preferred_element_type=F32)
                dst[slot_a, row0:row0 + tt, d0:d0 + w] = p
                yield p[0:1, 0:DK]

    qkv_scr[slot_b, tails, :] = jnp.where(first, dcs_ref[0], dtail_scr[tails, :])
    xr_scr[slot_b, tails, :] = jnp.where(first, lcs_ref[0], ltail_scr[tails, :])

    def put_o(tile, h, val):
        o_ref[tile[2]:tile[2] + CHUNK, h * DV:(h + 1) * DV] = val.astype(BF16)

    def put_state(b, h, val):
        s_scr[h] = val
        snew_ref[0, h] = val

    def put_y(val):
        y_ref[...] = val.reshape(tt, D_R).astype(BF16)

    def put_h(val):
        h_scr[...] = val
        hnew_ref[...] = val

    dio = _Io(ext=lambda tile, col0, w: qkv_scr[slot_b, tile[2]:tile[2] + TAIL + CHUNK, col0:col0 + w][None],
              ba=lambda tile: ba_scr[slot_b, tile[2]:tile[2] + CHUNK, :],
              z=lambda tile, col0: z_scr[slot_b, tile[2]:tile[2] + CHUNK, col0:col0 + DV],
              put_o=put_o, get_state=lambda b, h: jnp.where(first, ds0_ref[0, h], s_scr[h]),
              put_state=put_state, after=after)
    lio = _Io(ext=lambda: xr_scr[slot_b][None],
              gate=lambda: gate_scr[slot_b], h0=lambda: jnp.where(first, lh0_ref[...], h_scr[...]),
              put_y=put_y, put_h=put_h, after=after)
    tiles = tuple((0, 1, j * CHUNK) for j in range(tt // CHUNK))
    _interleave({"a": stream_a(),
                 "d": _delta_stream(dio, cw_ref, alog_ref, dtb_ref, on_ref, tiles=tiles, seg=CHUNK),
                 "l": _lru_stream(lio, lcw_ref, lcb_ref, lwg_ref, lbg_ref, lam_ref, tt=tt, bb=1)},
                MIX_PLAN, "a", set_dep)

    dtail_scr[...] = qkv_scr[slot_b, tt:tt + TAIL, :]
    ltail_scr[...] = xr_scr[slot_b, tt:tt + TAIL, :]
    dcsnew_ref[0] = qkv_scr[slot_b, tt + TAIL - (CONV_W - 1):tt + TAIL, :]
    lcsnew_ref[0] = xr_scr[slot_b, tt + TAIL - (CONV_W - 1):tt + TAIL, :]


def _in_mix(x, sdc, sd, slc, sl, lw, l, *, batch):
    r = x.shape[0]
    tt = MIX_TILE
    n_tiles = r // tt
    tiles_per_seq = n_tiles // batch
    assert n_tiles * tt == r and tiles_per_seq * batch == n_tiles
    cur = lambda s: (jnp.minimum(s, n_tiles - 1), 0)
    prev = lambda s: (jnp.maximum(s - 1, 0), 0)
    seq3 = lambda s: (jnp.maximum(s - 1, 0) // tiles_per_seq, 0, 0)
    seq4 = lambda s: (jnp.maximum(s - 1, 0) // tiles_per_seq, 0, 0, 0)
    layer = lambda shape: pl.BlockSpec((None,) + shape, lambda s: (l,) + (0,) * len(shape),
                                       pipeline_mode=pl.Buffered(1))
    return pl.pallas_call(
        functools.partial(_in_mix_body, tiles_per_seq=tiles_per_seq),
        grid=(n_tiles + 1,),
        in_specs=[pl.BlockSpec(memory_space=pltpu.SMEM),
                  pl.BlockSpec((tt, D_MODEL), cur), _resident((1, D_MODEL)),
                  _resident((D_MODEL, D_FF)), _resident((D_MODEL, D_FF)), _resident((D_FF, D_MODEL)),
                  _resident((1, D_MODEL)), _resident((D_MODEL, D_PROJ)),
                  layer((1, CONV_W - 1, D_QKV)), layer((1, H_A, DK, DV)),
                  _resident((CONV_W, D_QKV)), _resident((1, D_BA)), _resident((1, D_BA)), _resident((1, DV)),
                  layer((1, CONV_W - 1, D_R)), layer((1, 1, D_R)),
                  _resident((CONV_W, D_R)), _resident((1, D_R)), _resident((D_R, 2 * D_R)),
                  _resident((1, 2 * D_R)), _resident((1, D_R))],
        out_specs=[pl.BlockSpec((tt, D_MODEL), cur), pl.BlockSpec((tt, D_A), prev), pl.BlockSpec((tt, D_R), prev),
                   pl.BlockSpec((1, H_A, DK, DV), seq4), pl.BlockSpec((1, CONV_W - 1, D_QKV), seq3),
                   pl.BlockSpec((1, 1, D_R), seq3), pl.BlockSpec((1, CONV_W - 1, D_R), seq3)],
        out_shape=[jax.ShapeDtypeStruct((r, D_MODEL), F32), jax.ShapeDtypeStruct((r, D_A), BF16),
                   jax.ShapeDtypeStruct((r, D_R), BF16),
                   jax.ShapeDtypeStruct((batch, H_A, DK, DV), F32),
                   jax.ShapeDtypeStruct((batch, CONV_W - 1, D_QKV), F32),
                   jax.ShapeDtypeStruct((batch, 1, D_R), F32),
                   jax.ShapeDtypeStruct((batch, CONV_W - 1, D_R), F32)],
        scratch_shapes=[pltpu.VMEM((2, TAIL + tt, D_QKV), F32), pltpu.VMEM((2, tt, D_A), F32),
                        pltpu.VMEM((2, tt, D_BA), F32), pltpu.VMEM((2, TAIL + tt, D_R), F32),
                        pltpu.VMEM((2, tt, D_R), F32), pltpu.VMEM((TAIL, D_QKV), F32),
                        pltpu.VMEM((TAIL, D_R), F32), pltpu.VMEM((H_A, DK, DV), F32),
                        pltpu.VMEM((1, 1, D_R), F32)],
        compiler_params=pltpu.CompilerParams(dimension_semantics=("arbitrary",),
                                             vmem_limit_bytes=VMEM_LIMIT),
        name="in_mix",
    )(jnp.ones((1,), jnp.int32), x, lw["n1"], lw["f1g"], lw["f1u"], lw["f1d"], lw["nm"], lw["w_in"],
      sdc, sd, lw["cqkv"], lw["alog"], lw["dtb"], lw["onorm"],
      slc, sl, lw["clw"], lw["clb"], lw["wgate"], lw["bgate"], lw["lam"])


def _prep_layer(l, w):
    row = lambda v: v.reshape(1, -1).astype(F32)
    c1 = D_QKV
    c2 = c1 + D_A
    c3 = c2 + H_A
    c4 = c3 + H_A
    c5 = c4 + D_R
    w_in = w["w_in"][l].astype(BF16)
    pad = jnp.zeros((D_MODEL, D_BA - 2 * H_A), BF16)
    w_in_r = jnp.concatenate([w_in[:, :c2], w_in[:, c4:c5], w_in[:, c5:], w_in[:, c2:c4], pad], axis=1)
    lane_pad = lambda v: jnp.zeros((1, D_BA), F32).at[0, H_A:2 * H_A].set(v)
    eye = jnp.eye(NB_R, dtype=F32)
    bd = lambda wb: jnp.einsum("ncd,nm->ncmd", wb, eye).reshape(D_R, D_R)
    return dict(
        n1=row(w["norm_ffn1"][l]), f1g=w["w_ffn1_gate"][l].astype(BF16),
        f1u=w["w_ffn1_up"][l].astype(BF16), f1d=w["w_ffn1_down"][l].astype(BF16),
        nm=row(w["norm_mix"][l]), w_in=w_in_r,
        cqkv=w["conv_qkv"][l], alog=lane_pad(w["a_log"][l]), dtb=lane_pad(w["dt_bias"][l]),
        onorm=row(w["norm_delta_out"][l]),
        clw=w["conv_lru_w"][l], clb=row(w["conv_lru_b"][l]),
        wgate=jnp.concatenate([bd(w["w_rgate"][l]), bd(w["w_igate"][l])], axis=1).astype(BF16),
        bgate=jnp.concatenate([row(w["b_rgate"][l]), row(w["b_igate"][l])], axis=1),
        lam=row(w["lru_lambda"][l]),
        wo1=w["w_out"][l][:D_A].astype(BF16), wo2=w["w_out"][l][D_A:].astype(BF16),
        n2=row(w["norm_ffn2"][l]), f2g=w["w_ffn2_gate"][l].astype(BF16),
        f2u=w["w_ffn2_up"][l].astype(BF16), f2d=w["w_ffn2_down"][l].astype(BF16),
    )


def _run_group(x, sd, sdc, sl, slc, layers, nf, *, seg=None, delta_tt=None, delta_bb=None, tiles=None,
               lru_tt=None, lru_bb=None, fused=False, states_only=False):
    b, t, _ = x.shape
    xf = x.reshape(b * t, D_MODEL)
    nd, ndc, nl, nlc = [], [], [], []
    n_layers = len(layers)
    for l, lw in enumerate(layers):
        if fused:
            x1, o, y, s_new, cs_new, h_new, lcs_new = _in_mix(xf, sdc, sd, slc, sl, lw, l, batch=b)
        else:
            x1, qkv, z, xr, gate, ba = _ffn_in(xf, lw)
            r3 = lambda v: v.reshape(b, t, v.shape[-1])
            o, s_new, cs_new = _delta(r3(qkv), r3(z), r3(ba), sdc, sd, lw, l,
                                      seg=seg, tt=delta_tt, bb=delta_bb, tiles=tiles)
            y, h_new, lcs_new = _lru(r3(xr), r3(gate), slc, sl, lw, l, tt=lru_tt, bb=lru_bb)
        last = l == n_layers - 1
        if not (last and states_only):
            xf = _out_ffn(x1, o.reshape(b * t, D_A), y.reshape(b * t, D_R), lw, nf, last)
        nd.append(s_new)
        ndc.append(cs_new)
        nl.append(h_new)
        nlc.append(lcs_new)
    stack = jnp.stack
    y_out = None if states_only else xf.reshape(b, t, D_MODEL)
    return y_out, stack(nd), stack(ndc), stack(nl), stack(nlc)


def kernel(x_prompt, x_sample, state_delta, state_delta_conv, state_lru, state_lru_conv, meta_tokens, norm_ffn1, w_ffn1_gate, w_ffn1_up, w_ffn1_down, norm_mix, w_in, conv_qkv, a_log, dt_bias, norm_delta_out, conv_lru_w, conv_lru_b, w_rgate, b_rgate, w_igate, b_igate, lru_lambda, w_out, norm_ffn2, w_ffn2_gate, w_ffn2_up, w_ffn2_down, norm_final):
    w = dict(norm_ffn1=norm_ffn1, w_ffn1_gate=w_ffn1_gate, w_ffn1_up=w_ffn1_up, w_ffn1_down=w_ffn1_down,
             norm_mix=norm_mix, w_in=w_in, conv_qkv=conv_qkv, a_log=a_log, dt_bias=dt_bias,
             norm_delta_out=norm_delta_out, conv_lru_w=conv_lru_w, conv_lru_b=conv_lru_b,
             w_rgate=w_rgate, b_rgate=b_rgate, w_igate=w_igate, b_igate=b_igate, lru_lambda=lru_lambda,
             w_out=w_out, norm_ffn2=norm_ffn2, w_ffn2_gate=w_ffn2_gate, w_ffn2_up=w_ffn2_up,
             w_ffn2_down=w_ffn2_down)
    depth = norm_ffn1.shape[0]
    layers = [_prep_layer(l, w) for l in range(depth)]
    nf = norm_final.reshape(1, D_MODEL).astype(F32)
    bp, seq, _ = x_prompt.shape
    bs, dseq, _ = x_sample.shape
    seg_b = CHUNK // dseq

    zeros = lambda *s: jnp.zeros((depth,) + s, F32)
    _, md, mdc, ml, mlc = _run_group(
        meta_tokens.astype(F32)[None], zeros(1, H_A, DK, DV), zeros(1, CONV_W - 1, D_QKV),
        zeros(1, 1, D_R), zeros(1, CONV_W - 1, D_R), layers, nf,
        seg=N_META, delta_tt=N_META, delta_bb=1, tiles=((0, 1, 0),), lru_tt=N_META, lru_bb=1,
        states_only=True)
    y_prompt, pd, pdc, plr, plc = _run_group(x_prompt, md, mdc, ml, mlc, layers, nf, fused=True)
    y_sample, sdn, sdcn, sln, slcn = _run_group(
        x_sample, state_delta, state_delta_conv, state_lru.reshape(depth, bs, 1, D_R), state_lru_conv,
        layers, nf, seg=dseq, delta_tt=dseq, delta_bb=SAMPLE_TILES_PER_STEP * seg_b,
        tiles=tuple((j * seg_b, seg_b, 0) for j in range(SAMPLE_TILES_PER_STEP)),
        lru_tt=dseq, lru_bb=2 * seg_b)

    return (y_prompt, y_sample, pd, pdc, plr.reshape(depth, bp, D_R), plc,
            sdn, sdcn, sln.reshape(depth, bs, D_R), slcn)
```

```python
import functools
import math

import jax
import jax.numpy as jnp
from jax import lax
from jax.experimental import pallas as pl
from jax.experimental.pallas import tpu as pltpu

F32 = jnp.float32
BF16 = jnp.bfloat16

D_MODEL = 1024
D_FF = 2816
D_A = 512
D_R = 512
H_A = 4
DK = 128
DV = 128
CHUNK = 64
CONV_W = 4
NB_R = 8
C_RG = 8.0
EPS = 1e-6
N_META = 16
D_QKV = 3 * D_A
D_BA = 128
D_PROJ = D_QKV + D_A + 2 * D_R + D_BA

SUBLANES = 8
TAIL = SUBLANES
VMEM_BYTES_V7X = 64 * 1024 * 1024
VMEM_LIMIT = VMEM_BYTES_V7X - 8 * 1024 * 1024
ROW_TILE = 256
MIX_TILE = 256
FF_PIECE = 256
PROJ_PIECE = 256
MIX_PLAN = "ad" * 7 + ("adl" + "ad") * 3 + "adl" + "ad" * 8
SAMPLE_TILES_PER_STEP = 2


def _rms(x, w):
    ms = jnp.mean(x * x, axis=-1, keepdims=True)
    return x * lax.rsqrt(ms + EPS) * w


def _sigmoid(x):
    return 1.0 / (1.0 + jnp.exp(-x))


def _silu(x):
    return x * _sigmoid(x)


def _softplus(x):
    t = jnp.exp(-jnp.abs(x))
    u = 1.0 + t
    d = u - 1.0
    log1p_t = jnp.where(d == 0.0, t, jnp.log(u) * (t / jnp.where(d == 0.0, 1.0, d)))
    return jnp.maximum(x, 0.0) + log1p_t


def _dot(a, b):
    return jnp.dot(a.astype(BF16), b.astype(BF16), preferred_element_type=F32)


def _dot_nt(a, b):
    return lax.dot_general(a.astype(BF16), b.astype(BF16), (((1,), (1,)), ((), ())),
                           preferred_element_type=F32)


def _dot_tn(a, b):
    return lax.dot_general(a.astype(BF16), b.astype(BF16), (((0,), (0,)), ((), ())),
                           preferred_element_type=F32)


def _split3(a):
    a1 = a.astype(BF16)
    r = a - a1.astype(F32)
    a2 = r.astype(BF16)
    a3 = (r - a2.astype(F32)).astype(BF16)
    return a1, a2, a3


def _ffn(x, nw, wg_ref, wu_ref, wd_ref):
    h = _rms(x, nw).astype(BF16)
    g = jnp.dot(h, wg_ref[...], preferred_element_type=F32)
    u = jnp.dot(h, wu_ref[...], preferred_element_type=F32)
    a = (_silu(g) * u).astype(BF16)
    return x + 0.5 * jnp.dot(a, wd_ref[...], preferred_element_type=F32)


def _ffn_in_body(x_ref, n1_ref, wg_ref, wu_ref, wd_ref, nm_ref, win_ref,
                 x1_ref, qkv_ref, z_ref, xr_ref, gate_ref, ba_ref):
    x1 = _ffn(x_ref[...], n1_ref[...], wg_ref, wu_ref, wd_ref)
    x1_ref[...] = x1
    hm = _rms(x1, nm_ref[...]).astype(BF16)
    p = jnp.dot(hm, win_ref[...], preferred_element_type=F32)
    c1 = D_QKV
    c2 = c1 + D_A
    c3 = c2 + D_R
    c4 = c3 + D_R
    qkv_ref[...] = p[:, :c1]
    z_ref[...] = p[:, c1:c2]
    xr_ref[...] = p[:, c2:c3]
    gate_ref[...] = p[:, c3:c4]
    ba_ref[...] = p[:, c4:]


def _out_ffn_body(x_ref, o_ref, y_ref, wo1_ref, wo2_ref, n2_ref, wg_ref, wu_ref, wd_ref, nf_ref,
                  out_ref, *, final):
    x = x_ref[...] + (_dot(o_ref[...], wo1_ref[...]) + _dot(y_ref[...], wo2_ref[...]))
    x2 = _ffn(x, n2_ref[...], wg_ref, wu_ref, wd_ref)
    if final:
        x2 = _rms(x2, nf_ref[...])
    out_ref[...] = x2


def _resident(shape):
    nd = len(shape)
    return pl.BlockSpec(shape, lambda *_: (0,) * nd, pipeline_mode=pl.Buffered(1))


def _of_layer(shape, l, first=0):
    nd = len(shape)
    return pl.BlockSpec((None,) + shape, lambda *_: (l, first) + (0,) * (nd - 1), pipeline_mode=pl.Buffered(1))


def _rows(tm, width):
    return pl.BlockSpec((tm, width), lambda i: (i, 0))


def _ffn_in(x, lw):
    r = x.shape[0]
    l = lw["l"]
    tm = min(ROW_TILE, r)
    widths = (D_MODEL, D_QKV, D_A, D_R, D_R, D_BA)
    return pl.pallas_call(
        _ffn_in_body,
        grid=(pl.cdiv(r, tm),),
        in_specs=[_rows(tm, D_MODEL), _resident((1, D_MODEL)),
                  _of_layer((D_MODEL, D_FF), l), _of_layer((D_MODEL, D_FF), l), _of_layer((D_FF, D_MODEL), l),
                  _resident((1, D_MODEL)), _resident((D_MODEL, D_PROJ))],
        out_specs=[_rows(tm, w) for w in widths],
        out_shape=[jax.ShapeDtypeStruct((r, w), F32) for w in widths],
        compiler_params=pltpu.CompilerParams(dimension_semantics=("parallel",),
                                             vmem_limit_bytes=VMEM_LIMIT),
        name="ffn_in",
    )(x, lw["n1"], lw["f1g"], lw["f1u"], lw["f1d"], lw["nm"], lw["w_in"])


def _out_ffn(x, o, y, lw, nf, final):
    r = x.shape[0]
    l = lw["l"]
    tm = min(ROW_TILE, r)
    return pl.pallas_call(
        functools.partial(_out_ffn_body, final=final),
        grid=(pl.cdiv(r, tm),),
        in_specs=[_rows(tm, D_MODEL), _rows(tm, D_A), _rows(tm, D_R),
                  _of_layer((D_A, D_MODEL), l, 0), _of_layer((D_R, D_MODEL), l, 1), _resident((1, D_MODEL)),
                  _of_layer((D_MODEL, D_FF), l), _of_layer((D_MODEL, D_FF), l), _of_layer((D_FF, D_MODEL), l),
                  _resident((1, D_MODEL))],
        out_specs=_rows(tm, D_MODEL),
        out_shape=jax.ShapeDtypeStruct((r, D_MODEL), F32),
        compiler_params=pltpu.CompilerParams(dimension_semantics=("parallel",),
                                             vmem_limit_bytes=VMEM_LIMIT),
        name="out_ffn",
    )(x, o, y, lw["w_out"], lw["w_out"], lw["n2"], lw["f2g"], lw["f2u"], lw["f2d"], nf)


class _Io:
    def __init__(self, **fns):
        self.after = lambda x: x
        self.__dict__.update(fns)


def _delta_stream(io, cw_ref, alog_ref, dtb_ref, on_ref, *, tiles, seg):
    c = seg
    nb = tiles[0][1]
    r = nb * c
    levels = int(math.log2(c))
    assert 2 ** levels == c
    row = lax.broadcasted_iota(jnp.int32, (r, r), 0)
    col = lax.broadcasted_iota(jnp.int32, (r, r), 1)
    same = lax.shift_right_logical(row, levels) == lax.shift_right_logical(col, levels)
    causal = same & (row >= col)
    strict = same & (row > col)
    tri = causal.astype(BF16)
    ones_seg = same.astype(BF16)
    eye = (row == col).astype(F32)
    sel_r = lax.broadcasted_iota(jnp.int32, (SUBLANES, D_BA), 0)
    sel_c = lax.broadcasted_iota(jnp.int32, (SUBLANES, D_BA), 1)
    sel = ((sel_c == sel_r + H_A) & (sel_r < H_A)).astype(BF16)

    def conv_silu(tile, col0):
        win = io.ext(tile, col0, DK)
        acc = None
        for j in range(CONV_W):
            k = CONV_W - 1 - j
            x = win if k == 0 else pltpu.roll(win, k, 1)
            term = x[:, TAIL:, :].reshape(r, DK) * io.after(cw_ref[j:j + 1, col0:col0 + DK])
            acc = term if acc is None else acc + term
        return _silu(acc)

    def exact_dot(lhs01, x, nt=False):
        out = None
        for p in _split3(x):
            if nt:
                d = lax.dot_general(lhs01, p, (((1,), (1,)), ((), ())), preferred_element_type=F32)
            else:
                d = jnp.dot(lhs01, p, preferred_element_type=F32)
            out = d if out is None else out + d
        return out

    items = [(j, h) for j in range(len(tiles)) for h in range(H_A)]

    gcum, gtot, grow, eg_all, beta_all = [], [], [], [], []
    for tile in tiles:
        ba = io.ba(tile)
        g_all = -jnp.exp(io.after(alog_ref[...])) * _softplus(ba + dtb_ref[...])
        beta_all.append(_sigmoid(ba))
        gcum.append(exact_dot(tri, g_all))
        gtot.append(exact_dot(ones_seg, g_all) if nb > 1 else gcum[-1][r - 1:r, :])
    yield
    for j in range(len(tiles)):
        grow.append(exact_dot(sel, gcum[j], nt=True))
        eg_all.append(jnp.exp(gcum[j]))
    yield

    qg, kn, kb, kdec, decay, rhs = {}, {}, {}, {}, {}, {}
    for (j, h) in items:
        tile = tiles[j]
        qh = conv_silu(tile, h * DK)
        kh = conv_silu(tile, D_A + h * DK)
        vh = conv_silu(tile, 2 * D_A + h * DV)
        qn = qh * lax.rsqrt(jnp.sum(qh * qh, axis=-1, keepdims=True) + EPS) * (DK ** -0.5)
        k_n = kh * lax.rsqrt(jnp.sum(kh * kh, axis=-1, keepdims=True) + EPS)
        beta = beta_all[j][:, h:h + 1]
        gcol = gcum[j][:, H_A + h:H_A + h + 1]
        egcol = eg_all[j][:, H_A + h:H_A + h + 1]
        glast = gtot[j][:, H_A + h:H_A + h + 1]
        dmat = gcol - grow[j][h:h + 1, :]
        decay[j, h] = jnp.where(causal, jnp.exp(jnp.where(causal, dmat, 0.0)), 0.0)
        kn[j, h] = k_n
        kb[j, h] = k_n * beta
        qg[j, h] = (qn, qn * egcol)
        kdec[j, h] = k_n * jnp.exp(glast - gcol)
        rhs[j, h] = jnp.concatenate([vh * beta, kb[j, h] * egcol], axis=-1)
        if h == H_A - 1:
            yield

    kq = {it: _dot_nt(jnp.concatenate([kb[it], qg[it][0]], axis=0), kn[it]) for it in items}
    lmat = {it: jnp.where(strict, kq[it][:r] * decay[it], 0.0) for it in items}
    attn = {it: kq[it][r:] * decay[it] for it in items}
    yield

    tinv = {it: eye - lmat[it] for it in items}
    m = {it: _dot(lmat[it], lmat[it]) for it in items}
    yield
    for _ in range(levels - 2):
        tinv = {it: tinv[it] + _dot(tinv[it], m[it]) for it in items}
        m = {it: _dot(m[it], m[it]) for it in items}
        yield
    tinv = {it: tinv[it] + _dot(tinv[it], m[it]) for it in items}
    yield
    sol = {it: _dot(tinv[it], rhs[it]) for it in items}
    yield

    state = {}

    def get_state(b, h):
        if (b, h) not in state:
            state[b, h] = io.get_state(b, h)
        return state[b, h]

    for j, tile in enumerate(tiles):
        b0 = tile[0]
        segs = [(s, slice(s * c, (s + 1) * c)) for s in range(nb)]
        ws = {}
        for h in range(H_A):
            for s, rs in segs:
                lhs = jnp.concatenate([sol[j, h][rs, DV:], qg[j, h][1][rs]], axis=0)
                ws[h, s] = _dot(lhs, get_state(b0 + s, h))
        yield
        v_new, av = {}, {}
        for h in range(H_A):
            w_s = jnp.concatenate([ws[h, s][:c] for s, _ in segs], axis=0) if nb > 1 else ws[h, 0][:c]
            v_new[h] = sol[j, h][:, :DV] - w_s
        for h in range(H_A):
            av[h] = _dot(attn[j, h], v_new[h])
        for h in range(H_A):
            for s, rs in segs:
                g0 = rs.start if nb > 1 else 0
                scale = jnp.exp(gtot[j][g0:g0 + 1, H_A + h:H_A + h + 1])
                state[b0 + s, h] = state[b0 + s, h] * scale + _dot_tn(kdec[j, h][rs], v_new[h][rs])
        for h in range(H_A):
            q_s = jnp.concatenate([ws[h, s][c:] for s, _ in segs], axis=0) if nb > 1 else ws[h, 0][c:]
            on = _rms(q_s + av[h], io.after(on_ref[...]))
            io.put_o(tile, h, on * _silu(io.z(tile, h * DV)))
        if j == len(tiles) - 1:
            for (b, h), val in state.items():
                io.put_state(b, h, val)
        yield


def _delta_body(qkv_ref, z_ref, ba_ref, cs_ref, s0_ref, cw_ref, alog_ref, dtb_ref, on_ref, *rest,
                tiles, seg, tt, n_t, fill_layers):
    if fill_layers:
        o_ref, sall_ref, csnew_ref, ext_ref = rest
        snew_ref = sall_ref.at[0]
    else:
        _, o_ref, snew_ref, csnew_ref, ext_ref = rest
    c = seg
    nb = tiles[0][1]
    r = nb * c
    t = pl.program_id(1)

    @pl.when(t == 0)
    def _():
        ext_ref[:, TAIL - (CONV_W - 1):TAIL, :] = cs_ref[...]
        snew_ref[...] = s0_ref[...]

    ext_ref[:, TAIL:TAIL + tt, :] = qkv_ref[...]

    def rows(ref, tile, off, col0, width):
        b0, _, t0 = tile
        return ref[b0:b0 + nb, off + t0:off + t0 + c, col0:col0 + width].reshape(r, width)

    def put_o(tile, h, val):
        b0, _, t0 = tile
        o_ref[b0:b0 + nb, t0:t0 + c, h * DV:(h + 1) * DV] = val.reshape(nb, c, DV).astype(BF16)

    def put_state(b, h, val):
        snew_ref[b, h] = val

    def ext(tile, col0, w):
        b0, _, t0 = tile
        return ext_ref[b0:b0 + nb, t0:t0 + TAIL + c, col0:col0 + w]

    io = _Io(ext=ext, ba=lambda tile: rows(ba_ref, tile, 0, 0, D_BA),
             z=lambda tile, col0: rows(z_ref, tile, 0, col0, DV),
             put_o=put_o, get_state=lambda b, h: snew_ref[b, h], put_state=put_state)
    for _ in _delta_stream(io, cw_ref, alog_ref, dtb_ref, on_ref, tiles=tiles, seg=seg):
        pass

    tail = ext_ref[:, tt:tt + TAIL, :]
    ext_ref[:, 0:TAIL, :] = tail

    @pl.when(t == n_t - 1)
    def _():
        csnew_ref[...] = ext_ref[:, TAIL - (CONV_W - 1):TAIL, :]
        if fill_layers:
            for d in range(1, fill_layers):
                sall_ref[d] = sall_ref[0]


def _delta(qkv, z, ba, cs, s0, lw, l, s_all, *, seg, tt, bb, tiles):
    b, t, _ = qkv.shape
    depth = s0.shape[0]
    n_t = t // tt
    assert n_t * tt == t and b % bb == 0 and (l == 0) == (s_all is None)
    st = lambda i, j: (l, i, 0, 0, 0)
    ct = lambda i, j: (l, i, 0, 0)
    tok = lambda w: pl.BlockSpec((bb, tt, w), lambda i, j: (i, j, 0))
    const = lambda shape: pl.BlockSpec(shape, lambda i, j: (0,) * len(shape))
    in_specs = [tok(D_QKV), tok(D_A), tok(D_BA),
                pl.BlockSpec((None, bb, CONV_W - 1, D_QKV), ct),
                pl.BlockSpec((None, bb, H_A, DK, DV), st),
                const((CONV_W, D_QKV)), const((1, D_BA)), const((1, D_BA)), const((1, DV))]
    operands = [qkv, z, ba, cs, s0, lw["cqkv"], lw["alog"], lw["dtb"], lw["onorm"]]
    if s_all is None:
        s_spec = pl.BlockSpec((depth, bb, H_A, DK, DV), lambda i, j: (0, i, 0, 0, 0))
        aliases = {}
    else:
        s_spec = pl.BlockSpec((None, bb, H_A, DK, DV), st)
        in_specs.append(pl.BlockSpec(memory_space=pl.ANY))
        operands.append(s_all)
        aliases = {len(operands) - 1: 1}
    return pl.pallas_call(
        functools.partial(_delta_body, tiles=tiles, seg=seg, tt=tt, n_t=n_t,
                          fill_layers=depth if s_all is None else 0),
        grid=(b // bb, n_t),
        in_specs=in_specs,
        out_specs=[tok(D_A), s_spec,
                   pl.BlockSpec((bb, CONV_W - 1, D_QKV), lambda i, j: (i, 0, 0))],
        out_shape=[jax.ShapeDtypeStruct((b, t, D_A), BF16),
                   jax.ShapeDtypeStruct((depth, b, H_A, DK, DV), F32),
                   jax.ShapeDtypeStruct((b, CONV_W - 1, D_QKV), F32)],
        scratch_shapes=[pltpu.VMEM((bb, tt + TAIL, D_QKV), F32)],
        input_output_aliases=aliases,
        compiler_params=pltpu.CompilerParams(dimension_semantics=("arbitrary", "arbitrary"),
                                             vmem_limit_bytes=VMEM_LIMIT),
        name="delta",
    )(*operands)


def _lru_stream(io, cw_ref, cb_ref, wg_ref, bg_ref, lam_ref, *, tt, bb):
    win = io.ext()
    xc = None
    for j in range(CONV_W):
        k = CONV_W - 1 - j
        x = win if k == 0 else pltpu.roll(win, k, 1)
        term = x[:, TAIL:, :] * io.after(cw_ref[j:j + 1, :])
        xc = term if xc is None else xc + term
    xc = xc + cb_ref[...]
    n = bb * tt
    xc = xc.reshape(n, D_R)
    yield
    gates = jnp.dot(xc.astype(BF16), wg_ref[...], preferred_element_type=F32) + bg_ref[...]
    r = _sigmoid(gates[:, :D_R])
    i = _sigmoid(gates[:, D_R:])
    log_a = -C_RG * r * _softplus(-io.after(lam_ref[...]))
    a = jnp.exp(log_a)
    m2 = -jnp.tanh(log_a) * (a * a + 1.0)
    u = jnp.where(m2 > 0.0, m2 * lax.rsqrt(m2), 0.0) * i * xc
    yield

    ng = n // SUBLANES
    a = a.reshape(ng, SUBLANES, D_R)
    u = u.reshape(ng, SUBLANES, D_R)
    tpos = lax.broadcasted_iota(jnp.int32, (ng, SUBLANES, D_R), 1)
    s = 1
    while s < SUBLANES:
        a_sh = pltpu.roll(a, s, 1)
        u_sh = pltpu.roll(u, s, 1)
        m = tpos >= s
        u = jnp.where(m, u + a * u_sh, u)
        a = jnp.where(m, a * a_sh, a)
        s *= 2
    yield
    gpb = tt // SUBLANES
    if gpb == 1:
        hs = u + a * io.h0()
        h_last = hs[:, SUBLANES - 1:SUBLANES, :]
    else:
        assert bb == 1
        h_in = io.h0()[0]
        groups = []
        for g in range(gpb):
            hg = u[g] + a[g] * h_in
            groups.append(hg)
            h_in = hg[SUBLANES - 1:SUBLANES, :]
        hs = jnp.concatenate(groups, axis=0)
        h_last = h_in.reshape(1, 1, D_R)
    y = hs.reshape(n, D_R) * jax.nn.gelu(io.gate(), approximate=True)
    io.put_y(y.reshape(bb, tt, D_R))
    io.put_h(h_last)
    yield


def _lru_body(xr_ref, gate_ref, cs_ref, h0_ref, cw_ref, cb_ref, wg_ref, bg_ref, lam_ref,
              y_ref, hnew_ref, csnew_ref, ext_ref, *, tt, bb, n_t):
    t = pl.program_id(1)

    @pl.when(t == 0)
    def _():
        ext_ref[:, TAIL - (CONV_W - 1):TAIL, :] = cs_ref[...]
        hnew_ref[...] = h0_ref[...]

    ext_ref[:, TAIL:TAIL + tt, :] = xr_ref[...]

    def put_y(val):
        y_ref[...] = val.astype(BF16)

    def put_h(val):
        hnew_ref[...] = val

    io = _Io(ext=lambda: ext_ref[...], gate=lambda: gate_ref[...].reshape(bb * tt, D_R),
             h0=lambda: hnew_ref[...], put_y=put_y, put_h=put_h)
    for _ in _lru_stream(io, cw_ref, cb_ref, wg_ref, bg_ref, lam_ref, tt=tt, bb=bb):
        pass

    tail = ext_ref[:, tt:tt + TAIL, :]
    ext_ref[:, 0:TAIL, :] = tail

    @pl.when(t == n_t - 1)
    def _():
        csnew_ref[...] = ext_ref[:, TAIL - (CONV_W - 1):TAIL, :]


def _lru(xr, gate, cs, h0, lw, l, *, tt, bb):
    b, t, _ = xr.shape
    n_t = t // tt
    assert n_t * tt == t and b % bb == 0 and tt % SUBLANES == 0
    ct = lambda i, j: (l, i, 0, 0)
    tok = pl.BlockSpec((bb, tt, D_R), lambda i, j: (i, j, 0))
    const = lambda shape: pl.BlockSpec(shape, lambda i, j: (0,) * len(shape))
    return pl.pallas_call(
        functools.partial(_lru_body, tt=tt, bb=bb, n_t=n_t),
        grid=(b // bb, n_t),
        in_specs=[tok, tok,
                  pl.BlockSpec((None, bb, CONV_W - 1, D_R), ct),
                  pl.BlockSpec((None, bb, 1, D_R), ct),
                  const((CONV_W, D_R)), const((1, D_R)), const((D_R, 2 * D_R)), const((1, 2 * D_R)),
                  const((1, D_R))],
        out_specs=[tok,
                   pl.BlockSpec((bb, 1, D_R), lambda i, j: (i, 0, 0)),
                   pl.BlockSpec((bb, CONV_W - 1, D_R), lambda i, j: (i, 0, 0))],
        out_shape=[jax.ShapeDtypeStruct((b, t, D_R), BF16),
                   jax.ShapeDtypeStruct((b, 1, D_R), F32),
                   jax.ShapeDtypeStruct((b, CONV_W - 1, D_R), F32)],
        scratch_shapes=[pltpu.VMEM((bb, tt + TAIL, D_R), F32)],
        compiler_params=pltpu.CompilerParams(dimension_semantics=("arbitrary", "arbitrary"),
                                             vmem_limit_bytes=VMEM_LIMIT),
        name="lru",
    )(xr, gate, cs, h0, lw["clw"], lw["clb"], lw["wgate"], lw["bgate"], lw["lam"])


def _interleave(streams, plan, lead, set_dep):
    live = dict(streams)

    def step(key):
        if key not in live:
            return
        try:
            val = next(live[key])
            if key == lead:
                set_dep(val)
        except StopIteration:
            del live[key]

    for key in plan:
        step(key)
    while live:
        for key in list(live):
            step(key)


def _in_mix_body(flag_ref, x_ref, n1_ref, wg_ref, wu_ref, wd_ref, nm_ref, win_ref,
                 dcs_ref, ds0_ref, cw_ref, alog_ref, dtb_ref, on_ref,
                 lcs_ref, lh0_ref, lcw_ref, lcb_ref, lwg_ref, lbg_ref, lam_ref,
                 x1_ref, o_ref, y_ref, snew_ref, dcsnew_ref, hnew_ref, lcsnew_ref,
                 qkv_scr, z_scr, ba_scr, xr_scr, gate_scr, dtail_scr, ltail_scr, s_scr, h_scr,
                 *, tiles_per_seq):
    tt = MIX_TILE
    s = pl.program_id(0)
    slot_a = lax.rem(s, 2)
    slot_b = 1 - slot_a
    first = lax.rem(s + tiles_per_seq - 1, tiles_per_seq) == 0
    tails = slice(TAIL - (CONV_W - 1), TAIL)
    keep = flag_ref[0] == 1
    dep = [None]

    def after(x):
        if dep[0] is None:
            return x
        reps = x.shape[-1] // DK
        d = dep[0] if reps == 1 else jnp.concatenate([dep[0]] * reps, axis=-1)
        return jnp.where(keep, x, d)

    def set_dep(val):
        dep[0] = val

    @pl.when(s == 0)
    def _():
        for ref in (qkv_scr, z_scr, ba_scr, xr_scr, gate_scr, dtail_scr, ltail_scr, s_scr, h_scr):
            ref[...] = jnp.zeros(ref.shape, F32)

    def stream_a():
        x = x_ref[...]
        h = _rms(x, n1_ref[...]).astype(BF16)

        def down(gu, acc):
            g, u, c0 = gu
            a = (_silu(g) * u).astype(BF16)
            d = jnp.dot(a, wd_ref[c0:c0 + FF_PIECE, :], preferred_element_type=F32)
            return d if acc is None else acc + d

        acc, pending = None, None
        for c0 in range(0, D_FF, FF_PIECE):
            g = jnp.dot(h, wg_ref[:, c0:c0 + FF_PIECE], preferred_element_type=F32)
            u = jnp.dot(h, wu_ref[:, c0:c0 + FF_PIECE], preferred_element_type=F32)
            if pending is not None:
                acc = down(pending, acc)
            pending = (g, u, c0)
            yield g[0:1, 0:DK]
        acc = down(pending, acc)
        x1 = x + 0.5 * acc
        x1_ref[...] = x1
        hm = _rms(x1, nm_ref[...]).astype(BF16)
        yield acc[0:1, 0:DK]
        dsts = ((qkv_scr, TAIL, 0, D_QKV), (z_scr, 0, D_QKV, D_A), (xr_scr, TAIL, D_QKV + D_A, D_R),
                (gate_scr, 0, D_QKV + D_A + D_R, D_R), (ba_scr, 0, D_QKV + D_A + 2 * D_R, D_BA))
        for dst, row0, col0, width in dsts:
            for d0 in range(0, width, PROJ_PIECE):
                w = min(PROJ_PIECE, width - d0)
                p = jnp.dot(hm, win_ref[:, col0 + d0:col0 + d0 + w], preferred_element_type=F32)
                dst[slot_a, row0:row0 + tt, d0:d0 + w] = p
                yield p[0:1, 0:DK]

    qkv_scr[slot_b, tails, :] = jnp.where(first, dcs_ref[0], dtail_scr[tails, :])
    xr_scr[slot_b, tails, :] = jnp.where(first, lcs_ref[0], ltail_scr[tails, :])

    def put_o(tile, h, val):
        o_ref[tile[2]:tile[2] + CHUNK, h * DV:(h + 1) * DV] = val.astype(BF16)

    def put_state(b, h, val):
        s_scr[h] = val
        snew_ref[0, h] = val

    def put_y(val):
        y_ref[...] = val.reshape(tt, D_R).astype(BF16)

    def put_h(val):
        h_scr[...] = val
        hnew_ref[...] = val

    dio = _Io(ext=lambda tile, col0, w: qkv_scr[slot_b, tile[2]:tile[2] + TAIL + CHUNK, col0:col0 + w][None],
              ba=lambda tile: ba_scr[slot_b, tile[2]:tile[2] + CHUNK, :],
              z=lambda tile, col0: z_scr[slot_b, tile[2]:tile[2] + CHUNK, col0:col0 + DV],
              put_o=put_o, get_state=lambda b, h: jnp.where(first, ds0_ref[0, h], s_scr[h]),
              put_state=put_state, after=after)
    lio = _Io(ext=lambda: xr_scr[slot_b][None],
              gate=lambda: gate_scr[slot_b], h0=lambda: jnp.where(first, lh0_ref[...], h_scr[...]),
              put_y=put_y, put_h=put_h, after=after)
    tiles = tuple((0, 1, j * CHUNK) for j in range(tt // CHUNK))
    _interleave({"a": stream_a(),
                 "d": _delta_stream(dio, cw_ref, alog_ref, dtb_ref, on_ref, tiles=tiles, seg=CHUNK),
                 "l": _lru_stream(lio, lcw_ref, lcb_ref, lwg_ref, lbg_ref, lam_ref, tt=tt, bb=1)},
                MIX_PLAN, "a", set_dep)

    dtail_scr[...] = qkv_scr[slot_b, tt:tt + TAIL, :]
    ltail_scr[...] = xr_scr[slot_b, tt:tt + TAIL, :]
    dcsnew_ref[0] = qkv_scr[slot_b, tt + TAIL - (CONV_W - 1):tt + TAIL, :]
    lcsnew_ref[0] = xr_scr[slot_b, tt + TAIL - (CONV_W - 1):tt + TAIL, :]


def _in_mix(x, sdc, sd, slc, sl, lw, l, *, batch):
    r = x.shape[0]
    tt = MIX_TILE
    n_tiles = r // tt
    tiles_per_seq = n_tiles // batch
    assert n_tiles * tt == r and tiles_per_seq * batch == n_tiles
    cur = lambda s: (jnp.minimum(s, n_tiles - 1), 0)
    prev = lambda s: (jnp.maximum(s - 1, 0), 0)
    seq3 = lambda s: (jnp.maximum(s - 1, 0) // tiles_per_seq, 0, 0)
    seq4 = lambda s: (jnp.maximum(s - 1, 0) // tiles_per_seq, 0, 0, 0)
    layer = lambda shape: pl.BlockSpec((None,) + shape, lambda s: (l,) + (0,) * len(shape),
                                       pipeline_mode=pl.Buffered(1))
    return pl.pallas_call(
        functools.partial(_in_mix_body, tiles_per_seq=tiles_per_seq),
        grid=(n_tiles + 1,),
        in_specs=[pl.BlockSpec(memory_space=pltpu.SMEM),
                  pl.BlockSpec((tt, D_MODEL), cur), _resident((1, D_MODEL)),
                  _of_layer((D_MODEL, D_FF), l), _of_layer((D_MODEL, D_FF), l), _of_layer((D_FF, D_MODEL), l),
                  _resident((1, D_MODEL)), _resident((D_MODEL, D_PROJ)),
                  layer((1, CONV_W - 1, D_QKV)), layer((1, H_A, DK, DV)),
                  _resident((CONV_W, D_QKV)), _resident((1, D_BA)), _resident((1, D_BA)), _resident((1, DV)),
                  layer((1, CONV_W - 1, D_R)), layer((1, 1, D_R)),
                  _resident((CONV_W, D_R)), _resident((1, D_R)), _resident((D_R, 2 * D_R)),
                  _resident((1, 2 * D_R)), _resident((1, D_R))],
        out_specs=[pl.BlockSpec((tt, D_MODEL), cur), pl.BlockSpec((tt, D_A), prev), pl.BlockSpec((tt, D_R), prev),
                   pl.BlockSpec((1, H_A, DK, DV), seq4), pl.BlockSpec((1, CONV_W - 1, D_QKV), seq3),
                   pl.BlockSpec((1, 1, D_R), seq3), pl.BlockSpec((1, CONV_W - 1, D_R), seq3)],
        out_shape=[jax.ShapeDtypeStruct((r, D_MODEL), F32), jax.ShapeDtypeStruct((r, D_A), BF16),
                   jax.ShapeDtypeStruct((r, D_R), BF16),
                   jax.ShapeDtypeStruct((batch, H_A, DK, DV), F32),
                   jax.ShapeDtypeStruct((batch, CONV_W - 1, D_QKV), F32),
                   jax.ShapeDtypeStruct((batch, 1, D_R), F32),
                   jax.ShapeDtypeStruct((batch, CONV_W - 1, D_R), F32)],
        scratch_shapes=[pltpu.VMEM((2, TAIL + tt, D_QKV), F32), pltpu.VMEM((2, tt, D_A), F32),
                        pltpu.VMEM((2, tt, D_BA), F32), pltpu.VMEM((2, TAIL + tt, D_R), F32),
                        pltpu.VMEM((2, tt, D_R), F32), pltpu.VMEM((TAIL, D_QKV), F32),
                        pltpu.VMEM((TAIL, D_R), F32), pltpu.VMEM((H_A, DK, DV), F32),
                        pltpu.VMEM((1, 1, D_R), F32)],
        compiler_params=pltpu.CompilerParams(dimension_semantics=("arbitrary",),
                                             vmem_limit_bytes=VMEM_LIMIT),
        name="in_mix",
    )(jnp.ones((1,), jnp.int32), x, lw["n1"], lw["f1g"], lw["f1u"], lw["f1d"], lw["nm"], lw["w_in"],
      sdc, sd, lw["cqkv"], lw["alog"], lw["dtb"], lw["onorm"],
      slc, sl, lw["clw"], lw["clb"], lw["wgate"], lw["bgate"], lw["lam"])


def _prep_layer(l, w, stacked):
    row = lambda v: v.reshape(1, -1).astype(F32)
    c1 = D_QKV
    c2 = c1 + D_A
    c3 = c2 + H_A
    c4 = c3 + H_A
    c5 = c4 + D_R
    w_in = w["w_in"][l].astype(BF16)
    pad = jnp.zeros((D_MODEL, D_BA - 2 * H_A), BF16)
    w_in_r = jnp.concatenate([w_in[:, :c2], w_in[:, c4:c5], w_in[:, c5:], w_in[:, c2:c4], pad], axis=1)
    lane_pad = lambda v: jnp.zeros((1, D_BA), F32).at[0, H_A:2 * H_A].set(v)
    eye = jnp.eye(NB_R, dtype=F32)
    bd = lambda wb: jnp.einsum("ncd,nm->ncmd", wb, eye).reshape(D_R, D_R)
    return dict(
        l=l, n1=row(w["norm_ffn1"][l]), f1g=stacked["f1g"], f1u=stacked["f1u"], f1d=stacked["f1d"],
        nm=row(w["norm_mix"][l]), w_in=w_in_r,
        cqkv=w["conv_qkv"][l], alog=lane_pad(w["a_log"][l]), dtb=lane_pad(w["dt_bias"][l]),
        onorm=row(w["norm_delta_out"][l]),
        clw=w["conv_lru_w"][l], clb=row(w["conv_lru_b"][l]),
        wgate=jnp.concatenate([bd(w["w_rgate"][l]), bd(w["w_igate"][l])], axis=1).astype(BF16),
        bgate=jnp.concatenate([row(w["b_rgate"][l]), row(w["b_igate"][l])], axis=1),
        lam=row(w["lru_lambda"][l]),
        w_out=stacked["w_out"], n2=row(w["norm_ffn2"][l]),
        f2g=stacked["f2g"], f2u=stacked["f2u"], f2d=stacked["f2d"],
    )


def _run_group(x, sd, sdc, sl, slc, layers, nf, *, seg=None, delta_tt=None, delta_bb=None, tiles=None,
               lru_tt=None, lru_bb=None, fused=False, states_only=False):
    b, t, _ = x.shape
    xf = x.reshape(b * t, D_MODEL)
    nd, ndc, nl, nlc = [], [], [], []
    s_all = None
    n_layers = len(layers)
    for l, lw in enumerate(layers):
        if fused:
            x1, o, y, s_new, cs_new, h_new, lcs_new = _in_mix(xf, sdc, sd, slc, sl, lw, l, batch=b)
        else:
            x1, qkv, z, xr, gate, ba = _ffn_in(xf, lw)
            r3 = lambda v: v.reshape(b, t, v.shape[-1])
            o, s_all, cs_new = _delta(r3(qkv), r3(z), r3(ba), sdc, sd, lw, l, s_all,
                                      seg=seg, tt=delta_tt, bb=delta_bb, tiles=tiles)
            s_new = None
            y, h_new, lcs_new = _lru(r3(xr), r3(gate), slc, sl, lw, l, tt=lru_tt, bb=lru_bb)
        last = l == n_layers - 1
        if not (last and states_only):
            xf = _out_ffn(x1, o.reshape(b * t, D_A), y.reshape(b * t, D_R), lw, nf, last)
        nd.append(s_new)
        ndc.append(cs_new)
        nl.append(h_new)
        nlc.append(lcs_new)
    stack = jnp.stack
    y_out = None if states_only else xf.reshape(b, t, D_MODEL)
    return y_out, (stack(nd) if fused else s_all), stack(ndc), stack(nl), stack(nlc)


def kernel(x_prompt, x_sample, state_delta, state_delta_conv, state_lru, state_lru_conv, meta_tokens, norm_ffn1, w_ffn1_gate, w_ffn1_up, w_ffn1_down, norm_mix, w_in, conv_qkv, a_log, dt_bias, norm_delta_out, conv_lru_w, conv_lru_b, w_rgate, b_rgate, w_igate, b_igate, lru_lambda, w_out, norm_ffn2, w_ffn2_gate, w_ffn2_up, w_ffn2_down, norm_final):
    w = dict(norm_ffn1=norm_ffn1, w_ffn1_gate=w_ffn1_gate, w_ffn1_up=w_ffn1_up, w_ffn1_down=w_ffn1_down,
             norm_mix=norm_mix, w_in=w_in, conv_qkv=conv_qkv, a_log=a_log, dt_bias=dt_bias,
             norm_delta_out=norm_delta_out, conv_lru_w=conv_lru_w, conv_lru_b=conv_lru_b,
             w_rgate=w_rgate, b_rgate=b_rgate, w_igate=w_igate, b_igate=b_igate, lru_lambda=lru_lambda,
             w_out=w_out, norm_ffn2=norm_ffn2, w_ffn2_gate=w_ffn2_gate, w_ffn2_up=w_ffn2_up,
             w_ffn2_down=w_ffn2_down)
    depth = norm_ffn1.shape[0]
    stacked = dict(f1g=w_ffn1_gate.astype(BF16), f1u=w_ffn1_up.astype(BF16), f1d=w_ffn1_down.astype(BF16),
                   f2g=w_ffn2_gate.astype(BF16), f2u=w_ffn2_up.astype(BF16), f2d=w_ffn2_down.astype(BF16),
                   w_out=w_out.astype(BF16))
    layers = [_prep_layer(l, w, stacked) for l in range(depth)]
    nf = norm_final.reshape(1, D_MODEL).astype(F32)
    bp, seq, _ = x_prompt.shape
    bs, dseq, _ = x_sample.shape
    seg_b = CHUNK // dseq

    zeros = lambda *s: jnp.zeros((depth,) + s, F32)
    _, md, mdc, ml, mlc = _run_group(
        meta_tokens.astype(F32)[None], zeros(1, H_A, DK, DV), zeros(1, CONV_W - 1, D_QKV),
        zeros(1, 1, D_R), zeros(1, CONV_W - 1, D_R), layers, nf,
        seg=N_META, delta_tt=N_META, delta_bb=1, tiles=((0, 1, 0),), lru_tt=N_META, lru_bb=1,
        states_only=True)
    y_prompt, pd, pdc, plr, plc = _run_group(x_prompt, md, mdc, ml, mlc, layers, nf, fused=True)
    y_sample, sdn, sdcn, sln, slcn = _run_group(
        x_sample, state_delta, state_delta_conv, state_lru.reshape(depth, bs, 1, D_R), state_lru_conv,
        layers, nf, seg=dseq, delta_tt=dseq, delta_bb=SAMPLE_TILES_PER_STEP * seg_b,
        tiles=tuple((j * seg_b, seg_b, 0) for j in range(SAMPLE_TILES_PER_STEP)),
        lru_tt=dseq, lru_bb=2 * seg_b)

    return (y_prompt, y_sample, pd, pdc, plr.reshape(depth, bp, D_R), plc,
            sdn, sdcn, sln.reshape(depth, bs, D_R), slcn)
```

```python
import functools
import math

import jax
import jax.numpy as jnp
from jax import lax
from jax.experimental import pallas as pl
from jax.experimental.pallas import tpu as pltpu

F32 = jnp.float32
BF16 = jnp.bfloat16

D_MODEL = 1024
D_FF = 2816
D_A = 512
D_R = 512
H_A = 4
DK = 128
DV = 128
CHUNK = 64
CONV_W = 4
NB_R = 8
C_RG = 8.0
EPS = 1e-6
N_META = 16
D_QKV = 3 * D_A
D_BA = 128
D_PROJ = D_QKV + D_A + 2 * D_R + D_BA

SUBLANES = 8
TAIL = SUBLANES
VMEM_BYTES_V7X = 64 * 1024 * 1024
VMEM_LIMIT = VMEM_BYTES_V7X - 8 * 1024 * 1024
ROW_TILE = 512
MIX_TILE = 256
FF_PIECE = 256
PROJ_PIECE = 256
MIX_PLAN = "ad" * 7 + ("adl" + "ad") * 3 + "adl" + "ad" * 8
SAMPLE_TILES_PER_STEP = 2


def _rms(x, w):
    ms = jnp.mean(x * x, axis=-1, keepdims=True)
    return x * lax.rsqrt(ms + EPS) * w


def _sigmoid(x):
    return 1.0 / (1.0 + jnp.exp(-x))


def _silu(x):
    return x * _sigmoid(x)


def _softplus(x):
    t = jnp.exp(-jnp.abs(x))
    u = 1.0 + t
    d = u - 1.0
    log1p_t = jnp.where(d == 0.0, t, jnp.log(u) * (t / jnp.where(d == 0.0, 1.0, d)))
    return jnp.maximum(x, 0.0) + log1p_t


def _dot(a, b):
    return jnp.dot(a.astype(BF16), b.astype(BF16), preferred_element_type=F32)


def _dot_nt(a, b):
    return lax.dot_general(a.astype(BF16), b.astype(BF16), (((1,), (1,)), ((), ())),
                           preferred_element_type=F32)


def _dot_tn(a, b):
    return lax.dot_general(a.astype(BF16), b.astype(BF16), (((0,), (0,)), ((), ())),
                           preferred_element_type=F32)


def _split3(a):
    a1 = a.astype(BF16)
    r = a - a1.astype(F32)
    a2 = r.astype(BF16)
    a3 = (r - a2.astype(F32)).astype(BF16)
    return a1, a2, a3


def _ffn_pieces(h, wg_ref, wu_ref, wd_ref):
    def down(gu, acc):
        g, u, c0 = gu
        a = (_silu(g) * u).astype(BF16)
        d = jnp.dot(a, wd_ref[c0:c0 + FF_PIECE, :], preferred_element_type=F32)
        return d if acc is None else acc + d

    acc, pending = None, None
    for c0 in range(0, D_FF, FF_PIECE):
        g = jnp.dot(h, wg_ref[:, c0:c0 + FF_PIECE], preferred_element_type=F32)
        u = jnp.dot(h, wu_ref[:, c0:c0 + FF_PIECE], preferred_element_type=F32)
        if pending is not None:
            acc = down(pending, acc)
        pending = (g, u, c0)
        yield g
    yield down(pending, acc)


def _ffn(x, nw, wg_ref, wu_ref, wd_ref):
    h = _rms(x, nw).astype(BF16)
    for acc in _ffn_pieces(h, wg_ref, wu_ref, wd_ref):
        pass
    return x + 0.5 * acc


def _ffn_in_body(x_ref, n1_ref, wg_ref, wu_ref, wd_ref, nm_ref, win_ref,
                 x1_ref, qkv_ref, z_ref, xr_ref, gate_ref, ba_ref):
    x1 = _ffn(x_ref[...], n1_ref[...], wg_ref, wu_ref, wd_ref)
    x1_ref[...] = x1
    hm = _rms(x1, nm_ref[...]).astype(BF16)
    p = jnp.dot(hm, win_ref[...], preferred_element_type=F32)
    c1 = D_QKV
    c2 = c1 + D_A
    c3 = c2 + D_R
    c4 = c3 + D_R
    qkv_ref[...] = p[:, :c1]
    z_ref[...] = p[:, c1:c2]
    xr_ref[...] = p[:, c2:c3]
    gate_ref[...] = p[:, c3:c4]
    ba_ref[...] = p[:, c4:]


def _out_ffn_body(x_ref, o_ref, y_ref, wo1_ref, wo2_ref, n2_ref, wg_ref, wu_ref, wd_ref, nf_ref,
                  out_ref, *, final):
    x = x_ref[...] + (_dot(o_ref[...], wo1_ref[...]) + _dot(y_ref[...], wo2_ref[...]))
    x2 = _ffn(x, n2_ref[...], wg_ref, wu_ref, wd_ref)
    if final:
        x2 = _rms(x2, nf_ref[...])
    out_ref[...] = x2


def _resident(shape):
    nd = len(shape)
    return pl.BlockSpec(shape, lambda *_: (0,) * nd, pipeline_mode=pl.Buffered(1))


def _of_layer(shape, l, first=0):
    nd = len(shape)
    return pl.BlockSpec((None,) + shape, lambda *_: (l, first) + (0,) * (nd - 1), pipeline_mode=pl.Buffered(1))


def _rows(tm, width):
    return pl.BlockSpec((tm, width), lambda i: (i, 0))


def _ffn_in(x, lw):
    r = x.shape[0]
    l = lw["l"]
    tm = min(ROW_TILE, r)
    widths = (D_MODEL, D_QKV, D_A, D_R, D_R, D_BA)
    return pl.pallas_call(
        _ffn_in_body,
        grid=(pl.cdiv(r, tm),),
        in_specs=[_rows(tm, D_MODEL), _resident((1, D_MODEL)),
                  _of_layer((D_MODEL, D_FF), l), _of_layer((D_MODEL, D_FF), l), _of_layer((D_FF, D_MODEL), l),
                  _resident((1, D_MODEL)), _of_layer((D_MODEL, D_PROJ), l)],
        out_specs=[_rows(tm, w) for w in widths],
        out_shape=[jax.ShapeDtypeStruct((r, w), F32) for w in widths],
        compiler_params=pltpu.CompilerParams(dimension_semantics=("parallel",),
                                             vmem_limit_bytes=VMEM_LIMIT),
        name="ffn_in",
    )(x, lw["n1"], lw["f1g"], lw["f1u"], lw["f1d"], lw["nm"], lw["w_in"])


def _out_ffn(x, o, y, lw, nf, final):
    r = x.shape[0]
    l = lw["l"]
    tm = min(ROW_TILE, r)
    return pl.pallas_call(
        functools.partial(_out_ffn_body, final=final),
        grid=(pl.cdiv(r, tm),),
        in_specs=[_rows(tm, D_MODEL), _rows(tm, D_A), _rows(tm, D_R),
                  _of_layer((D_A, D_MODEL), l, 0), _of_layer((D_R, D_MODEL), l, 1), _resident((1, D_MODEL)),
                  _of_layer((D_MODEL, D_FF), l), _of_layer((D_MODEL, D_FF), l), _of_layer((D_FF, D_MODEL), l),
                  _resident((1, D_MODEL))],
        out_specs=_rows(tm, D_MODEL),
        out_shape=jax.ShapeDtypeStruct((r, D_MODEL), F32),
        compiler_params=pltpu.CompilerParams(dimension_semantics=("parallel",),
                                             vmem_limit_bytes=VMEM_LIMIT),
        name="out_ffn",
    )(x, o, y, lw["w_out"], lw["w_out"], lw["n2"], lw["f2g"], lw["f2u"], lw["f2d"], nf)


class _Io:
    def __init__(self, **fns):
        self.after = lambda x: x
        self.__dict__.update(fns)


def _delta_stream(io, cw_ref, alog_ref, dtb_ref, on_ref, *, tiles, seg):
    c = seg
    nb = tiles[0][1]
    r = nb * c
    levels = int(math.log2(c))
    assert 2 ** levels == c
    row = lax.broadcasted_iota(jnp.int32, (r, r), 0)
    col = lax.broadcasted_iota(jnp.int32, (r, r), 1)
    same = lax.shift_right_logical(row, levels) == lax.shift_right_logical(col, levels)
    causal = same & (row >= col)
    strict = same & (row > col)
    tri = causal.astype(BF16)
    ones_seg = same.astype(BF16)
    eye = (row == col).astype(F32)
    sel_r = lax.broadcasted_iota(jnp.int32, (SUBLANES, D_BA), 0)
    sel_c = lax.broadcasted_iota(jnp.int32, (SUBLANES, D_BA), 1)
    sel = ((sel_c == sel_r + H_A) & (sel_r < H_A)).astype(BF16)

    def conv_silu(tile, col0):
        win = io.ext(tile, col0, DK)
        acc = None
        for j in range(CONV_W):
            k = CONV_W - 1 - j
            x = win if k == 0 else pltpu.roll(win, k, 1)
            term = x[:, TAIL:, :].reshape(r, DK) * io.after(cw_ref[j:j + 1, col0:col0 + DK])
            acc = term if acc is None else acc + term
        return _silu(acc)

    def exact_dot(lhs01, x, nt=False):
        out = None
        for p in _split3(x):
            if nt:
                d = lax.dot_general(lhs01, p, (((1,), (1,)), ((), ())), preferred_element_type=F32)
            else:
                d = jnp.dot(lhs01, p, preferred_element_type=F32)
            out = d if out is None else out + d
        return out

    items = [(j, h) for j in range(len(tiles)) for h in range(H_A)]

    gcum, gtot, grow, eg_all, beta_all = [], [], [], [], []
    for tile in tiles:
        ba = io.ba(tile)
        g_all = -jnp.exp(io.after(alog_ref[...])) * _softplus(ba + dtb_ref[...])
        beta_all.append(_sigmoid(ba))
        gcum.append(exact_dot(tri, g_all))
        gtot.append(exact_dot(ones_seg, g_all) if nb > 1 else gcum[-1][r - 1:r, :])
    yield
    for j in range(len(tiles)):
        grow.append(exact_dot(sel, gcum[j], nt=True))
        eg_all.append(jnp.exp(gcum[j]))
    yield

    qg, kn, kb, kdec, decay, rhs = {}, {}, {}, {}, {}, {}
    for (j, h) in items:
        tile = tiles[j]
        qh = conv_silu(tile, h * DK)
        kh = conv_silu(tile, D_A + h * DK)
        vh = conv_silu(tile, 2 * D_A + h * DV)
        qn = qh * lax.rsqrt(jnp.sum(qh * qh, axis=-1, keepdims=True) + EPS) * (DK ** -0.5)
        k_n = kh * lax.rsqrt(jnp.sum(kh * kh, axis=-1, keepdims=True) + EPS)
        beta = beta_all[j][:, h:h + 1]
        gcol = gcum[j][:, H_A + h:H_A + h + 1]
        egcol = eg_all[j][:, H_A + h:H_A + h + 1]
        glast = gtot[j][:, H_A + h:H_A + h + 1]
        dmat = gcol - grow[j][h:h + 1, :]
        decay[j, h] = jnp.where(causal, jnp.exp(jnp.where(causal, dmat, 0.0)), 0.0)
        kn[j, h] = k_n
        kb[j, h] = k_n * beta
        qg[j, h] = (qn, qn * egcol)
        kdec[j, h] = k_n * jnp.exp(glast - gcol)
        rhs[j, h] = jnp.concatenate([vh * beta, kb[j, h] * egcol], axis=-1)
        if h == H_A - 1:
            yield

    kq = {it: _dot_nt(jnp.concatenate([kb[it], qg[it][0]], axis=0), kn[it]) for it in items}
    lmat = {it: jnp.where(strict, kq[it][:r] * decay[it], 0.0) for it in items}
    attn = {it: kq[it][r:] * decay[it] for it in items}
    yield

    tinv = {it: eye - lmat[it] for it in items}
    m = {it: _dot(lmat[it], lmat[it]) for it in items}
    yield
    for _ in range(levels - 2):
        tinv = {it: tinv[it] + _dot(tinv[it], m[it]) for it in items}
        m = {it: _dot(m[it], m[it]) for it in items}
        yield
    tinv = {it: tinv[it] + _dot(tinv[it], m[it]) for it in items}
    yield
    sol = {it: _dot(tinv[it], rhs[it]) for it in items}
    yield

    state = {}

    def get_state(b, h):
        if (b, h) not in state:
            state[b, h] = io.get_state(b, h)
        return state[b, h]

    for j, tile in enumerate(tiles):
        b0 = tile[0]
        segs = [(s, slice(s * c, (s + 1) * c)) for s in range(nb)]
        ws = {}
        for h in range(H_A):
            for s, rs in segs:
                lhs = jnp.concatenate([sol[j, h][rs, DV:], qg[j, h][1][rs]], axis=0)
                ws[h, s] = _dot(lhs, get_state(b0 + s, h))
        yield
        v_new, av = {}, {}
        for h in range(H_A):
            w_s = jnp.concatenate([ws[h, s][:c] for s, _ in segs], axis=0) if nb > 1 else ws[h, 0][:c]
            v_new[h] = sol[j, h][:, :DV] - w_s
        for h in range(H_A):
            av[h] = _dot(attn[j, h], v_new[h])
        for h in range(H_A):
            for s, rs in segs:
                g0 = rs.start if nb > 1 else 0
                scale = jnp.exp(gtot[j][g0:g0 + 1, H_A + h:H_A + h + 1])
                state[b0 + s, h] = state[b0 + s, h] * scale + _dot_tn(kdec[j, h][rs], v_new[h][rs])
        for h in range(H_A):
            q_s = jnp.concatenate([ws[h, s][c:] for s, _ in segs], axis=0) if nb > 1 else ws[h, 0][c:]
            on = _rms(q_s + av[h], io.after(on_ref[...]))
            io.put_o(tile, h, on * _silu(io.z(tile, h * DV)))
        if j == len(tiles) - 1:
            for (b, h), val in state.items():
                io.put_state(b, h, val)
        yield


def _delta_body(qkv_ref, z_ref, ba_ref, cs_ref, s0_ref, cw_ref, alog_ref, dtb_ref, on_ref, *rest,
                tiles, seg, tt, n_t, fill_layers):
    if fill_layers:
        o_ref, sall_ref, csnew_ref, ext_ref = rest
        snew_ref = sall_ref.at[0]
    else:
        _, o_ref, snew_ref, csnew_ref, ext_ref = rest
    c = seg
    nb = tiles[0][1]
    r = nb * c
    t = pl.program_id(1)

    @pl.when(t == 0)
    def _():
        ext_ref[:, TAIL - (CONV_W - 1):TAIL, :] = cs_ref[...]
        snew_ref[...] = s0_ref[...]

    ext_ref[:, TAIL:TAIL + tt, :] = qkv_ref[...]

    def rows(ref, tile, off, col0, width):
        b0, _, t0 = tile
        return ref[b0:b0 + nb, off + t0:off + t0 + c, col0:col0 + width].reshape(r, width)

    def put_o(tile, h, val):
        b0, _, t0 = tile
        o_ref[b0:b0 + nb, t0:t0 + c, h * DV:(h + 1) * DV] = val.reshape(nb, c, DV).astype(BF16)

    def put_state(b, h, val):
        snew_ref[b, h] = val

    def ext(tile, col0, w):
        b0, _, t0 = tile
        return ext_ref[b0:b0 + nb, t0:t0 + TAIL + c, col0:col0 + w]

    io = _Io(ext=ext, ba=lambda tile: rows(ba_ref, tile, 0, 0, D_BA),
             z=lambda tile, col0: rows(z_ref, tile, 0, col0, DV),
             put_o=put_o, get_state=lambda b, h: snew_ref[b, h], put_state=put_state)
    for _ in _delta_stream(io, cw_ref, alog_ref, dtb_ref, on_ref, tiles=tiles, seg=seg):
        pass

    tail = ext_ref[:, tt:tt + TAIL, :]
    ext_ref[:, 0:TAIL, :] = tail

    @pl.when(t == n_t - 1)
    def _():
        csnew_ref[...] = ext_ref[:, TAIL - (CONV_W - 1):TAIL, :]
        if fill_layers:
            for d in range(1, fill_layers):
                sall_ref[d] = sall_ref[0]


def _delta(qkv, z, ba, cs, s0, lw, l, s_all, *, seg, tt, bb, tiles):
    b, t, _ = qkv.shape
    depth = s0.shape[0]
    n_t = t // tt
    assert n_t * tt == t and b % bb == 0 and (l == 0) == (s_all is None)
    st = lambda i, j: (l, i, 0, 0, 0)
    ct = lambda i, j: (l, i, 0, 0)
    tok = lambda w: pl.BlockSpec((bb, tt, w), lambda i, j: (i, j, 0))
    const = lambda shape: pl.BlockSpec(shape, lambda i, j: (0,) * len(shape))
    in_specs = [tok(D_QKV), tok(D_A), tok(D_BA),
                pl.BlockSpec((None, bb, CONV_W - 1, D_QKV), ct),
                pl.BlockSpec((None, bb, H_A, DK, DV), st),
                const((CONV_W, D_QKV)), const((1, D_BA)), const((1, D_BA)), const((1, DV))]
    operands = [qkv, z, ba, cs, s0, lw["cqkv"], lw["alog"], lw["dtb"], lw["onorm"]]
    if s_all is None:
        s_spec = pl.BlockSpec((depth, bb, H_A, DK, DV), lambda i, j: (0, i, 0, 0, 0))
        aliases = {}
    else:
        s_spec = pl.BlockSpec((None, bb, H_A, DK, DV), st)
        in_specs.append(pl.BlockSpec(memory_space=pl.ANY))
        operands.append(s_all)
        aliases = {len(operands) - 1: 1}
    return pl.pallas_call(
        functools.partial(_delta_body, tiles=tiles, seg=seg, tt=tt, n_t=n_t,
                          fill_layers=depth if s_all is None else 0),
        grid=(b // bb, n_t),
        in_specs=in_specs,
        out_specs=[tok(D_A), s_spec,
                   pl.BlockSpec((bb, CONV_W - 1, D_QKV), lambda i, j: (i, 0, 0))],
        out_shape=[jax.ShapeDtypeStruct((b, t, D_A), BF16),
                   jax.ShapeDtypeStruct((depth, b, H_A, DK, DV), F32),
                   jax.ShapeDtypeStruct((b, CONV_W - 1, D_QKV), F32)],
        scratch_shapes=[pltpu.VMEM((bb, tt + TAIL, D_QKV), F32)],
        input_output_aliases=aliases,
        compiler_params=pltpu.CompilerParams(dimension_semantics=("arbitrary", "arbitrary"),
                                             vmem_limit_bytes=VMEM_LIMIT),
        name="delta",
    )(*operands)


def _lru_stream(io, cw_ref, cb_ref, wg_ref, bg_ref, lam_ref, *, tt, bb):
    win = io.ext()
    xc = None
    for j in range(CONV_W):
        k = CONV_W - 1 - j
        x = win if k == 0 else pltpu.roll(win, k, 1)
        term = x[:, TAIL:, :] * io.after(cw_ref[j:j + 1, :])
        xc = term if xc is None else xc + term
    xc = xc + cb_ref[...]
    n = bb * tt
    xc = xc.reshape(n, D_R)
    yield
    gates = jnp.dot(xc.astype(BF16), wg_ref[...], preferred_element_type=F32) + bg_ref[...]
    r = _sigmoid(gates[:, :D_R])
    i = _sigmoid(gates[:, D_R:])
    log_a = -C_RG * r * _softplus(-io.after(lam_ref[...]))
    a = jnp.exp(log_a)
    m2 = -jnp.tanh(log_a) * (a * a + 1.0)
    u = jnp.where(m2 > 0.0, m2 * lax.rsqrt(m2), 0.0) * i * xc
    yield

    ng = n // SUBLANES
    a = a.reshape(ng, SUBLANES, D_R)
    u = u.reshape(ng, SUBLANES, D_R)
    tpos = lax.broadcasted_iota(jnp.int32, (ng, SUBLANES, D_R), 1)
    s = 1
    while s < SUBLANES:
        a_sh = pltpu.roll(a, s, 1)
        u_sh = pltpu.roll(u, s, 1)
        m = tpos >= s
        u = jnp.where(m, u + a * u_sh, u)
        a = jnp.where(m, a * a_sh, a)
        s *= 2
    yield
    gpb = tt // SUBLANES
    if gpb == 1:
        hs = u + a * io.h0()
        h_last = hs[:, SUBLANES - 1:SUBLANES, :]
    else:
        assert bb == 1
        h_in = io.h0()[0]
        groups = []
        for g in range(gpb):
            hg = u[g] + a[g] * h_in
            groups.append(hg)
            h_in = hg[SUBLANES - 1:SUBLANES, :]
        hs = jnp.concatenate(groups, axis=0)
        h_last = h_in.reshape(1, 1, D_R)
    y = hs.reshape(n, D_R) * jax.nn.gelu(io.gate(), approximate=True)
    io.put_y(y.reshape(bb, tt, D_R))
    io.put_h(h_last)
    yield


def _lru_body(xr_ref, gate_ref, cs_ref, h0_ref, cw_ref, cb_ref, wg_ref, bg_ref, lam_ref,
              y_ref, hnew_ref, csnew_ref, ext_ref, *, tt, bb, n_t):
    t = pl.program_id(1)

    @pl.when(t == 0)
    def _():
        ext_ref[:, TAIL - (CONV_W - 1):TAIL, :] = cs_ref[...]
        hnew_ref[...] = h0_ref[...]

    ext_ref[:, TAIL:TAIL + tt, :] = xr_ref[...]

    def put_y(val):
        y_ref[...] = val.astype(BF16)

    def put_h(val):
        hnew_ref[...] = val

    io = _Io(ext=lambda: ext_ref[...], gate=lambda: gate_ref[...].reshape(bb * tt, D_R),
             h0=lambda: hnew_ref[...], put_y=put_y, put_h=put_h)
    for _ in _lru_stream(io, cw_ref, cb_ref, wg_ref, bg_ref, lam_ref, tt=tt, bb=bb):
        pass

    tail = ext_ref[:, tt:tt + TAIL, :]
    ext_ref[:, 0:TAIL, :] = tail

    @pl.when(t == n_t - 1)
    def _():
        csnew_ref[...] = ext_ref[:, TAIL - (CONV_W - 1):TAIL, :]


def _lru(xr, gate, cs, h0, lw, l, *, tt, bb):
    b, t, _ = xr.shape
    n_t = t // tt
    assert n_t * tt == t and b % bb == 0 and tt % SUBLANES == 0
    ct = lambda i, j: (l, i, 0, 0)
    tok = pl.BlockSpec((bb, tt, D_R), lambda i, j: (i, j, 0))
    const = lambda shape: pl.BlockSpec(shape, lambda i, j: (0,) * len(shape))
    return pl.pallas_call(
        functools.partial(_lru_body, tt=tt, bb=bb, n_t=n_t),
        grid=(b // bb, n_t),
        in_specs=[tok, tok,
                  pl.BlockSpec((None, bb, CONV_W - 1, D_R), ct),
                  pl.BlockSpec((None, bb, 1, D_R), ct),
                  const((CONV_W, D_R)), const((1, D_R)), const((D_R, 2 * D_R)), const((1, 2 * D_R)),
                  const((1, D_R))],
        out_specs=[tok,
                   pl.BlockSpec((bb, 1, D_R), lambda i, j: (i, 0, 0)),
                   pl.BlockSpec((bb, CONV_W - 1, D_R), lambda i, j: (i, 0, 0))],
        out_shape=[jax.ShapeDtypeStruct((b, t, D_R), BF16),
                   jax.ShapeDtypeStruct((b, 1, D_R), F32),
                   jax.ShapeDtypeStruct((b, CONV_W - 1, D_R), F32)],
        scratch_shapes=[pltpu.VMEM((bb, tt + TAIL, D_R), F32)],
        compiler_params=pltpu.CompilerParams(dimension_semantics=("arbitrary", "arbitrary"),
                                             vmem_limit_bytes=VMEM_LIMIT),
        name="lru",
    )(xr, gate, cs, h0, lw["clw"], lw["clb"], lw["wgate"], lw["bgate"], lw["lam"])


def _interleave(streams, plan, lead, set_dep):
    live = dict(streams)

    def step(key):
        if key not in live:
            return
        try:
            val = next(live[key])
            if key == lead:
                set_dep(val)
        except StopIteration:
            del live[key]

    for key in plan:
        step(key)
    while live:
        for key in list(live):
            step(key)


def _in_mix_body(flag_ref, x_ref, n1_ref, wg_ref, wu_ref, wd_ref, nm_ref, win_ref,
                 dcs_ref, ds0_ref, cw_ref, alog_ref, dtb_ref, on_ref,
                 lcs_ref, lh0_ref, lcw_ref, lcb_ref, lwg_ref, lbg_ref, lam_ref,
                 x1_ref, o_ref, y_ref, snew_ref, dcsnew_ref, hnew_ref, lcsnew_ref,
                 qkv_scr, z_scr, ba_scr, xr_scr, gate_scr, dtail_scr, ltail_scr, s_scr, h_scr,
                 *, tiles_per_seq):
    tt = MIX_TILE
    s = pl.program_id(0)
    slot_a = lax.rem(s, 2)
    slot_b = 1 - slot_a
    first = lax.rem(s + tiles_per_seq - 1, tiles_per_seq) == 0
    tails = slice(TAIL - (CONV_W - 1), TAIL)
    keep = flag_ref[0] == 1
    dep = [None]

    def after(x):
        if dep[0] is None:
            return x
        reps = x.shape[-1] // DK
        d = dep[0] if reps == 1 else jnp.concatenate([dep[0]] * reps, axis=-1)
        return jnp.where(keep, x, d)

    def set_dep(val):
        dep[0] = val

    @pl.when(s == 0)
    def _():
        for ref in (qkv_scr, z_scr, ba_scr, xr_scr, gate_scr, dtail_scr, ltail_scr, s_scr, h_scr):
            ref[...] = jnp.zeros(ref.shape, F32)

    def stream_a():
        x = x_ref[...]
        h = _rms(x, n1_ref[...]).astype(BF16)
        pieces = _ffn_pieces(h, wg_ref, wu_ref, wd_ref)
        for _ in range(D_FF // FF_PIECE):
            yield next(pieces)[0:1, 0:DK]
        acc = next(pieces)
        x1 = x + 0.5 * acc
        x1_ref[...] = x1
        hm = _rms(x1, nm_ref[...]).astype(BF16)
        yield acc[0:1, 0:DK]
        dsts = ((qkv_scr, TAIL, 0, D_QKV), (z_scr, 0, D_QKV, D_A), (xr_scr, TAIL, D_QKV + D_A, D_R),
                (gate_scr, 0, D_QKV + D_A + D_R, D_R), (ba_scr, 0, D_QKV + D_A + 2 * D_R, D_BA))
        for dst, row0, col0, width in dsts:
            for d0 in range(0, width, PROJ_PIECE):
                w = min(PROJ_PIECE, width - d0)
                p = jnp.dot(hm, win_ref[:, col0 + d0:col0 + d0 + w], preferred_element_type=F32)
                dst[slot_a, row0:row0 + tt, d0:d0 + w] = p
                yield p[0:1, 0:DK]

    qkv_scr[slot_b, tails, :] = jnp.where(first, dcs_ref[0], dtail_scr[tails, :])
    xr_scr[slot_b, tails, :] = jnp.where(first, lcs_ref[0], ltail_scr[tails, :])

    def put_o(tile, h, val):
        o_ref[tile[2]:tile[2] + CHUNK, h * DV:(h + 1) * DV] = val.astype(BF16)

    def put_state(b, h, val):
        s_scr[h] = val
        snew_ref[0, h] = val

    def put_y(val):
        y_ref[...] = val.reshape(tt, D_R).astype(BF16)

    def put_h(val):
        h_scr[...] = val
        hnew_ref[...] = val

    dio = _Io(ext=lambda tile, col0, w: qkv_scr[slot_b, tile[2]:tile[2] + TAIL + CHUNK, col0:col0 + w][None],
              ba=lambda tile: ba_scr[slot_b, tile[2]:tile[2] + CHUNK, :],
              z=lambda tile, col0: z_scr[slot_b, tile[2]:tile[2] + CHUNK, col0:col0 + DV],
              put_o=put_o, get_state=lambda b, h: jnp.where(first, ds0_ref[0, h], s_scr[h]),
              put_state=put_state, after=after)
    lio = _Io(ext=lambda: xr_scr[slot_b][None],
              gate=lambda: gate_scr[slot_b], h0=lambda: jnp.where(first, lh0_ref[...], h_scr[...]),
              put_y=put_y, put_h=put_h, after=after)
    tiles = tuple((0, 1, j * CHUNK) for j in range(tt // CHUNK))
    _interleave({"a": stream_a(),
                 "d": _delta_stream(dio, cw_ref, alog_ref, dtb_ref, on_ref, tiles=tiles, seg=CHUNK),
                 "l": _lru_stream(lio, lcw_ref, lcb_ref, lwg_ref, lbg_ref, lam_ref, tt=tt, bb=1)},
                MIX_PLAN, "a", set_dep)

    dtail_scr[...] = qkv_scr[slot_b, tt:tt + TAIL, :]
    ltail_scr[...] = xr_scr[slot_b, tt:tt + TAIL, :]
    dcsnew_ref[0] = qkv_scr[slot_b, tt + TAIL - (CONV_W - 1):tt + TAIL, :]
    lcsnew_ref[0] = xr_scr[slot_b, tt + TAIL - (CONV_W - 1):tt + TAIL, :]


def _in_mix(x, sdc, sd, slc, sl, lw, l, *, batch):
    r = x.shape[0]
    tt = MIX_TILE
    n_tiles = r // tt
    tiles_per_seq = n_tiles // batch
    assert n_tiles * tt == r and tiles_per_seq * batch == n_tiles
    cur = lambda s: (jnp.minimum(s, n_tiles - 1), 0)
    prev = lambda s: (jnp.maximum(s - 1, 0), 0)
    seq3 = lambda s: (jnp.maximum(s - 1, 0) // tiles_per_seq, 0, 0)
    seq4 = lambda s: (jnp.maximum(s - 1, 0) // tiles_per_seq, 0, 0, 0)
    layer = lambda shape: pl.BlockSpec((None,) + shape, lambda s: (l,) + (0,) * len(shape),
                                       pipeline_mode=pl.Buffered(1))
    return pl.pallas_call(
        functools.partial(_in_mix_body, tiles_per_seq=tiles_per_seq),
        grid=(n_tiles + 1,),
        in_specs=[pl.BlockSpec(memory_space=pltpu.SMEM),
                  pl.BlockSpec((tt, D_MODEL), cur), _resident((1, D_MODEL)),
                  _of_layer((D_MODEL, D_FF), l), _of_layer((D_MODEL, D_FF), l), _of_layer((D_FF, D_MODEL), l),
                  _resident((1, D_MODEL)), _of_layer((D_MODEL, D_PROJ), l),
                  layer((1, CONV_W - 1, D_QKV)), layer((1, H_A, DK, DV)),
                  _resident((CONV_W, D_QKV)), _resident((1, D_BA)), _resident((1, D_BA)), _resident((1, DV)),
                  layer((1, CONV_W - 1, D_R)), layer((1, 1, D_R)),
                  _resident((CONV_W, D_R)), _resident((1, D_R)), _resident((D_R, 2 * D_R)),
                  _resident((1, 2 * D_R)), _resident((1, D_R))],
        out_specs=[pl.BlockSpec((tt, D_MODEL), cur), pl.BlockSpec((tt, D_A), prev), pl.BlockSpec((tt, D_R), prev),
                   pl.BlockSpec((1, H_A, DK, DV), seq4), pl.BlockSpec((1, CONV_W - 1, D_QKV), seq3),
                   pl.BlockSpec((1, 1, D_R), seq3), pl.BlockSpec((1, CONV_W - 1, D_R), seq3)],
        out_shape=[jax.ShapeDtypeStruct((r, D_MODEL), F32), jax.ShapeDtypeStruct((r, D_A), BF16),
                   jax.ShapeDtypeStruct((r, D_R), BF16),
                   jax.ShapeDtypeStruct((batch, H_A, DK, DV), F32),
                   jax.ShapeDtypeStruct((batch, CONV_W - 1, D_QKV), F32),
                   jax.ShapeDtypeStruct((batch, 1, D_R), F32),
                   jax.ShapeDtypeStruct((batch, CONV_W - 1, D_R), F32)],
        scratch_shapes=[pltpu.VMEM((2, TAIL + tt, D_QKV), F32), pltpu.VMEM((2, tt, D_A), F32),
                        pltpu.VMEM((2, tt, D_BA), F32), pltpu.VMEM((2, TAIL + tt, D_R), F32),
                        pltpu.VMEM((2, tt, D_R), F32), pltpu.VMEM((TAIL, D_QKV), F32),
                        pltpu.VMEM((TAIL, D_R), F32), pltpu.VMEM((H_A, DK, DV), F32),
                        pltpu.VMEM((1, 1, D_R), F32)],
        compiler_params=pltpu.CompilerParams(dimension_semantics=("arbitrary",),
                                             vmem_limit_bytes=VMEM_LIMIT),
        name="in_mix",
    )(jnp.ones((1,), jnp.int32), x, lw["n1"], lw["f1g"], lw["f1u"], lw["f1d"], lw["nm"], lw["w_in"],
      sdc, sd, lw["cqkv"], lw["alog"], lw["dtb"], lw["onorm"],
      slc, sl, lw["clw"], lw["clb"], lw["wgate"], lw["bgate"], lw["lam"])


def _regroup_w_in(w_in):
    c2 = D_QKV + D_A
    c4 = c2 + 2 * H_A
    pad = jnp.zeros(w_in.shape[:2] + (D_BA - 2 * H_A,), w_in.dtype)
    return jnp.concatenate([w_in[..., :c2], w_in[..., c4:], w_in[..., c2:c4], pad], axis=-1).astype(BF16)


def _prep_layer(l, w, stacked):
    row = lambda v: v.reshape(1, -1).astype(F32)
    lane_pad = lambda v: jnp.zeros((1, D_BA), F32).at[0, H_A:2 * H_A].set(v)
    eye = jnp.eye(NB_R, dtype=F32)
    bd = lambda wb: jnp.einsum("ncd,nm->ncmd", wb, eye).reshape(D_R, D_R)
    return dict(
        l=l, n1=row(w["norm_ffn1"][l]), f1g=stacked["f1g"], f1u=stacked["f1u"], f1d=stacked["f1d"],
        nm=row(w["norm_mix"][l]), w_in=stacked["w_in"],
        cqkv=w["conv_qkv"][l], alog=lane_pad(w["a_log"][l]), dtb=lane_pad(w["dt_bias"][l]),
        onorm=row(w["norm_delta_out"][l]),
        clw=w["conv_lru_w"][l], clb=row(w["conv_lru_b"][l]),
        wgate=jnp.concatenate([bd(w["w_rgate"][l]), bd(w["w_igate"][l])], axis=1).astype(BF16),
        bgate=jnp.concatenate([row(w["b_rgate"][l]), row(w["b_igate"][l])], axis=1),
        lam=row(w["lru_lambda"][l]),
        w_out=stacked["w_out"], n2=row(w["norm_ffn2"][l]),
        f2g=stacked["f2g"], f2u=stacked["f2u"], f2d=stacked["f2d"],
    )


def _run_group(x, sd, sdc, sl, slc, layers, nf, *, seg=None, delta_tt=None, delta_bb=None, tiles=None,
               lru_tt=None, lru_bb=None, fused=False, states_only=False):
    b, t, _ = x.shape
    xf = x.reshape(b * t, D_MODEL)
    nd, ndc, nl, nlc = [], [], [], []
    s_all = None
    n_layers = len(layers)
    for l, lw in enumerate(layers):
        if fused:
            x1, o, y, s_new, cs_new, h_new, lcs_new = _in_mix(xf, sdc, sd, slc, sl, lw, l, batch=b)
        else:
            x1, qkv, z, xr, gate, ba = _ffn_in(xf, lw)
            r3 = lambda v: v.reshape(b, t, v.shape[-1])
            o, s_all, cs_new = _delta(r3(qkv), r3(z), r3(ba), sdc, sd, lw, l, s_all,
                                      seg=seg, tt=delta_tt, bb=delta_bb, tiles=tiles)
            s_new = None
            y, h_new, lcs_new = _lru(r3(xr), r3(gate), slc, sl, lw, l, tt=lru_tt, bb=lru_bb)
        last = l == n_layers - 1
        if not (last and states_only):
            xf = _out_ffn(x1, o.reshape(b * t, D_A), y.reshape(b * t, D_R), lw, nf, last)
        nd.append(s_new)
        ndc.append(cs_new)
        nl.append(h_new)
        nlc.append(lcs_new)
    stack = jnp.stack
    y_out = None if states_only else xf.reshape(b, t, D_MODEL)
    return y_out, (stack(nd) if fused else s_all), stack(ndc), stack(nl), stack(nlc)


def kernel(x_prompt, x_sample, state_delta, state_delta_conv, state_lru, state_lru_conv, meta_tokens, norm_ffn1, w_ffn1_gate, w_ffn1_up, w_ffn1_down, norm_mix, w_in, conv_qkv, a_log, dt_bias, norm_delta_out, conv_lru_w, conv_lru_b, w_rgate, b_rgate, w_igate, b_igate, lru_lambda, w_out, norm_ffn2, w_ffn2_gate, w_ffn2_up, w_ffn2_down, norm_final):
    w = dict(norm_ffn1=norm_ffn1, w_ffn1_gate=w_ffn1_gate, w_ffn1_up=w_ffn1_up, w_ffn1_down=w_ffn1_down,
             norm_mix=norm_mix, w_in=w_in, conv_qkv=conv_qkv, a_log=a_log, dt_bias=dt_bias,
             norm_delta_out=norm_delta_out, conv_lru_w=conv_lru_w, conv_lru_b=conv_lru_b,
             w_rgate=w_rgate, b_rgate=b_rgate, w_igate=w_igate, b_igate=b_igate, lru_lambda=lru_lambda,
             w_out=w_out, norm_ffn2=norm_ffn2, w_ffn2_gate=w_ffn2_gate, w_ffn2_up=w_ffn2_up,
             w_ffn2_down=w_ffn2_down)
    depth = norm_ffn1.shape[0]
    stacked = dict(f1g=w_ffn1_gate.astype(BF16), f1u=w_ffn1_up.astype(BF16), f1d=w_ffn1_down.astype(BF16),
                   f2g=w_ffn2_gate.astype(BF16), f2u=w_ffn2_up.astype(BF16), f2d=w_ffn2_down.astype(BF16),
                   w_out=w_out.astype(BF16), w_in=_regroup_w_in(w_in))
    layers = [_prep_layer(l, w, stacked) for l in range(depth)]
    nf = norm_final.reshape(1, D_MODEL).astype(F32)
    bp, seq, _ = x_prompt.shape
    bs, dseq, _ = x_sample.shape
    seg_b = CHUNK // dseq

    zeros = lambda *s: jnp.zeros((depth,) + s, F32)
    _, md, mdc, ml, mlc = _run_group(
        meta_tokens.astype(F32)[None], zeros(1, H_A, DK, DV), zeros(1, CONV_W - 1, D_QKV),
        zeros(1, 1, D_R), zeros(1, CONV_W - 1, D_R), layers, nf,
        seg=N_META, delta_tt=N_META, delta_bb=1, tiles=((0, 1, 0),), lru_tt=N_META, lru_bb=1,
        states_only=True)
    y_prompt, pd, pdc, plr, plc = _run_group(x_prompt, md, mdc, ml, mlc, layers, nf, fused=True)
    y_sample, sdn, sdcn, sln, slcn = _run_group(
        x_sample, state_delta, state_delta_conv, state_lru.reshape(depth, bs, 1, D_R), state_lru_conv,
        layers, nf, seg=dseq, delta_tt=dseq, delta_bb=SAMPLE_TILES_PER_STEP * seg_b,
        tiles=tuple((j * seg_b, seg_b, 0) for j in range(SAMPLE_TILES_PER_STEP)),
        lru_tt=dseq, lru_bb=2 * seg_b)

    return (y_prompt, y_sample, pd, pdc, plr.reshape(depth, bp, D_R), plc,
            sdn, sdcn, sln.reshape(depth, bs, D_R), slcn)
```

```python
import functools
import math

import jax
import jax.numpy as jnp
from jax import lax
from jax.experimental import pallas as pl
from jax.experimental.pallas import tpu as pltpu

F32 = jnp.float32
BF16 = jnp.bfloat16

D_MODEL = 1024
D_FF = 2816
D_A = 512
D_R = 512
H_A = 4
DK = 128
DV = 128
CHUNK = 64
CONV_W = 4
NB_R = 8
C_RG = 8.0
EPS = 1e-6
N_META = 16
D_QKV = 3 * D_A
D_BA = 128
D_PROJ_A = D_QKV + D_A
D_PROJ_B = 2 * D_R + D_BA

SUBLANES = 8
TAIL = SUBLANES
VMEM_BYTES_V7X = 64 * 1024 * 1024
VMEM_LIMIT = VMEM_BYTES_V7X - 8 * 1024 * 1024
ROW_TILE = 512
MIX_TILE = 256
FF_PIECE = 256
PROJ_PIECE = 256
MIX_PLAN = "ad" * 7 + ("adl" + "ad") * 3 + "adl" + "ad" * 8
SAMPLE_TILES_PER_STEP = 2


def _rms(x, w):
    ms = jnp.mean(x * x, axis=-1, keepdims=True)
    return x * lax.rsqrt(ms + EPS) * w


def _sigmoid(x):
    return 1.0 / (1.0 + jnp.exp(-x))


def _silu(x):
    return x * _sigmoid(x)


def _softplus(x):
    t = jnp.exp(-jnp.abs(x))
    u = 1.0 + t
    d = u - 1.0
    log1p_t = jnp.where(d == 0.0, t, jnp.log(u) * (t / jnp.where(d == 0.0, 1.0, d)))
    return jnp.maximum(x, 0.0) + log1p_t


def _dot(a, b):
    return jnp.dot(a.astype(BF16), b.astype(BF16), preferred_element_type=F32)


def _dot_nt(a, b):
    return lax.dot_general(a.astype(BF16), b.astype(BF16), (((1,), (1,)), ((), ())),
                           preferred_element_type=F32)


def _dot_tn(a, b):
    return lax.dot_general(a.astype(BF16), b.astype(BF16), (((0,), (0,)), ((), ())),
                           preferred_element_type=F32)


def _split3(a):
    a1 = a.astype(BF16)
    r = a - a1.astype(F32)
    a2 = r.astype(BF16)
    a3 = (r - a2.astype(F32)).astype(BF16)
    return a1, a2, a3


def _ffn_pieces(h, wg_ref, wu_ref, wd_ref):
    def down(gu, acc):
        g, u, c0 = gu
        a = (_silu(g) * u).astype(BF16)
        d = jnp.dot(a, wd_ref[c0:c0 + FF_PIECE, :], preferred_element_type=F32)
        return d if acc is None else acc + d

    acc, pending = None, None
    for c0 in range(0, D_FF, FF_PIECE):
        g = jnp.dot(h, wg_ref[:, c0:c0 + FF_PIECE], preferred_element_type=F32)
        u = jnp.dot(h, wu_ref[:, c0:c0 + FF_PIECE], preferred_element_type=F32)
        if pending is not None:
            acc = down(pending, acc)
        pending = (g, u, c0)
        yield g
    yield down(pending, acc)


def _ffn(x, nw, wg_ref, wu_ref, wd_ref):
    h = _rms(x, nw).astype(BF16)
    for acc in _ffn_pieces(h, wg_ref, wu_ref, wd_ref):
        pass
    return x + 0.5 * acc


def _ffn_in_body(x_ref, n1_ref, wg_ref, wu_ref, wd_ref, nm_ref, wina_ref, winb_ref,
                 x1_ref, qkv_ref, z_ref, xr_ref, gate_ref, ba_ref):
    x1 = _ffn(x_ref[...], n1_ref[...], wg_ref, wu_ref, wd_ref)
    x1_ref[...] = x1
    hm = _rms(x1, nm_ref[...]).astype(BF16)
    pa = jnp.dot(hm, wina_ref[...], preferred_element_type=F32)
    pb = jnp.dot(hm, winb_ref[...], preferred_element_type=F32)
    qkv_ref[...] = pa[:, :D_QKV]
    z_ref[...] = pa[:, D_QKV:]
    xr_ref[...] = pb[:, :D_R]
    gate_ref[...] = pb[:, D_R:2 * D_R]
    ba_ref[...] = pb[:, 2 * D_R:]


def _out_ffn_body(x_ref, o_ref, y_ref, wo1_ref, wo2_ref, n2_ref, wg_ref, wu_ref, wd_ref, nf_ref,
                  out_ref, *, final):
    x = x_ref[...] + (_dot(o_ref[...], wo1_ref[...]) + _dot(y_ref[...], wo2_ref[...]))
    x2 = _ffn(x, n2_ref[...], wg_ref, wu_ref, wd_ref)
    if final:
        x2 = _rms(x2, nf_ref[...])
    out_ref[...] = x2


def _resident(shape):
    nd = len(shape)
    return pl.BlockSpec(shape, lambda *_: (0,) * nd, pipeline_mode=pl.Buffered(1))


def _of_layer(shape, l, first=0):
    nd = len(shape)
    return pl.BlockSpec((None,) + shape, lambda *_: (l, first) + (0,) * (nd - 1), pipeline_mode=pl.Buffered(1))


def _rows(tm, width):
    return pl.BlockSpec((tm, width), lambda i: (i, 0))


def _ffn_in(x, lw):
    r = x.shape[0]
    l = lw["l"]
    tm = min(ROW_TILE, r)
    widths = (D_MODEL, D_QKV, D_A, D_R, D_R, D_BA)
    return pl.pallas_call(
        _ffn_in_body,
        grid=(pl.cdiv(r, tm),),
        in_specs=[_rows(tm, D_MODEL), _resident((1, D_MODEL)),
                  _of_layer((D_MODEL, D_FF), l), _of_layer((D_MODEL, D_FF), l), _of_layer((D_FF, D_MODEL), l),
                  _resident((1, D_MODEL)), _of_layer((D_MODEL, D_PROJ_A), l), _of_layer((D_MODEL, D_PROJ_B), l)],
        out_specs=[_rows(tm, w) for w in widths],
        out_shape=[jax.ShapeDtypeStruct((r, w), F32) for w in widths],
        compiler_params=pltpu.CompilerParams(dimension_semantics=("parallel",),
                                             vmem_limit_bytes=VMEM_LIMIT),
        name="ffn_in",
    )(x, lw["n1"], lw["f1g"], lw["f1u"], lw["f1d"], lw["nm"], lw["w_in_a"], lw["w_in_b"])


def _out_ffn(x, o, y, lw, nf, final):
    r = x.shape[0]
    l = lw["l"]
    tm = min(ROW_TILE, r)
    return pl.pallas_call(
        functools.partial(_out_ffn_body, final=final),
        grid=(pl.cdiv(r, tm),),
        in_specs=[_rows(tm, D_MODEL), _rows(tm, D_A), _rows(tm, D_R),
                  _of_layer((D_A, D_MODEL), l, 0), _of_layer((D_R, D_MODEL), l, 1), _resident((1, D_MODEL)),
                  _of_layer((D_MODEL, D_FF), l), _of_layer((D_MODEL, D_FF), l), _of_layer((D_FF, D_MODEL), l),
                  _resident((1, D_MODEL))],
        out_specs=_rows(tm, D_MODEL),
        out_shape=jax.ShapeDtypeStruct((r, D_MODEL), F32),
        compiler_params=pltpu.CompilerParams(dimension_semantics=("parallel",),
                                             vmem_limit_bytes=VMEM_LIMIT),
        name="out_ffn",
    )(x, o, y, lw["w_out"], lw["w_out"], lw["n2"], lw["f2g"], lw["f2u"], lw["f2d"], nf)


class _Io:
    def __init__(self, **fns):
        self.after = lambda x: x
        self.__dict__.update(fns)


def _delta_stream(io, cw_ref, alog_ref, dtb_ref, on_ref, *, tiles, seg):
    c = seg
    nb = tiles[0][1]
    r = nb * c
    levels = int(math.log2(c))
    assert 2 ** levels == c
    row = lax.broadcasted_iota(jnp.int32, (r, r), 0)
    col = lax.broadcasted_iota(jnp.int32, (r, r), 1)
    same = lax.shift_right_logical(row, levels) == lax.shift_right_logical(col, levels)
    causal = same & (row >= col)
    strict = same & (row > col)
    tri = causal.astype(BF16)
    ones_seg = same.astype(BF16)
    eye = (row == col).astype(F32)
    sel_r = lax.broadcasted_iota(jnp.int32, (SUBLANES, D_BA), 0)
    sel_c = lax.broadcasted_iota(jnp.int32, (SUBLANES, D_BA), 1)
    sel = ((sel_c == sel_r + H_A) & (sel_r < H_A)).astype(BF16)

    def conv_silu(tile, col0):
        win = io.ext(tile, col0, DK)
        acc = None
        for j in range(CONV_W):
            k = CONV_W - 1 - j
            x = win if k == 0 else pltpu.roll(win, k, 1)
            term = x[:, TAIL:, :].reshape(r, DK) * io.after(cw_ref[j:j + 1, col0:col0 + DK])
            acc = term if acc is None else acc + term
        return _silu(acc)

    def exact_dot(lhs01, x, nt=False):
        out = None
        for p in _split3(x):
            if nt:
                d = lax.dot_general(lhs01, p, (((1,), (1,)), ((), ())), preferred_element_type=F32)
            else:
                d = jnp.dot(lhs01, p, preferred_element_type=F32)
            out = d if out is None else out + d
        return out

    items = [(j, h) for j in range(len(tiles)) for h in range(H_A)]

    gcum, gtot, grow, eg_all, beta_all = [], [], [], [], []
    for tile in tiles:
        ba = io.ba(tile)
        g_all = -jnp.exp(io.after(alog_ref[...])) * _softplus(ba + dtb_ref[...])
        beta_all.append(_sigmoid(ba))
        gcum.append(exact_dot(tri, g_all))
        gtot.append(exact_dot(ones_seg, g_all) if nb > 1 else gcum[-1][r - 1:r, :])
    yield
    for j in range(len(tiles)):
        grow.append(exact_dot(sel, gcum[j], nt=True))
        eg_all.append(jnp.exp(gcum[j]))
    yield

    qg, kn, kb, kdec, decay, rhs = {}, {}, {}, {}, {}, {}
    for (j, h) in items:
        tile = tiles[j]
        qh = conv_silu(tile, h * DK)
        kh = conv_silu(tile, D_A + h * DK)
        vh = conv_silu(tile, 2 * D_A + h * DV)
        qn = qh * lax.rsqrt(jnp.sum(qh * qh, axis=-1, keepdims=True) + EPS) * (DK ** -0.5)
        k_n = kh * lax.rsqrt(jnp.sum(kh * kh, axis=-1, keepdims=True) + EPS)
        beta = beta_all[j][:, h:h + 1]
        gcol = gcum[j][:, H_A + h:H_A + h + 1]
        egcol = eg_all[j][:, H_A + h:H_A + h + 1]
        glast = gtot[j][:, H_A + h:H_A + h + 1]
        dmat = gcol - grow[j][h:h + 1, :]
        decay[j, h] = jnp.where(causal, jnp.exp(jnp.where(causal, dmat, 0.0)), 0.0)
        kn[j, h] = k_n
        kb[j, h] = k_n * beta
        qg[j, h] = (qn, qn * egcol)
        kdec[j, h] = k_n * jnp.exp(glast - gcol)
        rhs[j, h] = jnp.concatenate([vh * beta, kb[j, h] * egcol], axis=-1)
        if h == H_A - 1:
            yield

    kq = {it: _dot_nt(jnp.concatenate([kb[it], qg[it][0]], axis=0), kn[it]) for it in items}
    lmat = {it: jnp.where(strict, kq[it][:r] * decay[it], 0.0) for it in items}
    attn = {it: kq[it][r:] * decay[it] for it in items}
    yield

    tinv = {it: eye - lmat[it] for it in items}
    m = {it: _dot(lmat[it], lmat[it]) for it in items}
    yield
    for _ in range(levels - 2):
        tinv = {it: tinv[it] + _dot(tinv[it], m[it]) for it in items}
        m = {it: _dot(m[it], m[it]) for it in items}
        yield
    tinv = {it: tinv[it] + _dot(tinv[it], m[it]) for it in items}
    yield
    sol = {it: _dot(tinv[it], rhs[it]) for it in items}
    yield

    state = {}

    def get_state(b, h):
        if (b, h) not in state:
            state[b, h] = io.get_state(b, h)
        return state[b, h]

    for j, tile in enumerate(tiles):
        b0 = tile[0]
        segs = [(s, slice(s * c, (s + 1) * c)) for s in range(nb)]
        ws = {}
        for h in range(H_A):
            for s, rs in segs:
                lhs = jnp.concatenate([sol[j, h][rs, DV:], qg[j, h][1][rs]], axis=0)
                ws[h, s] = _dot(lhs, get_state(b0 + s, h))
        yield
        v_new, av = {}, {}
        for h in range(H_A):
            w_s = jnp.concatenate([ws[h, s][:c] for s, _ in segs], axis=0) if nb > 1 else ws[h, 0][:c]
            v_new[h] = sol[j, h][:, :DV] - w_s
        for h in range(H_A):
            av[h] = _dot(attn[j, h], v_new[h])
        for h in range(H_A):
            for s, rs in segs:
                g0 = rs.start if nb > 1 else 0
                scale = jnp.exp(gtot[j][g0:g0 + 1, H_A + h:H_A + h + 1])
                state[b0 + s, h] = state[b0 + s, h] * scale + _dot_tn(kdec[j, h][rs], v_new[h][rs])
        for h in range(H_A):
            q_s = jnp.concatenate([ws[h, s][c:] for s, _ in segs], axis=0) if nb > 1 else ws[h, 0][c:]
            on = _rms(q_s + av[h], io.after(on_ref[...]))
            io.put_o(tile, h, on * _silu(io.z(tile, h * DV)))
        if j == len(tiles) - 1:
            for (b, h), val in state.items():
                io.put_state(b, h, val)
        yield


def _delta_body(qkv_ref, z_ref, ba_ref, cs_ref, s0_ref, cw_ref, alog_ref, dtb_ref, on_ref, *rest,
                tiles, seg, tt, n_t, fill_layers):
    if fill_layers:
        o_ref, sall_ref, csnew_ref, ext_ref = rest
        snew_ref = sall_ref.at[0]
    else:
        _, o_ref, snew_ref, csnew_ref, ext_ref = rest
    c = seg
    nb = tiles[0][1]
    r = nb * c
    t = pl.program_id(1)

    @pl.when(t == 0)
    def _():
        ext_ref[:, TAIL - (CONV_W - 1):TAIL, :] = cs_ref[...]
        snew_ref[...] = s0_ref[...]

    ext_ref[:, TAIL:TAIL + tt, :] = qkv_ref[...]

    def rows(ref, tile, off, col0, width):
        b0, _, t0 = tile
        return ref[b0:b0 + nb, off + t0:off + t0 + c, col0:col0 + width].reshape(r, width)

    def put_o(tile, h, val):
        b0, _, t0 = tile
        o_ref[b0:b0 + nb, t0:t0 + c, h * DV:(h + 1) * DV] = val.reshape(nb, c, DV).astype(BF16)

    def put_state(b, h, val):
        snew_ref[b, h] = val

    def ext(tile, col0, w):
        b0, _, t0 = tile
        return ext_ref[b0:b0 + nb, t0:t0 + TAIL + c, col0:col0 + w]

    io = _Io(ext=ext, ba=lambda tile: rows(ba_ref, tile, 0, 0, D_BA),
             z=lambda tile, col0: rows(z_ref, tile, 0, col0, DV),
             put_o=put_o, get_state=lambda b, h: snew_ref[b, h], put_state=put_state)
    for _ in _delta_stream(io, cw_ref, alog_ref, dtb_ref, on_ref, tiles=tiles, seg=seg):
        pass

    tail = ext_ref[:, tt:tt + TAIL, :]
    ext_ref[:, 0:TAIL, :] = tail

    @pl.when(t == n_t - 1)
    def _():
        csnew_ref[...] = ext_ref[:, TAIL - (CONV_W - 1):TAIL, :]
        if fill_layers:
            for d in range(1, fill_layers):
                sall_ref[d] = sall_ref[0]


def _delta(qkv, z, ba, cs, s0, lw, l, s_all, *, seg, tt, bb, tiles):
    b, t, _ = qkv.shape
    depth = s0.shape[0]
    n_t = t // tt
    assert n_t * tt == t and b % bb == 0 and (l == 0) == (s_all is None)
    st = lambda i, j: (l, i, 0, 0, 0)
    ct = lambda i, j: (l, i, 0, 0)
    tok = lambda w: pl.BlockSpec((bb, tt, w), lambda i, j: (i, j, 0))
    const = lambda shape: pl.BlockSpec(shape, lambda i, j: (0,) * len(shape))
    in_specs = [tok(D_QKV), tok(D_A), tok(D_BA),
                pl.BlockSpec((None, bb, CONV_W - 1, D_QKV), ct),
                pl.BlockSpec((None, bb, H_A, DK, DV), st),
                const((CONV_W, D_QKV)), const((1, D_BA)), const((1, D_BA)), const((1, DV))]
    operands = [qkv, z, ba, cs, s0, lw["cqkv"], lw["alog"], lw["dtb"], lw["onorm"]]
    if s_all is None:
        s_spec = pl.BlockSpec((depth, bb, H_A, DK, DV), lambda i, j: (0, i, 0, 0, 0))
        aliases = {}
    else:
        s_spec = pl.BlockSpec((None, bb, H_A, DK, DV), st)
        in_specs.append(pl.BlockSpec(memory_space=pl.ANY))
        operands.append(s_all)
        aliases = {len(operands) - 1: 1}
    return pl.pallas_call(
        functools.partial(_delta_body, tiles=tiles, seg=seg, tt=tt, n_t=n_t,
                          fill_layers=depth if s_all is None else 0),
        grid=(b // bb, n_t),
        in_specs=in_specs,
        out_specs=[tok(D_A), s_spec,
                   pl.BlockSpec((bb, CONV_W - 1, D_QKV), lambda i, j: (i, 0, 0))],
        out_shape=[jax.ShapeDtypeStruct((b, t, D_A), BF16),
                   jax.ShapeDtypeStruct((depth, b, H_A, DK, DV), F32),
                   jax.ShapeDtypeStruct((b, CONV_W - 1, D_QKV), F32)],
        scratch_shapes=[pltpu.VMEM((bb, tt + TAIL, D_QKV), F32)],
        input_output_aliases=aliases,
        compiler_params=pltpu.CompilerParams(dimension_semantics=("arbitrary", "arbitrary"),
                                             vmem_limit_bytes=VMEM_LIMIT),
        name="delta",
    )(*operands)


def _lru_stream(io, cw_ref, cb_ref, wg_ref, bg_ref, lam_ref, *, tt, bb):
    win = io.ext()
    xc = None
    for j in range(CONV_W):
        k = CONV_W - 1 - j
        x = win if k == 0 else pltpu.roll(win, k, 1)
        term = x[:, TAIL:, :] * io.after(cw_ref[j:j + 1, :])
        xc = term if xc is None else xc + term
    xc = xc + cb_ref[...]
    n = bb * tt
    xc = xc.reshape(n, D_R)
    yield
    gates = jnp.dot(xc.astype(BF16), wg_ref[...], preferred_element_type=F32) + bg_ref[...]
    r = _sigmoid(gates[:, :D_R])
    i = _sigmoid(gates[:, D_R:])
    log_a = -C_RG * r * _softplus(-io.after(lam_ref[...]))
    a = jnp.exp(log_a)
    m2 = -jnp.tanh(log_a) * (a * a + 1.0)
    u = jnp.where(m2 > 0.0, m2 * lax.rsqrt(m2), 0.0) * i * xc
    yield

    ng = n // SUBLANES
    a = a.reshape(ng, SUBLANES, D_R)
    u = u.reshape(ng, SUBLANES, D_R)
    tpos = lax.broadcasted_iota(jnp.int32, (ng, SUBLANES, D_R), 1)
    s = 1
    while s < SUBLANES:
        a_sh = pltpu.roll(a, s, 1)
        u_sh = pltpu.roll(u, s, 1)
        m = tpos >= s
        u = jnp.where(m, u + a * u_sh, u)
        a = jnp.where(m, a * a_sh, a)
        s *= 2
    yield
    gpb = tt // SUBLANES
    if gpb == 1:
        hs = u + a * io.h0()
        h_last = hs[:, SUBLANES - 1:SUBLANES, :]
    else:
        assert bb == 1
        h_in = io.h0()[0]
        groups = []
        for g in range(gpb):
            hg = u[g] + a[g] * h_in
            groups.append(hg)
            h_in = hg[SUBLANES - 1:SUBLANES, :]
        hs = jnp.concatenate(groups, axis=0)
        h_last = h_in.reshape(1, 1, D_R)
    y = hs.reshape(n, D_R) * jax.nn.gelu(io.gate(), approximate=True)
    io.put_y(y.reshape(bb, tt, D_R))
    io.put_h(h_last)
    yield


def _lru_body(xr_ref, gate_ref, cs_ref, h0_ref, cw_ref, cb_ref, wg_ref, bg_ref, lam_ref,
              y_ref, hnew_ref, csnew_ref, ext_ref, *, tt, bb, n_t):
    t = pl.program_id(1)

    @pl.when(t == 0)
    def _():
        ext_ref[:, TAIL - (CONV_W - 1):TAIL, :] = cs_ref[...]
        hnew_ref[...] = h0_ref[...]

    ext_ref[:, TAIL:TAIL + tt, :] = xr_ref[...]

    def put_y(val):
        y_ref[...] = val.astype(BF16)

    def put_h(val):
        hnew_ref[...] = val

    io = _Io(ext=lambda: ext_ref[...], gate=lambda: gate_ref[...].reshape(bb * tt, D_R),
             h0=lambda: hnew_ref[...], put_y=put_y, put_h=put_h)
    for _ in _lru_stream(io, cw_ref, cb_ref, wg_ref, bg_ref, lam_ref, tt=tt, bb=bb):
        pass

    tail = ext_ref[:, tt:tt + TAIL, :]
    ext_ref[:, 0:TAIL, :] = tail

    @pl.when(t == n_t - 1)
    def _():
        csnew_ref[...] = ext_ref[:, TAIL - (CONV_W - 1):TAIL, :]


def _lru(xr, gate, cs, h0, lw, l, *, tt, bb):
    b, t, _ = xr.shape
    n_t = t // tt
    assert n_t * tt == t and b % bb == 0 and tt % SUBLANES == 0
    ct = lambda i, j: (l, i, 0, 0)
    tok = pl.BlockSpec((bb, tt, D_R), lambda i, j: (i, j, 0))
    const = lambda shape: pl.BlockSpec(shape, lambda i, j: (0,) * len(shape))
    return pl.pallas_call(
        functools.partial(_lru_body, tt=tt, bb=bb, n_t=n_t),
        grid=(b // bb, n_t),
        in_specs=[tok, tok,
                  pl.BlockSpec((None, bb, CONV_W - 1, D_R), ct),
                  pl.BlockSpec((None, bb, 1, D_R), ct),
                  const((CONV_W, D_R)), const((1, D_R)), const((D_R, 2 * D_R)), const((1, 2 * D_R)),
                  const((1, D_R))],
        out_specs=[tok,
                   pl.BlockSpec((bb, 1, D_R), lambda i, j: (i, 0, 0)),
                   pl.BlockSpec((bb, CONV_W - 1, D_R), lambda i, j: (i, 0, 0))],
        out_shape=[jax.ShapeDtypeStruct((b, t, D_R), BF16),
                   jax.ShapeDtypeStruct((b, 1, D_R), F32),
                   jax.ShapeDtypeStruct((b, CONV_W - 1, D_R), F32)],
        scratch_shapes=[pltpu.VMEM((bb, tt + TAIL, D_R), F32)],
        compiler_params=pltpu.CompilerParams(dimension_semantics=("arbitrary", "arbitrary"),
                                             vmem_limit_bytes=VMEM_LIMIT),
        name="lru",
    )(xr, gate, cs, h0, lw["clw"], lw["clb"], lw["wgate"], lw["bgate"], lw["lam"])


def _interleave(streams, plan, lead, set_dep):
    live = dict(streams)

    def step(key):
        if key not in live:
            return
        try:
            val = next(live[key])
            if key == lead:
                set_dep(val)
        except StopIteration:
            del live[key]

    for key in plan:
        step(key)
    while live:
        for key in list(live):
            step(key)


def _in_mix_body(flag_ref, x_ref, n1_ref, wg_ref, wu_ref, wd_ref, nm_ref, wina_ref, winb_ref,
                 dcs_ref, ds0_ref, cw_ref, alog_ref, dtb_ref, on_ref,
                 lcs_ref, lh0_ref, lcw_ref, lcb_ref, lwg_ref, lbg_ref, lam_ref,
                 x1_ref, o_ref, y_ref, snew_ref, dcsnew_ref, hnew_ref, lcsnew_ref,
                 qkv_scr, z_scr, ba_scr, xr_scr, gate_scr, dtail_scr, ltail_scr, s_scr, h_scr,
                 *, tiles_per_seq):
    tt = MIX_TILE
    s = pl.program_id(0)
    slot_a = lax.rem(s, 2)
    slot_b = 1 - slot_a
    first = lax.rem(s + tiles_per_seq - 1, tiles_per_seq) == 0
    tails = slice(TAIL - (CONV_W - 1), TAIL)
    keep = flag_ref[0] == 1
    dep = [None]

    def after(x):
        if dep[0] is None:
            return x
        reps = x.shape[-1] // DK
        d = dep[0] if reps == 1 else jnp.concatenate([dep[0]] * reps, axis=-1)
        return jnp.where(keep, x, d)

    def set_dep(val):
        dep[0] = val

    @pl.when(s == 0)
    def _():
        for ref in (qkv_scr, z_scr, ba_scr, xr_scr, gate_scr, dtail_scr, ltail_scr, s_scr, h_scr):
            ref[...] = jnp.zeros(ref.shape, F32)

    def stream_a():
        x = x_ref[...]
        h = _rms(x, n1_ref[...]).astype(BF16)
        pieces = _ffn_pieces(h, wg_ref, wu_ref, wd_ref)
        for _ in range(D_FF // FF_PIECE):
            yield next(pieces)[0:1, 0:DK]
        acc = next(pieces)
        x1 = x + 0.5 * acc
        x1_ref[...] = x1
        hm = _rms(x1, nm_ref[...]).astype(BF16)
        yield acc[0:1, 0:DK]
        dsts = ((qkv_scr, TAIL, wina_ref, 0, D_QKV), (z_scr, 0, wina_ref, D_QKV, D_A),
                (xr_scr, TAIL, winb_ref, 0, D_R), (gate_scr, 0, winb_ref, D_R, D_R),
                (ba_scr, 0, winb_ref, 2 * D_R, D_BA))
        for dst, row0, w_ref, col0, width in dsts:
            for d0 in range(0, width, PROJ_PIECE):
                w = min(PROJ_PIECE, width - d0)
                p = jnp.dot(hm, w_ref[:, col0 + d0:col0 + d0 + w], preferred_element_type=F32)
                dst[slot_a, row0:row0 + tt, d0:d0 + w] = p
                yield p[0:1, 0:DK]

    qkv_scr[slot_b, tails, :] = jnp.where(first, dcs_ref[0], dtail_scr[tails, :])
    xr_scr[slot_b, tails, :] = jnp.where(first, lcs_ref[0], ltail_scr[tails, :])

    def put_o(tile, h, val):
        o_ref[tile[2]:tile[2] + CHUNK, h * DV:(h + 1) * DV] = val.astype(BF16)

    def put_state(b, h, val):
        s_scr[h] = val
        snew_ref[0, h] = val

    def put_y(val):
        y_ref[...] = val.reshape(tt, D_R).astype(BF16)

    def put_h(val):
        h_scr[...] = val
        hnew_ref[...] = val

    dio = _Io(ext=lambda tile, col0, w: qkv_scr[slot_b, tile[2]:tile[2] + TAIL + CHUNK, col0:col0 + w][None],
              ba=lambda tile: ba_scr[slot_b, tile[2]:tile[2] + CHUNK, :],
              z=lambda tile, col0: z_scr[slot_b, tile[2]:tile[2] + CHUNK, col0:col0 + DV],
              put_o=put_o, get_state=lambda b, h: jnp.where(first, ds0_ref[0, h], s_scr[h]),
              put_state=put_state, after=after)
    lio = _Io(ext=lambda: xr_scr[slot_b][None],
              gate=lambda: gate_scr[slot_b], h0=lambda: jnp.where(first, lh0_ref[...], h_scr[...]),
              put_y=put_y, put_h=put_h, after=after)
    tiles = tuple((0, 1, j * CHUNK) for j in range(tt // CHUNK))
    _interleave({"a": stream_a(),
                 "d": _delta_stream(dio, cw_ref, alog_ref, dtb_ref, on_ref, tiles=tiles, seg=CHUNK),
                 "l": _lru_stream(lio, lcw_ref, lcb_ref, lwg_ref, lbg_ref, lam_ref, tt=tt, bb=1)},
                MIX_PLAN, "a", set_dep)

    dtail_scr[...] = qkv_scr[slot_b, tt:tt + TAIL, :]
    ltail_scr[...] = xr_scr[slot_b, tt:tt + TAIL, :]
    dcsnew_ref[0] = qkv_scr[slot_b, tt + TAIL - (CONV_W - 1):tt + TAIL, :]
    lcsnew_ref[0] = xr_scr[slot_b, tt + TAIL - (CONV_W - 1):tt + TAIL, :]


def _in_mix(x, sdc, sd, slc, sl, lw, l, *, batch):
    r = x.shape[0]
    tt = MIX_TILE
    n_tiles = r // tt
    tiles_per_seq = n_tiles // batch
    assert n_tiles * tt == r and tiles_per_seq * batch == n_tiles
    cur = lambda s: (jnp.minimum(s, n_tiles - 1), 0)
    prev = lambda s: (jnp.maximum(s - 1, 0), 0)
    seq3 = lambda s: (jnp.maximum(s - 1, 0) // tiles_per_seq, 0, 0)
    seq4 = lambda s: (jnp.maximum(s - 1, 0) // tiles_per_seq, 0, 0, 0)
    layer = lambda shape: pl.BlockSpec((None,) + shape, lambda s: (l,) + (0,) * len(shape),
                                       pipeline_mode=pl.Buffered(1))
    return pl.pallas_call(
        functools.partial(_in_mix_body, tiles_per_seq=tiles_per_seq),
        grid=(n_tiles + 1,),
        in_specs=[pl.BlockSpec(memory_space=pltpu.SMEM),
                  pl.BlockSpec((tt, D_MODEL), cur), _resident((1, D_MODEL)),
                  _of_layer((D_MODEL, D_FF), l), _of_layer((D_MODEL, D_FF), l), _of_layer((D_FF, D_MODEL), l),
                  _resident((1, D_MODEL)), _of_layer((D_MODEL, D_PROJ_A), l), _of_layer((D_MODEL, D_PROJ_B), l),
                  layer((1, CONV_W - 1, D_QKV)), layer((1, H_A, DK, DV)),
                  _resident((CONV_W, D_QKV)), _resident((1, D_BA)), _resident((1, D_BA)), _resident((1, DV)),
                  layer((1, CONV_W - 1, D_R)), layer((1, 1, D_R)),
                  _resident((CONV_W, D_R)), _resident((1, D_R)), _resident((D_R, 2 * D_R)),
                  _resident((1, 2 * D_R)), _resident((1, D_R))],
        out_specs=[pl.BlockSpec((tt, D_MODEL), cur), pl.BlockSpec((tt, D_A), prev), pl.BlockSpec((tt, D_R), prev),
                   pl.BlockSpec((1, H_A, DK, DV), seq4), pl.BlockSpec((1, CONV_W - 1, D_QKV), seq3),
                   pl.BlockSpec((1, 1, D_R), seq3), pl.BlockSpec((1, CONV_W - 1, D_R), seq3)],
        out_shape=[jax.ShapeDtypeStruct((r, D_MODEL), F32), jax.ShapeDtypeStruct((r, D_A), BF16),
                   jax.ShapeDtypeStruct((r, D_R), BF16),
                   jax.ShapeDtypeStruct((batch, H_A, DK, DV), F32),
                   jax.ShapeDtypeStruct((batch, CONV_W - 1, D_QKV), F32),
                   jax.ShapeDtypeStruct((batch, 1, D_R), F32),
                   jax.ShapeDtypeStruct((batch, CONV_W - 1, D_R), F32)],
        scratch_shapes=[pltpu.VMEM((2, TAIL + tt, D_QKV), F32), pltpu.VMEM((2, tt, D_A), F32),
                        pltpu.VMEM((2, tt, D_BA), F32), pltpu.VMEM((2, TAIL + tt, D_R), F32),
                        pltpu.VMEM((2, tt, D_R), F32), pltpu.VMEM((TAIL, D_QKV), F32),
                        pltpu.VMEM((TAIL, D_R), F32), pltpu.VMEM((H_A, DK, DV), F32),
                        pltpu.VMEM((1, 1, D_R), F32)],
        compiler_params=pltpu.CompilerParams(dimension_semantics=("arbitrary",),
                                             vmem_limit_bytes=VMEM_LIMIT),
        name="in_mix",
    )(jnp.ones((1,), jnp.int32), x, lw["n1"], lw["f1g"], lw["f1u"], lw["f1d"], lw["nm"], lw["w_in_a"], lw["w_in_b"],
      sdc, sd, lw["cqkv"], lw["alog"], lw["dtb"], lw["onorm"],
      slc, sl, lw["clw"], lw["clb"], lw["wgate"], lw["bgate"], lw["lam"])


def _split_w_in(w_in):
    c2 = D_PROJ_A
    c4 = c2 + 2 * H_A
    pad = jnp.zeros(w_in.shape[:2] + (D_BA - 2 * H_A,), w_in.dtype)
    tail = jnp.concatenate([w_in[..., c4:], w_in[..., c2:c4], pad], axis=-1)
    return w_in[..., :c2].astype(BF16), tail.astype(BF16)


def _prep_layer(l, w, stacked):
    row = lambda v: v.reshape(1, -1).astype(F32)
    lane_pad = lambda v: jnp.zeros((1, D_BA), F32).at[0, H_A:2 * H_A].set(v)
    eye = jnp.eye(NB_R, dtype=F32)
    bd = lambda wb: jnp.einsum("ncd,nm->ncmd", wb, eye).reshape(D_R, D_R)
    return dict(
        l=l, n1=row(w["norm_ffn1"][l]), f1g=stacked["f1g"], f1u=stacked["f1u"], f1d=stacked["f1d"],
        nm=row(w["norm_mix"][l]), w_in_a=stacked["w_in_a"], w_in_b=stacked["w_in_b"],
        cqkv=w["conv_qkv"][l], alog=lane_pad(w["a_log"][l]), dtb=lane_pad(w["dt_bias"][l]),
        onorm=row(w["norm_delta_out"][l]),
        clw=w["conv_lru_w"][l], clb=row(w["conv_lru_b"][l]),
        wgate=jnp.concatenate([bd(w["w_rgate"][l]), bd(w["w_igate"][l])], axis=1).astype(BF16),
        bgate=jnp.concatenate([row(w["b_rgate"][l]), row(w["b_igate"][l])], axis=1),
        lam=row(w["lru_lambda"][l]),
        w_out=stacked["w_out"], n2=row(w["norm_ffn2"][l]),
        f2g=stacked["f2g"], f2u=stacked["f2u"], f2d=stacked["f2d"],
    )


def _run_group(x, sd, sdc, sl, slc, layers, nf, *, seg=None, delta_tt=None, delta_bb=None, tiles=None,
               lru_tt=None, lru_bb=None, fused=False, states_only=False):
    b, t, _ = x.shape
    xf = x.reshape(b * t, D_MODEL)
    nd, ndc, nl, nlc = [], [], [], []
    s_all = None
    n_layers = len(layers)
    for l, lw in enumerate(layers):
        if fused:
            x1, o, y, s_new, cs_new, h_new, lcs_new = _in_mix(xf, sdc, sd, slc, sl, lw, l, batch=b)
        else:
            x1, qkv, z, xr, gate, ba = _ffn_in(xf, lw)
            r3 = lambda v: v.reshape(b, t, v.shape[-1])
            o, s_all, cs_new = _delta(r3(qkv), r3(z), r3(ba), sdc, sd, lw, l, s_all,
                                      seg=seg, tt=delta_tt, bb=delta_bb, tiles=tiles)
            y, h_new, lcs_new = _lru(r3(xr), r3(gate), slc, sl, lw, l, tt=lru_tt, bb=lru_bb)
        last = l == n_layers - 1
        if not (last and states_only):
            xf = _out_ffn(x1, o.reshape(b * t, D_A), y.reshape(b * t, D_R), lw, nf, last)
        if fused:
            nd.append(s_new)
        ndc.append(cs_new)
        nl.append(h_new)
        nlc.append(lcs_new)
    stack = jnp.stack
    y_out = None if states_only else xf.reshape(b, t, D_MODEL)
    return y_out, (stack(nd) if fused else s_all), stack(ndc), stack(nl), stack(nlc)


def kernel(x_prompt, x_sample, state_delta, state_delta_conv, state_lru, state_lru_conv, meta_tokens, norm_ffn1, w_ffn1_gate, w_ffn1_up, w_ffn1_down, norm_mix, w_in, conv_qkv, a_log, dt_bias, norm_delta_out, conv_lru_w, conv_lru_b, w_rgate, b_rgate, w_igate, b_igate, lru_lambda, w_out, norm_ffn2, w_ffn2_gate, w_ffn2_up, w_ffn2_down, norm_final):
    w = dict(norm_ffn1=norm_ffn1, w_ffn1_gate=w_ffn1_gate, w_ffn1_up=w_ffn1_up, w_ffn1_down=w_ffn1_down,
             norm_mix=norm_mix, w_in=w_in, conv_qkv=conv_qkv, a_log=a_log, dt_bias=dt_bias,
             norm_delta_out=norm_delta_out, conv_lru_w=conv_lru_w, conv_lru_b=conv_lru_b,
             w_rgate=w_rgate, b_rgate=b_rgate, w_igate=w_igate, b_igate=b_igate, lru_lambda=lru_lambda,
             w_out=w_out, norm_ffn2=norm_ffn2, w_ffn2_gate=w_ffn2_gate, w_ffn2_up=w_ffn2_up,
             w_ffn2_down=w_ffn2_down)
    depth = norm_ffn1.shape[0]
    stacked = dict(f1g=w_ffn1_gate.astype(BF16), f1u=w_ffn1_up.astype(BF16), f1d=w_ffn1_down.astype(BF16),
                   f2g=w_ffn2_gate.astype(BF16), f2u=w_ffn2_up.astype(BF16), f2d=w_ffn2_down.astype(BF16),
                   w_out=w_out.astype(BF16))
    stacked["w_in_a"], stacked["w_in_b"] = _split_w_in(w_in)
    layers = [_prep_layer(l, w, stacked) for l in range(depth)]
    nf = norm_final.reshape(1, D_MODEL).astype(F32)
    bp, seq, _ = x_prompt.shape
    bs, dseq, _ = x_sample.shape
    seg_b = CHUNK // dseq

    zeros = lambda *s: jnp.zeros((depth,) + s, F32)
    _, md, mdc, ml, mlc = _run_group(
        meta_tokens.astype(F32)[None], zeros(1, H_A, DK, DV), zeros(1, CONV_W - 1, D_QKV),
        zeros(1, 1, D_R), zeros(1, CONV_W - 1, D_R), layers, nf,
        seg=N_META, delta_tt=N_META, delta_bb=1, tiles=((0, 1, 0),), lru_tt=N_META, lru_bb=1,
        states_only=True)
    y_prompt, pd, pdc, plr, plc = _run_group(x_prompt, md, mdc, ml, mlc, layers, nf, fused=True)
    y_sample, sdn, sdcn, sln, slcn = _run_group(
        x_sample, state_delta, state_delta_conv, state_lru.reshape(depth, bs, 1, D_R), state_lru_conv,
        layers, nf, seg=dseq, delta_tt=dseq, delta_bb=SAMPLE_TILES_PER_STEP * seg_b,
        tiles=tuple((j * seg_b, seg_b, 0) for j in range(SAMPLE_TILES_PER_STEP)),
        lru_tt=dseq, lru_bb=2 * seg_b)

    return (y_prompt, y_sample, pd, pdc, plr.reshape(depth, bp, D_R), plc,
            sdn, sdcn, sln.reshape(depth, bs, D_R), slcn)
```

```python
import functools
import math

import jax
import jax.numpy as jnp
from jax import lax
from jax.experimental import pallas as pl
from jax.experimental.pallas import tpu as pltpu

F32 = jnp.float32
BF16 = jnp.bfloat16

D_MODEL = 1024
D_FF = 2816
D_A = 512
D_R = 512
H_A = 4
DK = 128
DV = 128
CHUNK = 64
CONV_W = 4
NB_R = 8
C_RG = 8.0
EPS = 1e-6
N_META = 16
D_QKV = 3 * D_A
D_BA = 128
D_PROJ_A = D_QKV + D_A
D_PROJ_B = 2 * D_R + D_BA

SUBLANES = 8
BF16_ROWS = 16
TAIL = SUBLANES
VMEM_BYTES_V7X = 64 * 1024 * 1024
VMEM_LIMIT = VMEM_BYTES_V7X - 8 * 1024 * 1024
ROW_TILE = 512
MIX_TILE = 256
FF_PIECE = 256
PROJ_PIECE = 256
MIX_PLAN = "ad" * 7 + ("adl" + "ad") * 3 + "adl" + "ad" * 8
SAMPLE_TILES_PER_STEP = 2


def _rms(x, w):
    ms = jnp.mean(x * x, axis=-1, keepdims=True)
    return x * lax.rsqrt(ms + EPS) * w


def _sigmoid(x):
    return 1.0 / (1.0 + jnp.exp(-x))


def _silu(x):
    return x * _sigmoid(x)


def _softplus(x):
    t = jnp.exp(-jnp.abs(x))
    u = 1.0 + t
    d = u - 1.0
    log1p_t = jnp.where(d == 0.0, t, jnp.log(u) * (t / jnp.where(d == 0.0, 1.0, d)))
    return jnp.maximum(x, 0.0) + log1p_t


def _dot(a, b):
    return jnp.dot(a.astype(BF16), b.astype(BF16), preferred_element_type=F32)


def _dot_nt(a, b):
    return lax.dot_general(a.astype(BF16), b.astype(BF16), (((1,), (1,)), ((), ())),
                           preferred_element_type=F32)


def _dot_tn(a, b):
    return lax.dot_general(a.astype(BF16), b.astype(BF16), (((0,), (0,)), ((), ())),
                           preferred_element_type=F32)


def _split3(a):
    a1 = a.astype(BF16)
    r = a - a1.astype(F32)
    a2 = r.astype(BF16)
    a3 = (r - a2.astype(F32)).astype(BF16)
    return a1, a2, a3


def _ffn_pieces(h, wg_ref, wu_ref, wd_ref):
    def down(gu, acc):
        g, u, c0 = gu
        a = (_silu(g) * u).astype(BF16)
        d = jnp.dot(a, wd_ref[c0:c0 + FF_PIECE, :], preferred_element_type=F32)
        return d if acc is None else acc + d

    acc, pending = None, None
    for c0 in range(0, D_FF, FF_PIECE):
        g = jnp.dot(h, wg_ref[:, c0:c0 + FF_PIECE], preferred_element_type=F32)
        u = jnp.dot(h, wu_ref[:, c0:c0 + FF_PIECE], preferred_element_type=F32)
        if pending is not None:
            acc = down(pending, acc)
        pending = (g, u, c0)
        yield g
    yield down(pending, acc)


def _ffn(x, nw, wg_ref, wu_ref, wd_ref):
    h = _rms(x, nw).astype(BF16)
    for acc in _ffn_pieces(h, wg_ref, wu_ref, wd_ref):
        pass
    return x + 0.5 * acc


def _ffn_in_body(x_ref, n1_ref, wg_ref, wu_ref, wd_ref, nm_ref, wina_ref, winb_ref,
                 x1_ref, qkv_ref, z_ref, xr_ref, gate_ref, ba_ref):
    x1 = _ffn(x_ref[...], n1_ref[...], wg_ref, wu_ref, wd_ref)
    x1_ref[...] = x1
    hm = _rms(x1, nm_ref[...]).astype(BF16)
    pa = jnp.dot(hm, wina_ref[...], preferred_element_type=F32)
    pb = jnp.dot(hm, winb_ref[...], preferred_element_type=F32)
    qkv_ref[...] = pa[:, :D_QKV]
    z_ref[...] = pa[:, D_QKV:]
    xr_ref[...] = pb[:, :D_R]
    gate_ref[...] = pb[:, D_R:2 * D_R]
    ba_ref[...] = pb[:, 2 * D_R:]


def _out_ffn_body(x_ref, o_ref, y_ref, wo1_ref, wo2_ref, n2_ref, wg_ref, wu_ref, wd_ref, nf_ref, *rest,
                  final, n_cast):
    src_refs, out_ref, dst_refs = rest[:n_cast], rest[n_cast], rest[n_cast + 1:]
    x = x_ref[...] + (_dot(o_ref[...], wo1_ref[...]) + _dot(y_ref[...], wo2_ref[...]))
    x2 = _ffn(x, n2_ref[...], wg_ref, wu_ref, wd_ref)
    if final:
        x2 = _rms(x2, nf_ref[...])
    out_ref[...] = x2
    for s_ref, d_ref in zip(src_refs, dst_refs):
        d_ref[...] = s_ref[...].astype(BF16)


def _resident(shape):
    nd = len(shape)
    return pl.BlockSpec(shape, lambda *_: (0,) * nd, pipeline_mode=pl.Buffered(1))


def _row_block(shape, first):
    return pl.BlockSpec(shape, lambda *_: (first, 0), pipeline_mode=pl.Buffered(1))


def _rows(tm, width):
    return pl.BlockSpec((tm, width), lambda i: (i, 0))


def _ffn_in(x, lw):
    r = x.shape[0]
    tm = min(ROW_TILE, r)
    widths = (D_MODEL, D_QKV, D_A, D_R, D_R, D_BA)
    return pl.pallas_call(
        _ffn_in_body,
        grid=(pl.cdiv(r, tm),),
        in_specs=[_rows(tm, D_MODEL), _resident((1, D_MODEL)),
                  _resident((D_MODEL, D_FF)), _resident((D_MODEL, D_FF)), _resident((D_FF, D_MODEL)),
                  _resident((1, D_MODEL)), _resident((D_MODEL, D_PROJ_A)), _resident((D_MODEL, D_PROJ_B))],
        out_specs=[_rows(tm, w) for w in widths],
        out_shape=[jax.ShapeDtypeStruct((r, w), F32) for w in widths],
        compiler_params=pltpu.CompilerParams(dimension_semantics=("parallel",),
                                             vmem_limit_bytes=VMEM_LIMIT),
        name="ffn_in",
    )(x, lw["n1"], lw["f1g"], lw["f1u"], lw["f1d"], lw["nm"], lw["w_in_a"], lw["w_in_b"])


def _out_ffn(x, o, y, lw, nf, final, cast=()):
    r = x.shape[0]
    tm = min(ROW_TILE, r)
    steps = pl.cdiv(r, tm)
    cast_in, cast_out, cast_shape = [], [], []
    for arr, layer, cols in cast:
        rows = arr.shape[1]
        cr = BF16_ROWS * pl.cdiv(pl.cdiv(rows, steps), BF16_ROWS)
        last = pl.cdiv(rows, cr) - 1
        cast_in.append(pl.BlockSpec((None, cr, cols), lambda i, layer=layer, last=last: (layer, jnp.minimum(i, last), 0)))
        cast_out.append(pl.BlockSpec((cr, cols), lambda i, last=last: (jnp.minimum(i, last), 0)))
        cast_shape.append(jax.ShapeDtypeStruct((rows, cols), BF16))
    res = pl.pallas_call(
        functools.partial(_out_ffn_body, final=final, n_cast=len(cast)),
        grid=(steps,),
        in_specs=[_rows(tm, D_MODEL), _rows(tm, D_A), _rows(tm, D_R),
                  _row_block((D_A, D_MODEL), 0), _row_block((D_R, D_MODEL), 1), _resident((1, D_MODEL)),
                  _resident((D_MODEL, D_FF)), _resident((D_MODEL, D_FF)), _resident((D_FF, D_MODEL)),
                  _resident((1, D_MODEL))] + cast_in,
        out_specs=[_rows(tm, D_MODEL)] + cast_out,
        out_shape=[jax.ShapeDtypeStruct((r, D_MODEL), F32)] + cast_shape,
        compiler_params=pltpu.CompilerParams(dimension_semantics=("arbitrary",),
                                             vmem_limit_bytes=VMEM_LIMIT),
        name="out_ffn",
    )(x, o, y, lw["w_out"], lw["w_out"], lw["n2"], lw["f2g"], lw["f2u"], lw["f2d"], nf, *[c[0] for c in cast])
    return res[0], res[1:]


class _Io:
    def __init__(self, **fns):
        self.after = lambda x: x
        self.__dict__.update(fns)


def _delta_stream(io, cw_ref, alog_ref, dtb_ref, on_ref, *, tiles, seg):
    c = seg
    nb = tiles[0][1]
    r = nb * c
    levels = int(math.log2(c))
    assert 2 ** levels == c
    row = lax.broadcasted_iota(jnp.int32, (r, r), 0)
    col = lax.broadcasted_iota(jnp.int32, (r, r), 1)
    same = lax.shift_right_logical(row, levels) == lax.shift_right_logical(col, levels)
    causal = same & (row >= col)
    strict = same & (row > col)
    tri = causal.astype(BF16)
    ones_seg = same.astype(BF16)
    eye = (row == col).astype(F32)
    sel_r = lax.broadcasted_iota(jnp.int32, (SUBLANES, D_BA), 0)
    sel_c = lax.broadcasted_iota(jnp.int32, (SUBLANES, D_BA), 1)
    sel = ((sel_c == sel_r + H_A) & (sel_r < H_A)).astype(BF16)

    def conv_silu(tile, col0):
        win = io.ext(tile, col0, DK)
        acc = None
        for j in range(CONV_W):
            k = CONV_W - 1 - j
            x = win if k == 0 else pltpu.roll(win, k, 1)
            term = x[:, TAIL:, :].reshape(r, DK) * io.after(cw_ref[j:j + 1, col0:col0 + DK])
            acc = term if acc is None else acc + term
        return _silu(acc)

    def exact_dot(lhs01, x, nt=False):
        out = None
        for p in _split3(x):
            if nt:
                d = lax.dot_general(lhs01, p, (((1,), (1,)), ((), ())), preferred_element_type=F32)
            else:
                d = jnp.dot(lhs01, p, preferred_element_type=F32)
            out = d if out is None else out + d
        return out

    items = [(j, h) for j in range(len(tiles)) for h in range(H_A)]

    gcum, gtot, grow, eg_all, beta_all = [], [], [], [], []
    for tile in tiles:
        ba = io.ba(tile)
        g_all = -jnp.exp(io.after(alog_ref[...])) * _softplus(ba + dtb_ref[...])
        beta_all.append(_sigmoid(ba))
        gcum.append(exact_dot(tri, g_all))
        gtot.append(exact_dot(ones_seg, g_all) if nb > 1 else gcum[-1][r - 1:r, :])
    yield
    for j in range(len(tiles)):
        grow.append(exact_dot(sel, gcum[j], nt=True))
        eg_all.append(jnp.exp(gcum[j]))
    yield

    qg, kn, kb, kdec, decay, rhs = {}, {}, {}, {}, {}, {}
    for (j, h) in items:
        tile = tiles[j]
        qh = conv_silu(tile, h * DK)
        kh = conv_silu(tile, D_A + h * DK)
        vh = conv_silu(tile, 2 * D_A + h * DV)
        qn = qh * lax.rsqrt(jnp.sum(qh * qh, axis=-1, keepdims=True) + EPS) * (DK ** -0.5)
        k_n = kh * lax.rsqrt(jnp.sum(kh * kh, axis=-1, keepdims=True) + EPS)
        beta = beta_all[j][:, h:h + 1]
        gcol = gcum[j][:, H_A + h:H_A + h + 1]
        egcol = eg_all[j][:, H_A + h:H_A + h + 1]
        glast = gtot[j][:, H_A + h:H_A + h + 1]
        dmat = gcol - grow[j][h:h + 1, :]
        decay[j, h] = jnp.where(causal, jnp.exp(jnp.where(causal, dmat, 0.0)), 0.0)
        kn[j, h] = k_n
        kb[j, h] = k_n * beta
        qg[j, h] = (qn, qn * egcol)
        kdec[j, h] = k_n * jnp.exp(glast - gcol)
        rhs[j, h] = jnp.concatenate([vh * beta, kb[j, h] * egcol], axis=-1)
        if h == H_A - 1:
            yield

    kq = {it: _dot_nt(jnp.concatenate([kb[it], qg[it][0]], axis=0), kn[it]) for it in items}
    lmat = {it: jnp.where(strict, kq[it][:r] * decay[it], 0.0) for it in items}
    attn = {it: kq[it][r:] * decay[it] for it in items}
    yield

    tinv = {it: eye - lmat[it] for it in items}
    m = {it: _dot(lmat[it], lmat[it]) for it in items}
    yield
    for _ in range(levels - 2):
        tinv = {it: tinv[it] + _dot(tinv[it], m[it]) for it in items}
        m = {it: _dot(m[it], m[it]) for it in items}
        yield
    tinv = {it: tinv[it] + _dot(tinv[it], m[it]) for it in items}
    yield
    sol = {it: _dot(tinv[it], rhs[it]) for it in items}
    yield

    state = {}

    def get_state(b, h):
        if (b, h) not in state:
            state[b, h] = io.get_state(b, h)
        return state[b, h]

    for j, tile in enumerate(tiles):
        b0 = tile[0]
        segs = [(s, slice(s * c, (s + 1) * c)) for s in range(nb)]
        ws = {}
        for h in range(H_A):
            for s, rs in segs:
                lhs = jnp.concatenate([sol[j, h][rs, DV:], qg[j, h][1][rs]], axis=0)
                ws[h, s] = _dot(lhs, get_state(b0 + s, h))
        yield
        v_new, av = {}, {}
        for h in range(H_A):
            w_s = jnp.concatenate([ws[h, s][:c] for s, _ in segs], axis=0) if nb > 1 else ws[h, 0][:c]
            v_new[h] = sol[j, h][:, :DV] - w_s
        for h in range(H_A):
            av[h] = _dot(attn[j, h], v_new[h])
        for h in range(H_A):
            for s, rs in segs:
                g0 = rs.start if nb > 1 else 0
                scale = jnp.exp(gtot[j][g0:g0 + 1, H_A + h:H_A + h + 1])
                state[b0 + s, h] = state[b0 + s, h] * scale + _dot_tn(kdec[j, h][rs], v_new[h][rs])
        for h in range(H_A):
            q_s = jnp.concatenate([ws[h, s][c:] for s, _ in segs], axis=0) if nb > 1 else ws[h, 0][c:]
            on = _rms(q_s + av[h], io.after(on_ref[...]))
            io.put_o(tile, h, on * _silu(io.z(tile, h * DV)))
        if j == len(tiles) - 1:
            for (b, h), val in state.items():
                io.put_state(b, h, val)
        yield


def _delta_body(qkv_ref, z_ref, ba_ref, cs_ref, s0_ref, cw_ref, alog_ref, dtb_ref, on_ref, *rest,
                tiles, seg, tt, n_t, fill_layers):
    if fill_layers:
        o_ref, sall_ref, csnew_ref, ext_ref = rest
        snew_ref = sall_ref.at[0]
    else:
        _, o_ref, snew_ref, csnew_ref, ext_ref = rest
    c = seg
    nb = tiles[0][1]
    r = nb * c
    t = pl.program_id(1)

    @pl.when(t == 0)
    def _():
        ext_ref[:, TAIL - (CONV_W - 1):TAIL, :] = cs_ref[...]
        snew_ref[...] = s0_ref[...]

    ext_ref[:, TAIL:TAIL + tt, :] = qkv_ref[...]

    def rows(ref, tile, off, col0, width):
        b0, _, t0 = tile
        return ref[b0:b0 + nb, off + t0:off + t0 + c, col0:col0 + width].reshape(r, width)

    def put_o(tile, h, val):
        b0, _, t0 = tile
        o_ref[b0:b0 + nb, t0:t0 + c, h * DV:(h + 1) * DV] = val.reshape(nb, c, DV).astype(BF16)

    def put_state(b, h, val):
        snew_ref[b, h] = val

    def ext(tile, col0, w):
        b0, _, t0 = tile
        return ext_ref[b0:b0 + nb, t0:t0 + TAIL + c, col0:col0 + w]

    io = _Io(ext=ext, ba=lambda tile: rows(ba_ref, tile, 0, 0, D_BA),
             z=lambda tile, col0: rows(z_ref, tile, 0, col0, DV),
             put_o=put_o, get_state=lambda b, h: snew_ref[b, h], put_state=put_state)
    for _ in _delta_stream(io, cw_ref, alog_ref, dtb_ref, on_ref, tiles=tiles, seg=seg):
        pass

    tail = ext_ref[:, tt:tt + TAIL, :]
    ext_ref[:, 0:TAIL, :] = tail

    @pl.when(t == n_t - 1)
    def _():
        csnew_ref[...] = ext_ref[:, TAIL - (CONV_W - 1):TAIL, :]
        if fill_layers:
            for d in range(1, fill_layers):
                sall_ref[d] = sall_ref[0]


def _delta(qkv, z, ba, cs, s0, lw, l, s_all, *, seg, tt, bb, tiles):
    b, t, _ = qkv.shape
    depth = s0.shape[0]
    n_t = t // tt
    assert n_t * tt == t and b % bb == 0 and (l == 0) == (s_all is None)
    st = lambda i, j: (l, i, 0, 0, 0)
    ct = lambda i, j: (l, i, 0, 0)
    tok = lambda w: pl.BlockSpec((bb, tt, w), lambda i, j: (i, j, 0))
    const = lambda shape: pl.BlockSpec(shape, lambda i, j: (0,) * len(shape))
    in_specs = [tok(D_QKV), tok(D_A), tok(D_BA),
                pl.BlockSpec((None, bb, CONV_W - 1, D_QKV), ct),
                pl.BlockSpec((None, bb, H_A, DK, DV), st),
                const((CONV_W, D_QKV)), const((1, D_BA)), const((1, D_BA)), const((1, DV))]
    operands = [qkv, z, ba, cs, s0, lw["cqkv"], lw["alog"], lw["dtb"], lw["onorm"]]
    if s_all is None:
        s_spec = pl.BlockSpec((depth, bb, H_A, DK, DV), lambda i, j: (0, i, 0, 0, 0))
        aliases = {}
    else:
        s_spec = pl.BlockSpec((None, bb, H_A, DK, DV), st)
        in_specs.append(pl.BlockSpec(memory_space=pl.ANY))
        operands.append(s_all)
        aliases = {len(operands) - 1: 1}
    return pl.pallas_call(
        functools.partial(_delta_body, tiles=tiles, seg=seg, tt=tt, n_t=n_t,
                          fill_layers=depth if s_all is None else 0),
        grid=(b // bb, n_t),
        in_specs=in_specs,
        out_specs=[tok(D_A), s_spec,
                   pl.BlockSpec((bb, CONV_W - 1, D_QKV), lambda i, j: (i, 0, 0))],
        out_shape=[jax.ShapeDtypeStruct((b, t, D_A), BF16),
                   jax.ShapeDtypeStruct((depth, b, H_A, DK, DV), F32),
                   jax.ShapeDtypeStruct((b, CONV_W - 1, D_QKV), F32)],
        scratch_shapes=[pltpu.VMEM((bb, tt + TAIL, D_QKV), F32)],
        input_output_aliases=aliases,
        compiler_params=pltpu.CompilerParams(dimension_semantics=("arbitrary", "arbitrary"),
                                             vmem_limit_bytes=VMEM_LIMIT),
        name="delta",
    )(*operands)


def _lru_stream(io, cw_ref, cb_ref, wg_ref, bg_ref, lam_ref, *, tt, bb):
    win = io.ext()
    xc = None
    for j in range(CONV_W):
        k = CONV_W - 1 - j
        x = win if k == 0 else pltpu.roll(win, k, 1)
        term = x[:, TAIL:, :] * io.after(cw_ref[j:j + 1, :])
        xc = term if xc is None else xc + term
    xc = xc + cb_ref[...]
    n = bb * tt
    xc = xc.reshape(n, D_R)
    yield
    gates = jnp.dot(xc.astype(BF16), wg_ref[...], preferred_element_type=F32) + bg_ref[...]
    r = _sigmoid(gates[:, :D_R])
    i = _sigmoid(gates[:, D_R:])
    log_a = -C_RG * r * _softplus(-io.after(lam_ref[...]))
    a = jnp.exp(log_a)
    m2 = -jnp.tanh(log_a) * (a * a + 1.0)
    u = jnp.where(m2 > 0.0, m2 * lax.rsqrt(m2), 0.0) * i * xc
    yield

    ng = n // SUBLANES
    a = a.reshape(ng, SUBLANES, D_R)
    u = u.reshape(ng, SUBLANES, D_R)
    tpos = lax.broadcasted_iota(jnp.int32, (ng, SUBLANES, D_R), 1)
    s = 1
    while s < SUBLANES:
        a_sh = pltpu.roll(a, s, 1)
        u_sh = pltpu.roll(u, s, 1)
        m = tpos >= s
        u = jnp.where(m, u + a * u_sh, u)
        a = jnp.where(m, a * a_sh, a)
        s *= 2
    yield
    gpb = tt // SUBLANES
    if gpb == 1:
        hs = u + a * io.h0()
        h_last = hs[:, SUBLANES - 1:SUBLANES, :]
    else:
        assert bb == 1
        h_in = io.h0()[0]
        groups = []
        for g in range(gpb):
            hg = u[g] + a[g] * h_in
            groups.append(hg)
            h_in = hg[SUBLANES - 1:SUBLANES, :]
        hs = jnp.concatenate(groups, axis=0)
        h_last = h_in.reshape(1, 1, D_R)
    y = hs.reshape(n, D_R) * jax.nn.gelu(io.gate(), approximate=True)
    io.put_y(y.reshape(bb, tt, D_R))
    io.put_h(h_last)
    yield


def _lru_body(xr_ref, gate_ref, cs_ref, h0_ref, cw_ref, cb_ref, wg_ref, bg_ref, lam_ref,
              y_ref, hnew_ref, csnew_ref, ext_ref, *, tt, bb, n_t):
    t = pl.program_id(1)

    @pl.when(t == 0)
    def _():
        ext_ref[:, TAIL - (CONV_W - 1):TAIL, :] = cs_ref[...]
        hnew_ref[...] = h0_ref[...]

    ext_ref[:, TAIL:TAIL + tt, :] = xr_ref[...]

    def put_y(val):
        y_ref[...] = val.astype(BF16)

    def put_h(val):
        hnew_ref[...] = val

    io = _Io(ext=lambda: ext_ref[...], gate=lambda: gate_ref[...].reshape(bb * tt, D_R),
             h0=lambda: hnew_ref[...], put_y=put_y, put_h=put_h)
    for _ in _lru_stream(io, cw_ref, cb_ref, wg_ref, bg_ref, lam_ref, tt=tt, bb=bb):
        pass

    tail = ext_ref[:, tt:tt + TAIL, :]
    ext_ref[:, 0:TAIL, :] = tail

    @pl.when(t == n_t - 1)
    def _():
        csnew_ref[...] = ext_ref[:, TAIL - (CONV_W - 1):TAIL, :]


def _lru(xr, gate, cs, h0, lw, l, *, tt, bb):
    b, t, _ = xr.shape
    n_t = t // tt
    assert n_t * tt == t and b % bb == 0 and tt % SUBLANES == 0
    ct = lambda i, j: (l, i, 0, 0)
    tok = pl.BlockSpec((bb, tt, D_R), lambda i, j: (i, j, 0))
    const = lambda shape: pl.BlockSpec(shape, lambda i, j: (0,) * len(shape))
    return pl.pallas_call(
        functools.partial(_lru_body, tt=tt, bb=bb, n_t=n_t),
        grid=(b // bb, n_t),
        in_specs=[tok, tok,
                  pl.BlockSpec((None, bb, CONV_W - 1, D_R), ct),
                  pl.BlockSpec((None, bb, 1, D_R), ct),
                  const((CONV_W, D_R)), const((1, D_R)), const((D_R, 2 * D_R)), const((1, 2 * D_R)),
                  const((1, D_R))],
        out_specs=[tok,
                   pl.BlockSpec((bb, 1, D_R), lambda i, j: (i, 0, 0)),
                   pl.BlockSpec((bb, CONV_W - 1, D_R), lambda i, j: (i, 0, 0))],
        out_shape=[jax.ShapeDtypeStruct((b, t, D_R), BF16),
                   jax.ShapeDtypeStruct((b, 1, D_R), F32),
                   jax.ShapeDtypeStruct((b, CONV_W - 1, D_R), F32)],
        scratch_shapes=[pltpu.VMEM((bb, tt + TAIL, D_R), F32)],
        compiler_params=pltpu.CompilerParams(dimension_semantics=("arbitrary", "arbitrary"),
                                             vmem_limit_bytes=VMEM_LIMIT),
        name="lru",
    )(xr, gate, cs, h0, lw["clw"], lw["clb"], lw["wgate"], lw["bgate"], lw["lam"])


def _interleave(streams, plan, lead, set_dep):
    live = dict(streams)

    def step(key):
        if key not in live:
            return
        try:
            val = next(live[key])
            if key == lead:
                set_dep(val)
        except StopIteration:
            del live[key]

    for key in plan:
        step(key)
    while live:
        for key in list(live):
            step(key)


def _in_mix_body(flag_ref, x_ref, n1_ref, wg_ref, wu_ref, wd_ref, nm_ref, wina_ref, winb_ref,
                 dcs_ref, ds0_ref, cw_ref, alog_ref, dtb_ref, on_ref,
                 lcs_ref, lh0_ref, lcw_ref, lcb_ref, lwg_ref, lbg_ref, lam_ref,
                 x1_ref, o_ref, y_ref, snew_ref, dcsnew_ref, hnew_ref, lcsnew_ref,
                 qkv_scr, z_scr, ba_scr, xr_scr, gate_scr, dtail_scr, ltail_scr, s_scr, h_scr,
                 *, tiles_per_seq):
    tt = MIX_TILE
    s = pl.program_id(0)
    slot_a = lax.rem(s, 2)
    slot_b = 1 - slot_a
    first = lax.rem(s + tiles_per_seq - 1, tiles_per_seq) == 0
    tails = slice(TAIL - (CONV_W - 1), TAIL)
    keep = flag_ref[0] == 1
    dep = [None]

    def after(x):
        if dep[0] is None:
            return x
        reps = x.shape[-1] // DK
        d = dep[0] if reps == 1 else jnp.concatenate([dep[0]] * reps, axis=-1)
        return jnp.where(keep, x, d)

    def set_dep(val):
        dep[0] = val

    @pl.when(s == 0)
    def _():
        for ref in (qkv_scr, z_scr, ba_scr, xr_scr, gate_scr, dtail_scr, ltail_scr, s_scr, h_scr):
            ref[...] = jnp.zeros(ref.shape, F32)

    def stream_a():
        x = x_ref[...]
        h = _rms(x, n1_ref[...]).astype(BF16)
        pieces = _ffn_pieces(h, wg_ref, wu_ref, wd_ref)
        for _ in range(D_FF // FF_PIECE):
            yield next(pieces)[0:1, 0:DK]
        acc = next(pieces)
        x1 = x + 0.5 * acc
        x1_ref[...] = x1
        hm = _rms(x1, nm_ref[...]).astype(BF16)
        yield acc[0:1, 0:DK]
        dsts = ((qkv_scr, TAIL, wina_ref, 0, D_QKV), (z_scr, 0, wina_ref, D_QKV, D_A),
                (xr_scr, TAIL, winb_ref, 0, D_R), (gate_scr, 0, winb_ref, D_R, D_R),
                (ba_scr, 0, winb_ref, 2 * D_R, D_BA))
        for dst, row0, w_ref, col0, width in dsts:
            for d0 in range(0, width, PROJ_PIECE):
                w = min(PROJ_PIECE, width - d0)
                p = jnp.dot(hm, w_ref[:, col0 + d0:col0 + d0 + w], preferred_element_type=F32)
                dst[slot_a, row0:row0 + tt, d0:d0 + w] = p
                yield p[0:1, 0:DK]

    qkv_scr[slot_b, tails, :] = jnp.where(first, dcs_ref[0], dtail_scr[tails, :])
    xr_scr[slot_b, tails, :] = jnp.where(first, lcs_ref[0], ltail_scr[tails, :])

    def put_o(tile, h, val):
        o_ref[tile[2]:tile[2] + CHUNK, h * DV:(h + 1) * DV] = val.astype(BF16)

    def put_state(b, h, val):
        s_scr[h] = val
        snew_ref[0, h] = val

    def put_y(val):
        y_ref[...] = val.reshape(tt, D_R).astype(BF16)

    def put_h(val):
        h_scr[...] = val
        hnew_ref[...] = val

    dio = _Io(ext=lambda tile, col0, w: qkv_scr[slot_b, tile[2]:tile[2] + TAIL + CHUNK, col0:col0 + w][None],
              ba=lambda tile: ba_scr[slot_b, tile[2]:tile[2] + CHUNK, :],
              z=lambda tile, col0: z_scr[slot_b, tile[2]:tile[2] + CHUNK, col0:col0 + DV],
              put_o=put_o, get_state=lambda b, h: jnp.where(first, ds0_ref[0, h], s_scr[h]),
              put_state=put_state, after=after)
    lio = _Io(ext=lambda: xr_scr[slot_b][None],
              gate=lambda: gate_scr[slot_b], h0=lambda: jnp.where(first, lh0_ref[...], h_scr[...]),
              put_y=put_y, put_h=put_h, after=after)
    tiles = tuple((0, 1, j * CHUNK) for j in range(tt // CHUNK))
    _interleave({"a": stream_a(),
                 "d": _delta_stream(dio, cw_ref, alog_ref, dtb_ref, on_ref, tiles=tiles, seg=CHUNK),
                 "l": _lru_stream(lio, lcw_ref, lcb_ref, lwg_ref, lbg_ref, lam_ref, tt=tt, bb=1)},
                MIX_PLAN, "a", set_dep)

    dtail_scr[...] = qkv_scr[slot_b, tt:tt + TAIL, :]
    ltail_scr[...] = xr_scr[slot_b, tt:tt + TAIL, :]
    dcsnew_ref[0] = qkv_scr[slot_b, tt + TAIL - (CONV_W - 1):tt + TAIL, :]
    lcsnew_ref[0] = xr_scr[slot_b, tt + TAIL - (CONV_W - 1):tt + TAIL, :]


def _in_mix(x, sdc, sd, slc, sl, lw, l, *, batch):
    r = x.shape[0]
    tt = MIX_TILE
    n_tiles = r // tt
    tiles_per_seq = n_tiles // batch
    assert n_tiles * tt == r and tiles_per_seq * batch == n_tiles
    cur = lambda s: (jnp.minimum(s, n_tiles - 1), 0)
    prev = lambda s: (jnp.maximum(s - 1, 0), 0)
    seq3 = lambda s: (jnp.maximum(s - 1, 0) // tiles_per_seq, 0, 0)
    seq4 = lambda s: (jnp.maximum(s - 1, 0) // tiles_per_seq, 0, 0, 0)
    layer = lambda shape: pl.BlockSpec((None,) + shape, lambda s: (l,) + (0,) * len(shape),
                                       pipeline_mode=pl.Buffered(1))
    return pl.pallas_call(
        functools.partial(_in_mix_body, tiles_per_seq=tiles_per_seq),
        grid=(n_tiles + 1,),
        in_specs=[pl.BlockSpec(memory_space=pltpu.SMEM),
                  pl.BlockSpec((tt, D_MODEL), cur), _resident((1, D_MODEL)),
                  _resident((D_MODEL, D_FF)), _resident((D_MODEL, D_FF)), _resident((D_FF, D_MODEL)),
                  _resident((1, D_MODEL)), _resident((D_MODEL, D_PROJ_A)), _resident((D_MODEL, D_PROJ_B)),
                  layer((1, CONV_W - 1, D_QKV)), layer((1, H_A, DK, DV)),
                  _resident((CONV_W, D_QKV)), _resident((1, D_BA)), _resident((1, D_BA)), _resident((1, DV)),
                  layer((1, CONV_W - 1, D_R)), layer((1, 1, D_R)),
                  _resident((CONV_W, D_R)), _resident((1, D_R)), _resident((D_R, 2 * D_R)),
                  _resident((1, 2 * D_R)), _resident((1, D_R))],
        out_specs=[pl.BlockSpec((tt, D_MODEL), cur), pl.BlockSpec((tt, D_A), prev), pl.BlockSpec((tt, D_R), prev),
                   pl.BlockSpec((1, H_A, DK, DV), seq4), pl.BlockSpec((1, CONV_W - 1, D_QKV), seq3),
                   pl.BlockSpec((1, 1, D_R), seq3), pl.BlockSpec((1, CONV_W - 1, D_R), seq3)],
        out_shape=[jax.ShapeDtypeStruct((r, D_MODEL), F32), jax.ShapeDtypeStruct((r, D_A), BF16),
                   jax.ShapeDtypeStruct((r, D_R), BF16),
                   jax.ShapeDtypeStruct((batch, H_A, DK, DV), F32),
                   jax.ShapeDtypeStruct((batch, CONV_W - 1, D_QKV), F32),
                   jax.ShapeDtypeStruct((batch, 1, D_R), F32),
                   jax.ShapeDtypeStruct((batch, CONV_W - 1, D_R), F32)],
        scratch_shapes=[pltpu.VMEM((2, TAIL + tt, D_QKV), F32), pltpu.VMEM((2, tt, D_A), F32),
                        pltpu.VMEM((2, tt, D_BA), F32), pltpu.VMEM((2, TAIL + tt, D_R), F32),
                        pltpu.VMEM((2, tt, D_R), F32), pltpu.VMEM((TAIL, D_QKV), F32),
                        pltpu.VMEM((TAIL, D_R), F32), pltpu.VMEM((H_A, DK, DV), F32),
                        pltpu.VMEM((1, 1, D_R), F32)],
        compiler_params=pltpu.CompilerParams(dimension_semantics=("arbitrary",),
                                             vmem_limit_bytes=VMEM_LIMIT),
        name="in_mix",
    )(jnp.ones((1,), jnp.int32), x, lw["n1"], lw["f1g"], lw["f1u"], lw["f1d"], lw["nm"], lw["w_in_a"], lw["w_in_b"],
      sdc, sd, lw["cqkv"], lw["alog"], lw["dtb"], lw["onorm"],
      slc, sl, lw["clw"], lw["clb"], lw["wgate"], lw["bgate"], lw["lam"])


BIG_WEIGHTS = ("f1g", "f1u", "f1d", "w_in_a", "w_out", "f2g", "f2u", "f2d")


def _prep_layer(l, w, big):
    row = lambda v: v.reshape(1, -1).astype(F32)
    lane_pad = lambda v: jnp.zeros((1, D_BA), F32).at[0, H_A:2 * H_A].set(v)
    eye = jnp.eye(NB_R, dtype=F32)
    bd = lambda wb: jnp.einsum("ncd,nm->ncmd", wb, eye).reshape(D_R, D_R)
    w_in = w["w_in"][l]
    c4 = D_PROJ_A + 2 * H_A
    pad = jnp.zeros((D_MODEL, D_BA - 2 * H_A), w_in.dtype)
    w_in_b = jnp.concatenate([w_in[:, c4:], w_in[:, D_PROJ_A:c4], pad], axis=-1).astype(BF16)
    return dict(
        n1=row(w["norm_ffn1"][l]), f1g=big["f1g"], f1u=big["f1u"], f1d=big["f1d"],
        nm=row(w["norm_mix"][l]), w_in_a=big["w_in_a"], w_in_b=w_in_b,
        cqkv=w["conv_qkv"][l], alog=lane_pad(w["a_log"][l]), dtb=lane_pad(w["dt_bias"][l]),
        onorm=row(w["norm_delta_out"][l]),
        clw=w["conv_lru_w"][l], clb=row(w["conv_lru_b"][l]),
        wgate=jnp.concatenate([bd(w["w_rgate"][l]), bd(w["w_igate"][l])], axis=1).astype(BF16),
        bgate=jnp.concatenate([row(w["b_rgate"][l]), row(w["b_igate"][l])], axis=1),
        lam=row(w["lru_lambda"][l]),
        w_out=big["w_out"], n2=row(w["norm_ffn2"][l]), f2g=big["f2g"], f2u=big["f2u"], f2d=big["f2d"],
    )


def _group_layers(x, states_at, layers, nf, out, *, seg=None, delta_tt=None, delta_bb=None, tiles=None,
                  lru_tt=None, lru_bb=None, fused=False, states_only=False, cast_next=None):
    b, t, _ = x.shape
    xf = x.reshape(b * t, D_MODEL)
    s_all = None
    n_layers = len(layers)
    for l in range(n_layers):
        lw = layers[l]
        sd, sdc, sl, slc = states_at(l)
        if fused:
            x1, o, y, s_new, cs_new, h_new, lcs_new = _in_mix(xf, sdc, sd, slc, sl, lw, l, batch=b)
            out["nd"].append(s_new)
        else:
            x1, qkv, z, xr, gate, ba = _ffn_in(xf, lw)
            r3 = lambda v: v.reshape(b, t, v.shape[-1])
            o, s_all, cs_new = _delta(r3(qkv), r3(z), r3(ba), sdc, sd, lw, l, s_all,
                                      seg=seg, tt=delta_tt, bb=delta_bb, tiles=tiles)
            y, h_new, lcs_new = _lru(r3(xr), r3(gate), slc, sl, lw, l, tt=lru_tt, bb=lru_bb)
            out["s_all"] = s_all
        last = l == n_layers - 1
        if not (last and states_only):
            cast = cast_next(l) if cast_next is not None and not last else ()
            xf, copies = _out_ffn(x1, o.reshape(b * t, D_A), y.reshape(b * t, D_R), lw, nf, last, cast)
            if cast:
                out["cast"][l + 1] = dict(zip(BIG_WEIGHTS, copies))
        out["ndc"].append(cs_new)
        out["nl"].append(h_new)
        out["nlc"].append(lcs_new)
        if last and not states_only:
            out["y"] = xf.reshape(b, t, D_MODEL)
        yield


def kernel(x_prompt, x_sample, state_delta, state_delta_conv, state_lru, state_lru_conv, meta_tokens, norm_ffn1, w_ffn1_gate, w_ffn1_up, w_ffn1_down, norm_mix, w_in, conv_qkv, a_log, dt_bias, norm_delta_out, conv_lru_w, conv_lru_b, w_rgate, b_rgate, w_igate, b_igate, lru_lambda, w_out, norm_ffn2, w_ffn2_gate, w_ffn2_up, w_ffn2_down, norm_final):
    w = dict(norm_ffn1=norm_ffn1, w_ffn1_gate=w_ffn1_gate, w_ffn1_up=w_ffn1_up, w_ffn1_down=w_ffn1_down,
             norm_mix=norm_mix, w_in=w_in, conv_qkv=conv_qkv, a_log=a_log, dt_bias=dt_bias,
             norm_delta_out=norm_delta_out, conv_lru_w=conv_lru_w, conv_lru_b=conv_lru_b,
             w_rgate=w_rgate, b_rgate=b_rgate, w_igate=w_igate, b_igate=b_igate, lru_lambda=lru_lambda,
             w_out=w_out, norm_ffn2=norm_ffn2, w_ffn2_gate=w_ffn2_gate, w_ffn2_up=w_ffn2_up,
             w_ffn2_down=w_ffn2_down)
    depth = norm_ffn1.shape[0]
    big_f32 = dict(f1g=(w_ffn1_gate, D_FF), f1u=(w_ffn1_up, D_FF), f1d=(w_ffn1_down, D_MODEL),
                   w_in_a=(w_in, D_PROJ_A), w_out=(w_out, D_MODEL),
                   f2g=(w_ffn2_gate, D_FF), f2u=(w_ffn2_up, D_FF), f2d=(w_ffn2_down, D_MODEL))
    layers = [None] * depth
    layers[0] = _prep_layer(0, w, {k: big_f32[k][0][0, :, :big_f32[k][1]].astype(BF16) for k in BIG_WEIGHTS})
    cast_next = lambda l: tuple((big_f32[k][0], l + 1, big_f32[k][1]) for k in BIG_WEIGHTS)
    nf = norm_final.reshape(1, D_MODEL).astype(F32)
    bp, seq, _ = x_prompt.shape
    bs, dseq, _ = x_sample.shape
    seg_b = CHUNK // dseq
    new_out = lambda: dict(nd=[], ndc=[], nl=[], nlc=[], cast={}, s_all=None, y=None)
    stack = jnp.stack

    zeros = lambda *s: jnp.zeros((depth,) + s, F32)
    m_states = (zeros(1, H_A, DK, DV), zeros(1, CONV_W - 1, D_QKV), zeros(1, 1, D_R), zeros(1, CONV_W - 1, D_R))
    m_out, p_out, s_out = new_out(), new_out(), new_out()
    meta = _group_layers(meta_tokens.astype(F32)[None], lambda l: m_states, layers, nf, m_out,
                         seg=N_META, delta_tt=N_META, delta_bb=1, tiles=((0, 1, 0),), lru_tt=N_META, lru_bb=1,
                         states_only=True)
    after_meta = lambda l: (m_out["s_all"], stack(m_out["ndc"]), stack(m_out["nl"]), stack(m_out["nlc"]))
    prompt = _group_layers(x_prompt, after_meta, layers, nf, p_out, fused=True, cast_next=cast_next)
    s_states = (state_delta, state_delta_conv, state_lru.reshape(depth, bs, 1, D_R), state_lru_conv)
    sample = _group_layers(x_sample, lambda l: s_states, layers, nf, s_out,
                           seg=dseq, delta_tt=dseq, delta_bb=SAMPLE_TILES_PER_STEP * seg_b,
                           tiles=tuple((j * seg_b, seg_b, 0) for j in range(SAMPLE_TILES_PER_STEP)),
                           lru_tt=dseq, lru_bb=2 * seg_b)
    for l in range(depth):
        next(meta)
        next(prompt)
        if l + 1 < depth:
            layers[l + 1] = _prep_layer(l + 1, w, p_out["cast"][l + 1])
        next(sample)

    return (p_out["y"], s_out["y"], stack(p_out["nd"]), stack(p_out["ndc"]),
            stack(p_out["nl"]).reshape(depth, bp, D_R), stack(p_out["nlc"]),
            s_out["s_all"], stack(s_out["ndc"]), stack(s_out["nl"]).reshape(depth, bs, D_R), stack(s_out["nlc"]))
```

```python
import functools
import math

import jax
import jax.numpy as jnp
from jax import lax
from jax.experimental import pallas as pl
from jax.experimental.pallas import tpu as pltpu

F32 = jnp.float32
BF16 = jnp.bfloat16

D_MODEL = 1024
D_FF = 2816
D_A = 512
D_R = 512
H_A = 4
DK = 128
DV = 128
CHUNK = 64
CONV_W = 4
NB_R = 8
C_RG = 8.0
EPS = 1e-6
N_META = 16
D_QKV = 3 * D_A
D_BA = 128
D_PROJ_A = D_QKV + D_A
D_PROJ_B = 2 * D_R + D_BA

SUBLANES = 8
BF16_ROWS = 16
TAIL = SUBLANES
VMEM_BYTES_V7X = 64 * 1024 * 1024
VMEM_LIMIT = VMEM_BYTES_V7X - 8 * 1024 * 1024
ROW_TILE = 512
MIX_TILE = 256
FF_PIECE = 256
PROJ_PIECE = 256
MIX_PLAN = "ad" * 7 + ("adl" + "ad") * 3 + "adl" + "ad" * 8
SAMPLE_TILES_PER_STEP = 2


def _rms(x, w):
    ms = jnp.mean(x * x, axis=-1, keepdims=True)
    return x * lax.rsqrt(ms + EPS) * w


def _sigmoid(x):
    return 1.0 / (1.0 + jnp.exp(-x))


def _silu(x):
    return x * _sigmoid(x)


def _softplus(x):
    t = jnp.exp(-jnp.abs(x))
    u = 1.0 + t
    d = u - 1.0
    log1p_t = jnp.where(d == 0.0, t, jnp.log(u) * (t / jnp.where(d == 0.0, 1.0, d)))
    return jnp.maximum(x, 0.0) + log1p_t


def _dot(a, b):
    return jnp.dot(a.astype(BF16), b.astype(BF16), preferred_element_type=F32)


def _dot_nt(a, b):
    return lax.dot_general(a.astype(BF16), b.astype(BF16), (((1,), (1,)), ((), ())),
                           preferred_element_type=F32)


def _dot_tn(a, b):
    return lax.dot_general(a.astype(BF16), b.astype(BF16), (((0,), (0,)), ((), ())),
                           preferred_element_type=F32)


def _split3(a):
    a1 = a.astype(BF16)
    r = a - a1.astype(F32)
    a2 = r.astype(BF16)
    a3 = (r - a2.astype(F32)).astype(BF16)
    return a1, a2, a3


def _ffn_pieces(h, wg_ref, wu_ref, wd_ref):
    def down(gu, acc):
        g, u, c0 = gu
        a = (_silu(g) * u).astype(BF16)
        d = jnp.dot(a, wd_ref[c0:c0 + FF_PIECE, :], preferred_element_type=F32)
        return d if acc is None else acc + d

    acc, pending = None, None
    for c0 in range(0, D_FF, FF_PIECE):
        g = jnp.dot(h, wg_ref[:, c0:c0 + FF_PIECE], preferred_element_type=F32)
        u = jnp.dot(h, wu_ref[:, c0:c0 + FF_PIECE], preferred_element_type=F32)
        if pending is not None:
            acc = down(pending, acc)
        pending = (g, u, c0)
        yield g
    yield down(pending, acc)


def _ffn(x, nw, wg_ref, wu_ref, wd_ref):
    h = _rms(x, nw).astype(BF16)
    for acc in _ffn_pieces(h, wg_ref, wu_ref, wd_ref):
        pass
    return x + 0.5 * acc


def _ffn_in_body(x_ref, n1_ref, wg_ref, wu_ref, wd_ref, nm_ref, wina_ref, winb_ref,
                 x1_ref, qkv_ref, z_ref, xr_ref, gate_ref, ba_ref):
    x1 = _ffn(x_ref[...], n1_ref[...], wg_ref, wu_ref, wd_ref)
    x1_ref[...] = x1
    hm = _rms(x1, nm_ref[...]).astype(BF16)
    pa = jnp.dot(hm, wina_ref[...], preferred_element_type=F32)
    pb = jnp.dot(hm, winb_ref[...], preferred_element_type=F32)
    qkv_ref[...] = pa[:, :D_QKV]
    z_ref[...] = pa[:, D_QKV:]
    xr_ref[...] = pb[:, :D_R]
    gate_ref[...] = pb[:, D_R:2 * D_R]
    ba_ref[...] = pb[:, 2 * D_R:]


def _out_ffn_body(x_ref, o_ref, y_ref, wo1_ref, wo2_ref, n2_ref, wg_ref, wu_ref, wd_ref, nf_ref, *rest,
                  final, n_cast):
    src_refs, out_ref, dst_refs = rest[:n_cast], rest[n_cast], rest[n_cast + 1:]
    x = x_ref[...] + (_dot(o_ref[...], wo1_ref[...]) + _dot(y_ref[...], wo2_ref[...]))
    x2 = _ffn(x, n2_ref[...], wg_ref, wu_ref, wd_ref)
    if final:
        x2 = _rms(x2, nf_ref[...])
    out_ref[...] = x2
    for s_ref, d_ref in zip(src_refs, dst_refs):
        d_ref[...] = s_ref[...].astype(BF16)


def _resident(shape):
    nd = len(shape)
    return pl.BlockSpec(shape, lambda *_: (0,) * nd, pipeline_mode=pl.Buffered(1))


def _row_block(shape, first):
    return pl.BlockSpec(shape, lambda *_: (first, 0), pipeline_mode=pl.Buffered(1))


def _rows(tm, width):
    return pl.BlockSpec((tm, width), lambda i: (i, 0))


def _ffn_in(x, lw):
    r = x.shape[0]
    tm = min(ROW_TILE, r)
    widths = (D_MODEL, D_QKV, D_A, D_R, D_R, D_BA)
    return pl.pallas_call(
        _ffn_in_body,
        grid=(pl.cdiv(r, tm),),
        in_specs=[_rows(tm, D_MODEL), _resident((1, D_MODEL)),
                  _resident((D_MODEL, D_FF)), _resident((D_MODEL, D_FF)), _resident((D_FF, D_MODEL)),
                  _resident((1, D_MODEL)), _resident((D_MODEL, D_PROJ_A)), _resident((D_MODEL, D_PROJ_B))],
        out_specs=[_rows(tm, w) for w in widths],
        out_shape=[jax.ShapeDtypeStruct((r, w), F32) for w in widths],
        compiler_params=pltpu.CompilerParams(dimension_semantics=("parallel",),
                                             vmem_limit_bytes=VMEM_LIMIT),
        name="ffn_in",
    )(x, lw["n1"], lw["f1g"], lw["f1u"], lw["f1d"], lw["nm"], lw["w_in_a"], lw["w_in_b"])


def _out_ffn(x, o, y, lw, nf, final, cast=()):
    r = x.shape[0]
    tm = min(ROW_TILE, r)
    steps = pl.cdiv(r, tm)
    cast_in, cast_out, cast_shape = [], [], []
    for arr, layer, cols in cast:
        rows = arr.shape[1]
        cr = BF16_ROWS * pl.cdiv(pl.cdiv(rows, steps), BF16_ROWS)
        last = pl.cdiv(rows, cr) - 1
        cast_in.append(pl.BlockSpec((None, cr, cols), lambda i, layer=layer, last=last: (layer, jnp.minimum(i, last), 0)))
        cast_out.append(pl.BlockSpec((cr, cols), lambda i, last=last: (jnp.minimum(i, last), 0)))
        cast_shape.append(jax.ShapeDtypeStruct((rows, cols), BF16))
    res = pl.pallas_call(
        functools.partial(_out_ffn_body, final=final, n_cast=len(cast)),
        grid=(steps,),
        in_specs=[_rows(tm, D_MODEL), _rows(tm, D_A), _rows(tm, D_R),
                  _row_block((D_A, D_MODEL), 0), _row_block((D_R, D_MODEL), 1), _resident((1, D_MODEL)),
                  _resident((D_MODEL, D_FF)), _resident((D_MODEL, D_FF)), _resident((D_FF, D_MODEL)),
                  _resident((1, D_MODEL))] + cast_in,
        out_specs=[_rows(tm, D_MODEL)] + cast_out,
        out_shape=[jax.ShapeDtypeStruct((r, D_MODEL), F32)] + cast_shape,
        compiler_params=pltpu.CompilerParams(dimension_semantics=("arbitrary",),
                                             vmem_limit_bytes=VMEM_LIMIT),
        name="out_ffn",
    )(x, o, y, lw["w_out"], lw["w_out"], lw["n2"], lw["f2g"], lw["f2u"], lw["f2d"], nf, *[c[0] for c in cast])
    return res[0], res[1:]


class _Io:
    def __init__(self, **fns):
        self.after = lambda x: x
        self.__dict__.update(fns)


def _delta_stream(io, cw_ref, alog_ref, dtb_ref, on_ref, *, tiles, seg):
    c = seg
    nb = tiles[0][1]
    r = nb * c
    levels = int(math.log2(c))
    assert 2 ** levels == c
    row = lax.broadcasted_iota(jnp.int32, (r, r), 0)
    col = lax.broadcasted_iota(jnp.int32, (r, r), 1)
    same = lax.shift_right_logical(row, levels) == lax.shift_right_logical(col, levels)
    causal = same & (row >= col)
    strict = same & (row > col)
    tri = causal.astype(BF16)
    ones_seg = same.astype(BF16)
    eye = (row == col).astype(F32)
    sel_r = lax.broadcasted_iota(jnp.int32, (SUBLANES, D_BA), 0)
    sel_c = lax.broadcasted_iota(jnp.int32, (SUBLANES, D_BA), 1)
    sel = ((sel_c == sel_r + H_A) & (sel_r < H_A)).astype(BF16)

    def conv_silu(tile, col0):
        win = io.ext(tile, col0, DK)
        acc = None
        for j in range(CONV_W):
            k = CONV_W - 1 - j
            x = win if k == 0 else pltpu.roll(win, k, 1)
            term = x[:, TAIL:, :].reshape(r, DK) * io.after(cw_ref[j:j + 1, col0:col0 + DK])
            acc = term if acc is None else acc + term
        return _silu(acc)

    def exact_dot(lhs01, x, nt=False):
        out = None
        for p in _split3(x):
            if nt:
                d = lax.dot_general(lhs01, p, (((1,), (1,)), ((), ())), preferred_element_type=F32)
            else:
                d = jnp.dot(lhs01, p, preferred_element_type=F32)
            out = d if out is None else out + d
        return out

    items = [(j, h) for j in range(len(tiles)) for h in range(H_A)]

    gcum, gtot, grow, eg_all, beta_all = [], [], [], [], []
    for tile in tiles:
        ba = io.ba(tile)
        g_all = -jnp.exp(io.after(alog_ref[...])) * _softplus(ba + dtb_ref[...])
        beta_all.append(_sigmoid(ba))
        gcum.append(exact_dot(tri, g_all))
        gtot.append(exact_dot(ones_seg, g_all) if nb > 1 else gcum[-1][r - 1:r, :])
    yield
    for j in range(len(tiles)):
        grow.append(exact_dot(sel, gcum[j], nt=True))
        eg_all.append(jnp.exp(gcum[j]))
    yield

    qg, kn, kb, kdec, decay, rhs = {}, {}, {}, {}, {}, {}
    for (j, h) in items:
        tile = tiles[j]
        qh = conv_silu(tile, h * DK)
        kh = conv_silu(tile, D_A + h * DK)
        vh = conv_silu(tile, 2 * D_A + h * DV)
        qn = qh * lax.rsqrt(jnp.sum(qh * qh, axis=-1, keepdims=True) + EPS) * (DK ** -0.5)
        k_n = kh * lax.rsqrt(jnp.sum(kh * kh, axis=-1, keepdims=True) + EPS)
        beta = beta_all[j][:, h:h + 1]
        gcol = gcum[j][:, H_A + h:H_A + h + 1]
        egcol = eg_all[j][:, H_A + h:H_A + h + 1]
        glast = gtot[j][:, H_A + h:H_A + h + 1]
        dmat = gcol - grow[j][h:h + 1, :]
        decay[j, h] = jnp.where(causal, jnp.exp(jnp.where(causal, dmat, 0.0)), 0.0)
        kn[j, h] = k_n
        kb[j, h] = k_n * beta
        qg[j, h] = (qn, qn * egcol)
        kdec[j, h] = k_n * jnp.exp(glast - gcol)
        rhs[j, h] = jnp.concatenate([vh * beta, kb[j, h] * egcol], axis=-1)
        if h == H_A - 1:
            yield

    kq = {it: _dot_nt(jnp.concatenate([kb[it], qg[it][0]], axis=0), kn[it]) for it in items}
    lmat = {it: jnp.where(strict, kq[it][:r] * decay[it], 0.0) for it in items}
    attn = {it: kq[it][r:] * decay[it] for it in items}
    yield

    tinv = {it: eye - lmat[it] for it in items}
    m = {it: _dot(lmat[it], lmat[it]) for it in items}
    yield
    for _ in range(levels - 2):
        tinv = {it: tinv[it] + _dot(tinv[it], m[it]) for it in items}
        m = {it: _dot(m[it], m[it]) for it in items}
        yield
    tinv = {it: tinv[it] + _dot(tinv[it], m[it]) for it in items}
    yield
    sol = {it: _dot(tinv[it], rhs[it]) for it in items}
    yield

    state = {}

    def get_state(b, h):
        if (b, h) not in state:
            state[b, h] = io.get_state(b, h)
        return state[b, h]

    for j, tile in enumerate(tiles):
        b0 = tile[0]
        segs = [(s, slice(s * c, (s + 1) * c)) for s in range(nb)]
        ws = {}
        for h in range(H_A):
            for s, rs in segs:
                lhs = jnp.concatenate([sol[j, h][rs, DV:], qg[j, h][1][rs]], axis=0)
                ws[h, s] = _dot(lhs, get_state(b0 + s, h))
        yield
        v_new, av = {}, {}
        for h in range(H_A):
            w_s = jnp.concatenate([ws[h, s][:c] for s, _ in segs], axis=0) if nb > 1 else ws[h, 0][:c]
            v_new[h] = sol[j, h][:, :DV] - w_s
        for h in range(H_A):
            av[h] = _dot(attn[j, h], v_new[h])
        for h in range(H_A):
            for s, rs in segs:
                g0 = rs.start if nb > 1 else 0
                scale = jnp.exp(gtot[j][g0:g0 + 1, H_A + h:H_A + h + 1])
                state[b0 + s, h] = state[b0 + s, h] * scale + _dot_tn(kdec[j, h][rs], v_new[h][rs])
        for h in range(H_A):
            q_s = jnp.concatenate([ws[h, s][c:] for s, _ in segs], axis=0) if nb > 1 else ws[h, 0][c:]
            on = _rms(q_s + av[h], io.after(on_ref[...]))
            io.put_o(tile, h, on * _silu(io.z(tile, h * DV)))
        if j == len(tiles) - 1:
            for (b, h), val in state.items():
                io.put_state(b, h, val)
        yield


def _delta_body(qkv_ref, z_ref, ba_ref, cs_ref, s0_ref, cw_ref, alog_ref, dtb_ref, on_ref, *rest,
                tiles, seg, tt, n_t, fill_layers):
    if fill_layers:
        o_ref, sall_ref, csnew_ref, ext_ref = rest
        snew_ref = sall_ref.at[0]
    else:
        _, o_ref, snew_ref, csnew_ref, ext_ref = rest
    c = seg
    nb = tiles[0][1]
    r = nb * c
    t = pl.program_id(1)

    @pl.when(t == 0)
    def _():
        ext_ref[:, TAIL - (CONV_W - 1):TAIL, :] = cs_ref[...]
        snew_ref[...] = s0_ref[...]

    ext_ref[:, TAIL:TAIL + tt, :] = qkv_ref[...]

    def rows(ref, tile, off, col0, width):
        b0, _, t0 = tile
        return ref[b0:b0 + nb, off + t0:off + t0 + c, col0:col0 + width].reshape(r, width)

    def put_o(tile, h, val):
        b0, _, t0 = tile
        o_ref[b0:b0 + nb, t0:t0 + c, h * DV:(h + 1) * DV] = val.reshape(nb, c, DV).astype(BF16)

    def put_state(b, h, val):
        snew_ref[b, h] = val

    def ext(tile, col0, w):
        b0, _, t0 = tile
        return ext_ref[b0:b0 + nb, t0:t0 + TAIL + c, col0:col0 + w]

    io = _Io(ext=ext, ba=lambda tile: rows(ba_ref, tile, 0, 0, D_BA),
             z=lambda tile, col0: rows(z_ref, tile, 0, col0, DV),
             put_o=put_o, get_state=lambda b, h: snew_ref[b, h], put_state=put_state)
    for _ in _delta_stream(io, cw_ref, alog_ref, dtb_ref, on_ref, tiles=tiles, seg=seg):
        pass

    tail = ext_ref[:, tt:tt + TAIL, :]
    ext_ref[:, 0:TAIL, :] = tail

    @pl.when(t == n_t - 1)
    def _():
        csnew_ref[...] = ext_ref[:, TAIL - (CONV_W - 1):TAIL, :]
        if fill_layers:
            for d in range(1, fill_layers):
                sall_ref[d] = sall_ref[0]


def _delta(qkv, z, ba, cs, s0, lw, l, s_all, *, seg, tt, bb, tiles):
    b, t, _ = qkv.shape
    depth = s0.shape[0]
    n_t = t // tt
    assert n_t * tt == t and b % bb == 0 and (l == 0) == (s_all is None)
    st = lambda i, j: (l, i, 0, 0, 0)
    ct = lambda i, j: (l, i, 0, 0)
    tok = lambda w: pl.BlockSpec((bb, tt, w), lambda i, j: (i, j, 0))
    const = lambda shape: pl.BlockSpec(shape, lambda i, j: (0,) * len(shape))
    in_specs = [tok(D_QKV), tok(D_A), tok(D_BA),
                pl.BlockSpec((None, bb, CONV_W - 1, D_QKV), ct),
                pl.BlockSpec((None, bb, H_A, DK, DV), st),
                const((CONV_W, D_QKV)), const((1, D_BA)), const((1, D_BA)), const((1, DV))]
    operands = [qkv, z, ba, cs, s0, lw["cqkv"], lw["alog"], lw["dtb"], lw["onorm"]]
    if s_all is None:
        s_spec = pl.BlockSpec((depth, bb, H_A, DK, DV), lambda i, j: (0, i, 0, 0, 0))
        aliases = {}
    else:
        s_spec = pl.BlockSpec((None, bb, H_A, DK, DV), st)
        in_specs.append(pl.BlockSpec(memory_space=pl.ANY))
        operands.append(s_all)
        aliases = {len(operands) - 1: 1}
    return pl.pallas_call(
        functools.partial(_delta_body, tiles=tiles, seg=seg, tt=tt, n_t=n_t,
                          fill_layers=depth if s_all is None else 0),
        grid=(b // bb, n_t),
        in_specs=in_specs,
        out_specs=[tok(D_A), s_spec,
                   pl.BlockSpec((bb, CONV_W - 1, D_QKV), lambda i, j: (i, 0, 0))],
        out_shape=[jax.ShapeDtypeStruct((b, t, D_A), BF16),
                   jax.ShapeDtypeStruct((depth, b, H_A, DK, DV), F32),
                   jax.ShapeDtypeStruct((b, CONV_W - 1, D_QKV), F32)],
        scratch_shapes=[pltpu.VMEM((bb, tt + TAIL, D_QKV), F32)],
        input_output_aliases=aliases,
        compiler_params=pltpu.CompilerParams(dimension_semantics=("arbitrary", "arbitrary"),
                                             vmem_limit_bytes=VMEM_LIMIT),
        name="delta",
    )(*operands)


def _lru_stream(io, cw_ref, cb_ref, wg_ref, bg_ref, lam_ref, *, tt, bb):
    win = io.ext()
    xc = None
    for j in range(CONV_W):
        k = CONV_W - 1 - j
        x = win if k == 0 else pltpu.roll(win, k, 1)
        term = x[:, TAIL:, :] * io.after(cw_ref[j:j + 1, :])
        xc = term if xc is None else xc + term
    xc = xc + cb_ref[...]
    n = bb * tt
    xc = xc.reshape(n, D_R)
    yield
    gates = jnp.dot(xc.astype(BF16), wg_ref[...], preferred_element_type=F32) + bg_ref[...]
    r = _sigmoid(gates[:, :D_R])
    i = _sigmoid(gates[:, D_R:])
    log_a = -C_RG * r * _softplus(-io.after(lam_ref[...]))
    a = jnp.exp(log_a)
    m2 = -jnp.tanh(log_a) * (a * a + 1.0)
    u = jnp.where(m2 > 0.0, m2 * lax.rsqrt(m2), 0.0) * i * xc
    yield

    ng = n // SUBLANES
    a = a.reshape(ng, SUBLANES, D_R)
    u = u.reshape(ng, SUBLANES, D_R)
    tpos = lax.broadcasted_iota(jnp.int32, (ng, SUBLANES, D_R), 1)
    s = 1
    while s < SUBLANES:
        a_sh = pltpu.roll(a, s, 1)
        u_sh = pltpu.roll(u, s, 1)
        m = tpos >= s
        u = jnp.where(m, u + a * u_sh, u)
        a = jnp.where(m, a * a_sh, a)
        s *= 2
    yield
    gpb = tt // SUBLANES
    if gpb == 1:
        hs = u + a * io.h0()
        h_last = hs[:, SUBLANES - 1:SUBLANES, :]
    else:
        assert bb == 1
        h_in = io.h0()[0]
        groups = []
        for g in range(gpb):
            hg = u[g] + a[g] * h_in
            groups.append(hg)
            h_in = hg[SUBLANES - 1:SUBLANES, :]
        hs = jnp.concatenate(groups, axis=0)
        h_last = h_in.reshape(1, 1, D_R)
    y = hs.reshape(n, D_R) * jax.nn.gelu(io.gate(), approximate=True)
    io.put_y(y.reshape(bb, tt, D_R))
    io.put_h(h_last)
    yield


def _lru_body(xr_ref, gate_ref, cs_ref, h0_ref, cw_ref, cb_ref, wg_ref, bg_ref, lam_ref,
              y_ref, hnew_ref, csnew_ref, ext_ref, *, tt, bb, n_t):
    t = pl.program_id(1)

    @pl.when(t == 0)
    def _():
        ext_ref[:, TAIL - (CONV_W - 1):TAIL, :] = cs_ref[...]
        hnew_ref[...] = h0_ref[...]

    ext_ref[:, TAIL:TAIL + tt, :] = xr_ref[...]

    def put_y(val):
        y_ref[...] = val.astype(BF16)

    def put_h(val):
        hnew_ref[...] = val

    io = _Io(ext=lambda: ext_ref[...], gate=lambda: gate_ref[...].reshape(bb * tt, D_R),
             h0=lambda: hnew_ref[...], put_y=put_y, put_h=put_h)
    for _ in _lru_stream(io, cw_ref, cb_ref, wg_ref, bg_ref, lam_ref, tt=tt, bb=bb):
        pass

    tail = ext_ref[:, tt:tt + TAIL, :]
    ext_ref[:, 0:TAIL, :] = tail

    @pl.when(t == n_t - 1)
    def _():
        csnew_ref[...] = ext_ref[:, TAIL - (CONV_W - 1):TAIL, :]


def _lru(xr, gate, cs, h0, lw, l, *, tt, bb):
    b, t, _ = xr.shape
    n_t = t // tt
    assert n_t * tt == t and b % bb == 0 and tt % SUBLANES == 0
    ct = lambda i, j: (l, i, 0, 0)
    tok = pl.BlockSpec((bb, tt, D_R), lambda i, j: (i, j, 0))
    const = lambda shape: pl.BlockSpec(shape, lambda i, j: (0,) * len(shape))
    return pl.pallas_call(
        functools.partial(_lru_body, tt=tt, bb=bb, n_t=n_t),
        grid=(b // bb, n_t),
        in_specs=[tok, tok,
                  pl.BlockSpec((None, bb, CONV_W - 1, D_R), ct),
                  pl.BlockSpec((None, bb, 1, D_R), ct),
                  const((CONV_W, D_R)), const((1, D_R)), const((D_R, 2 * D_R)), const((1, 2 * D_R)),
                  const((1, D_R))],
        out_specs=[tok,
                   pl.BlockSpec((bb, 1, D_R), lambda i, j: (i, 0, 0)),
                   pl.BlockSpec((bb, CONV_W - 1, D_R), lambda i, j: (i, 0, 0))],
        out_shape=[jax.ShapeDtypeStruct((b, t, D_R), BF16),
                   jax.ShapeDtypeStruct((b, 1, D_R), F32),
                   jax.ShapeDtypeStruct((b, CONV_W - 1, D_R), F32)],
        scratch_shapes=[pltpu.VMEM((bb, tt + TAIL, D_R), F32)],
        compiler_params=pltpu.CompilerParams(dimension_semantics=("arbitrary", "arbitrary"),
                                             vmem_limit_bytes=VMEM_LIMIT),
        name="lru",
    )(xr, gate, cs, h0, lw["clw"], lw["clb"], lw["wgate"], lw["bgate"], lw["lam"])


def _interleave(streams, plan, lead, set_dep):
    live = dict(streams)

    def step(key):
        if key not in live:
            return
        try:
            val = next(live[key])
            if key == lead:
                set_dep(val)
        except StopIteration:
            del live[key]

    for key in plan:
        step(key)
    while live:
        for key in list(live):
            step(key)


def _in_mix_body(flag_ref, x_ref, n1_ref, wg_ref, wu_ref, wd_ref, nm_ref, wina_ref, winb_ref,
                 dcs_ref, ds0_ref, cw_ref, alog_ref, dtb_ref, on_ref,
                 lcs_ref, lh0_ref, lcw_ref, lcb_ref, lwg_ref, lbg_ref, lam_ref,
                 x1_ref, o_ref, y_ref, snew_ref, dcsnew_ref, hnew_ref, lcsnew_ref,
                 qkv_scr, z_scr, ba_scr, xr_scr, gate_scr, dtail_scr, ltail_scr, s_scr, h_scr,
                 *, tiles_per_seq):
    tt = MIX_TILE
    s = pl.program_id(0)
    slot_a = lax.rem(s, 2)
    slot_b = 1 - slot_a
    first = lax.rem(s + tiles_per_seq - 1, tiles_per_seq) == 0
    tails = slice(TAIL - (CONV_W - 1), TAIL)
    keep = flag_ref[0] == 1
    dep = [None]

    def after(x):
        if dep[0] is None:
            return x
        reps = x.shape[-1] // DK
        d = dep[0] if reps == 1 else jnp.concatenate([dep[0]] * reps, axis=-1)
        return jnp.where(keep, x, d)

    def set_dep(val):
        dep[0] = val

    @pl.when(s == 0)
    def _():
        for ref in (qkv_scr, z_scr, ba_scr, xr_scr, gate_scr, dtail_scr, ltail_scr, s_scr, h_scr):
            ref[...] = jnp.zeros(ref.shape, F32)

    def stream_a():
        x = x_ref[...]
        h = _rms(x, n1_ref[...]).astype(BF16)
        pieces = _ffn_pieces(h, wg_ref, wu_ref, wd_ref)
        for _ in range(D_FF // FF_PIECE):
            yield next(pieces)[0:1, 0:DK]
        acc = next(pieces)
        x1 = x + 0.5 * acc
        x1_ref[...] = x1
        hm = _rms(x1, nm_ref[...]).astype(BF16)
        yield acc[0:1, 0:DK]
        dsts = ((qkv_scr, TAIL, wina_ref, 0, D_QKV), (z_scr, 0, wina_ref, D_QKV, D_A),
                (xr_scr, TAIL, winb_ref, 0, D_R), (gate_scr, 0, winb_ref, D_R, D_R),
                (ba_scr, 0, winb_ref, 2 * D_R, D_BA))
        for dst, row0, w_ref, col0, width in dsts:
            for d0 in range(0, width, PROJ_PIECE):
                w = min(PROJ_PIECE, width - d0)
                p = jnp.dot(hm, w_ref[:, col0 + d0:col0 + d0 + w], preferred_element_type=F32)
                dst[slot_a, row0:row0 + tt, d0:d0 + w] = p
                yield p[0:1, 0:DK]

    qkv_scr[slot_b, tails, :] = jnp.where(first, dcs_ref[0], dtail_scr[tails, :])
    xr_scr[slot_b, tails, :] = jnp.where(first, lcs_ref[0], ltail_scr[tails, :])

    def put_o(tile, h, val):
        o_ref[tile[2]:tile[2] + CHUNK, h * DV:(h + 1) * DV] = val.astype(BF16)

    def put_state(b, h, val):
        s_scr[h] = val
        snew_ref[0, h] = val

    def put_y(val):
        y_ref[...] = val.reshape(tt, D_R).astype(BF16)

    def put_h(val):
        h_scr[...] = val
        hnew_ref[...] = val

    dio = _Io(ext=lambda tile, col0, w: qkv_scr[slot_b, tile[2]:tile[2] + TAIL + CHUNK, col0:col0 + w][None],
              ba=lambda tile: ba_scr[slot_b, tile[2]:tile[2] + CHUNK, :],
              z=lambda tile, col0: z_scr[slot_b, tile[2]:tile[2] + CHUNK, col0:col0 + DV],
              put_o=put_o, get_state=lambda b, h: jnp.where(first, ds0_ref[0, h], s_scr[h]),
              put_state=put_state, after=after)
    lio = _Io(ext=lambda: xr_scr[slot_b][None],
              gate=lambda: gate_scr[slot_b], h0=lambda: jnp.where(first, lh0_ref[...], h_scr[...]),
              put_y=put_y, put_h=put_h, after=after)
    tiles = tuple((0, 1, j * CHUNK) for j in range(tt // CHUNK))
    _interleave({"a": stream_a(),
                 "d": _delta_stream(dio, cw_ref, alog_ref, dtb_ref, on_ref, tiles=tiles, seg=CHUNK),
                 "l": _lru_stream(lio, lcw_ref, lcb_ref, lwg_ref, lbg_ref, lam_ref, tt=tt, bb=1)},
                MIX_PLAN, "a", set_dep)

    dtail_scr[...] = qkv_scr[slot_b, tt:tt + TAIL, :]
    ltail_scr[...] = xr_scr[slot_b, tt:tt + TAIL, :]
    dcsnew_ref[0] = qkv_scr[slot_b, tt + TAIL - (CONV_W - 1):tt + TAIL, :]
    lcsnew_ref[0] = xr_scr[slot_b, tt + TAIL - (CONV_W - 1):tt + TAIL, :]


def _in_mix(x, sdc, sd, slc, sl, lw, l, *, batch):
    r = x.shape[0]
    tt = MIX_TILE
    n_tiles = r // tt
    tiles_per_seq = n_tiles // batch
    assert n_tiles * tt == r and tiles_per_seq * batch == n_tiles
    cur = lambda s: (jnp.minimum(s, n_tiles - 1), 0)
    prev = lambda s: (jnp.maximum(s - 1, 0), 0)
    seq3 = lambda s: (jnp.maximum(s - 1, 0) // tiles_per_seq, 0, 0)
    seq4 = lambda s: (jnp.maximum(s - 1, 0) // tiles_per_seq, 0, 0, 0)
    layer = lambda shape: pl.BlockSpec((None,) + shape, lambda s: (l,) + (0,) * len(shape),
                                       pipeline_mode=pl.Buffered(1))
    return pl.pallas_call(
        functools.partial(_in_mix_body, tiles_per_seq=tiles_per_seq),
        grid=(n_tiles + 1,),
        in_specs=[pl.BlockSpec(memory_space=pltpu.SMEM),
                  pl.BlockSpec((tt, D_MODEL), cur), _resident((1, D_MODEL)),
                  _resident((D_MODEL, D_FF)), _resident((D_MODEL, D_FF)), _resident((D_FF, D_MODEL)),
                  _resident((1, D_MODEL)), _resident((D_MODEL, D_PROJ_A)), _resident((D_MODEL, D_PROJ_B)),
                  layer((1, CONV_W - 1, D_QKV)), layer((1, H_A, DK, DV)),
                  _resident((CONV_W, D_QKV)), _resident((1, D_BA)), _resident((1, D_BA)), _resident((1, DV)),
                  layer((1, CONV_W - 1, D_R)), layer((1, 1, D_R)),
                  _resident((CONV_W, D_R)), _resident((1, D_R)), _resident((D_R, 2 * D_R)),
                  _resident((1, 2 * D_R)), _resident((1, D_R))],
        out_specs=[pl.BlockSpec((tt, D_MODEL), cur), pl.BlockSpec((tt, D_A), prev), pl.BlockSpec((tt, D_R), prev),
                   pl.BlockSpec((1, H_A, DK, DV), seq4), pl.BlockSpec((1, CONV_W - 1, D_QKV), seq3),
                   pl.BlockSpec((1, 1, D_R), seq3), pl.BlockSpec((1, CONV_W - 1, D_R), seq3)],
        out_shape=[jax.ShapeDtypeStruct((r, D_MODEL), F32), jax.ShapeDtypeStruct((r, D_A), BF16),
                   jax.ShapeDtypeStruct((r, D_R), BF16),
                   jax.ShapeDtypeStruct((batch, H_A, DK, DV), F32),
                   jax.ShapeDtypeStruct((batch, CONV_W - 1, D_QKV), F32),
                   jax.ShapeDtypeStruct((batch, 1, D_R), F32),
                   jax.ShapeDtypeStruct((batch, CONV_W - 1, D_R), F32)],
        scratch_shapes=[pltpu.VMEM((2, TAIL + tt, D_QKV), F32), pltpu.VMEM((2, tt, D_A), F32),
                        pltpu.VMEM((2, tt, D_BA), F32), pltpu.VMEM((2, TAIL + tt, D_R), F32),
                        pltpu.VMEM((2, tt, D_R), F32), pltpu.VMEM((TAIL, D_QKV), F32),
                        pltpu.VMEM((TAIL, D_R), F32), pltpu.VMEM((H_A, DK, DV), F32),
                        pltpu.VMEM((1, 1, D_R), F32)],
        compiler_params=pltpu.CompilerParams(dimension_semantics=("arbitrary",),
                                             vmem_limit_bytes=VMEM_LIMIT),
        name="in_mix",
    )(jnp.ones((1,), jnp.int32), x, lw["n1"], lw["f1g"], lw["f1u"], lw["f1d"], lw["nm"], lw["w_in_a"], lw["w_in_b"],
      sdc, sd, lw["cqkv"], lw["alog"], lw["dtb"], lw["onorm"],
      slc, sl, lw["clw"], lw["clb"], lw["wgate"], lw["bgate"], lw["lam"])


BIG_WEIGHTS = ("f1g", "f1u", "f1d", "w_out", "f2g", "f2u", "f2d")


def _prep_layer(l, w, big):
    row = lambda v: v.reshape(1, -1).astype(F32)
    lane_pad = lambda v: jnp.zeros((1, D_BA), F32).at[0, H_A:2 * H_A].set(v)
    eye = jnp.eye(NB_R, dtype=F32)
    bd = lambda wb: jnp.einsum("ncd,nm->ncmd", wb, eye).reshape(D_R, D_R)
    w_in = w["w_in"][l]
    c4 = D_PROJ_A + 2 * H_A
    pad = jnp.zeros((D_MODEL, D_BA - 2 * H_A), w_in.dtype)
    w_in_b = jnp.concatenate([w_in[:, c4:], w_in[:, D_PROJ_A:c4], pad], axis=-1).astype(BF16)
    return dict(
        n1=row(w["norm_ffn1"][l]), f1g=big["f1g"], f1u=big["f1u"], f1d=big["f1d"],
        nm=row(w["norm_mix"][l]), w_in_a=w_in[:, :D_PROJ_A].astype(BF16), w_in_b=w_in_b,
        cqkv=w["conv_qkv"][l], alog=lane_pad(w["a_log"][l]), dtb=lane_pad(w["dt_bias"][l]),
        onorm=row(w["norm_delta_out"][l]),
        clw=w["conv_lru_w"][l], clb=row(w["conv_lru_b"][l]),
        wgate=jnp.concatenate([bd(w["w_rgate"][l]), bd(w["w_igate"][l])], axis=1).astype(BF16),
        bgate=jnp.concatenate([row(w["b_rgate"][l]), row(w["b_igate"][l])], axis=1),
        lam=row(w["lru_lambda"][l]),
        w_out=big["w_out"], n2=row(w["norm_ffn2"][l]), f2g=big["f2g"], f2u=big["f2u"], f2d=big["f2d"],
    )


def _group_layers(x, states_at, layers, nf, out, *, seg=None, delta_tt=None, delta_bb=None, tiles=None,
                  lru_tt=None, lru_bb=None, fused=False, states_only=False, cast_next=None):
    b, t, _ = x.shape
    xf = x.reshape(b * t, D_MODEL)
    s_all = None
    n_layers = len(layers)
    for l in range(n_layers):
        lw = layers[l]
        sd, sdc, sl, slc = states_at(l)
        if fused:
            x1, o, y, s_new, cs_new, h_new, lcs_new = _in_mix(xf, sdc, sd, slc, sl, lw, l, batch=b)
            out["nd"].append(s_new)
        else:
            x1, qkv, z, xr, gate, ba = _ffn_in(xf, lw)
            r3 = lambda v: v.reshape(b, t, v.shape[-1])
            o, s_all, cs_new = _delta(r3(qkv), r3(z), r3(ba), sdc, sd, lw, l, s_all,
                                      seg=seg, tt=delta_tt, bb=delta_bb, tiles=tiles)
            y, h_new, lcs_new = _lru(r3(xr), r3(gate), slc, sl, lw, l, tt=lru_tt, bb=lru_bb)
            out["s_all"] = s_all
        last = l == n_layers - 1
        if not (last and states_only):
            cast = cast_next(l) if cast_next is not None and not last else ()
            xf, copies = _out_ffn(x1, o.reshape(b * t, D_A), y.reshape(b * t, D_R), lw, nf, last, cast)
            if cast:
                out["cast"][l + 1] = dict(zip(BIG_WEIGHTS, copies))
        out["ndc"].append(cs_new)
        out["nl"].append(h_new)
        out["nlc"].append(lcs_new)
        if last and not states_only:
            out["y"] = xf.reshape(b, t, D_MODEL)
        yield


def kernel(x_prompt, x_sample, state_delta, state_delta_conv, state_lru, state_lru_conv, meta_tokens, norm_ffn1, w_ffn1_gate, w_ffn1_up, w_ffn1_down, norm_mix, w_in, conv_qkv, a_log, dt_bias, norm_delta_out, conv_lru_w, conv_lru_b, w_rgate, b_rgate, w_igate, b_igate, lru_lambda, w_out, norm_ffn2, w_ffn2_gate, w_ffn2_up, w_ffn2_down, norm_final):
    w = dict(norm_ffn1=norm_ffn1, w_ffn1_gate=w_ffn1_gate, w_ffn1_up=w_ffn1_up, w_ffn1_down=w_ffn1_down,
             norm_mix=norm_mix, w_in=w_in, conv_qkv=conv_qkv, a_log=a_log, dt_bias=dt_bias,
             norm_delta_out=norm_delta_out, conv_lru_w=conv_lru_w, conv_lru_b=conv_lru_b,
             w_rgate=w_rgate, b_rgate=b_rgate, w_igate=w_igate, b_igate=b_igate, lru_lambda=lru_lambda,
             w_out=w_out, norm_ffn2=norm_ffn2, w_ffn2_gate=w_ffn2_gate, w_ffn2_up=w_ffn2_up,
             w_ffn2_down=w_ffn2_down)
    depth = norm_ffn1.shape[0]
    big_f32 = dict(f1g=(w_ffn1_gate, D_FF), f1u=(w_ffn1_up, D_FF), f1d=(w_ffn1_down, D_MODEL),
                   w_out=(w_out, D_MODEL),
                   f2g=(w_ffn2_gate, D_FF), f2u=(w_ffn2_up, D_FF), f2d=(w_ffn2_down, D_MODEL))
    layers = [None] * depth
    layers[0] = _prep_layer(0, w, {k: big_f32[k][0][0, :, :big_f32[k][1]].astype(BF16) for k in BIG_WEIGHTS})
    cast_next = lambda l: tuple((big_f32[k][0], l + 1, big_f32[k][1]) for k in BIG_WEIGHTS)
    nf = norm_final.reshape(1, D_MODEL).astype(F32)
    bp, seq, _ = x_prompt.shape
    bs, dseq, _ = x_sample.shape
    seg_b = CHUNK // dseq
    new_out = lambda: dict(nd=[], ndc=[], nl=[], nlc=[], cast={}, s_all=None, y=None)
    stack = jnp.stack

    zeros = lambda *s: jnp.zeros((depth,) + s, F32)
    m_states = (zeros(1, H_A, DK, DV), zeros(1, CONV_W - 1, D_QKV), zeros(1, 1, D_R), zeros(1, CONV_W - 1, D_R))
    m_out, p_out, s_out = new_out(), new_out(), new_out()
    meta = _group_layers(meta_tokens.astype(F32)[None], lambda l: m_states, layers, nf, m_out,
                         seg=N_META, delta_tt=N_META, delta_bb=1, tiles=((0, 1, 0),), lru_tt=N_META, lru_bb=1,
                         states_only=True)
    after_meta = lambda l: (m_out["s_all"], stack(m_out["ndc"]), stack(m_out["nl"]), stack(m_out["nlc"]))
    prompt = _group_layers(x_prompt, after_meta, layers, nf, p_out, fused=True, cast_next=cast_next)
    s_states = (state_delta, state_delta_conv, state_lru.reshape(depth, bs, 1, D_R), state_lru_conv)
    sample = _group_layers(x_sample, lambda l: s_states, layers, nf, s_out,
                           seg=dseq, delta_tt=dseq, delta_bb=SAMPLE_TILES_PER_STEP * seg_b,
                           tiles=tuple((j * seg_b, seg_b, 0) for j in range(SAMPLE_TILES_PER_STEP)),
                           lru_tt=dseq, lru_bb=2 * seg_b)
    for l in range(depth):
        next(meta)
        next(prompt)
        if l + 1 < depth:
            layers[l + 1] = _prep_layer(l + 1, w, p_out["cast"][l + 1])
        next(sample)

    return (p_out["y"], s_out["y"], stack(p_out["nd"]), stack(p_out["ndc"]),
            stack(p_out["nl"]).reshape(depth, bp, D_R), stack(p_out["nlc"]),
            s_out["s_all"], stack(s_out["ndc"]), stack(s_out["nl"]).reshape(depth, bs, D_R), stack(s_out["nlc"]))
```

```python
import functools
import math

import jax
import jax.numpy as jnp
from jax import lax
from jax.experimental import pallas as pl
from jax.experimental.pallas import tpu as pltpu

F32 = jnp.float32
BF16 = jnp.bfloat16

D_MODEL = 1024
D_FF = 2816
D_A = 512
D_R = 512
H_A = 4
DK = 128
DV = 128
CHUNK = 64
CONV_W = 4
NB_R = 8
C_RG = 8.0
EPS = 1e-6
N_META = 16
D_QKV = 3 * D_A
D_BA = 128
D_PROJ_A = D_QKV + D_A
D_PROJ_B = 2 * D_R + D_BA

SUBLANES = 8
BF16_ROWS = 16
TAIL = SUBLANES
VMEM_BYTES_V7X = 64 * 1024 * 1024
VMEM_LIMIT = VMEM_BYTES_V7X - 8 * 1024 * 1024
ROW_TILE = 512
MIX_TILE = 256
FF_PIECE = 256
PROJ_PIECE = 256
MIX_PLAN = "ad" * 7 + ("adl" + "ad") * 3 + "adl" + "ad" * 8
SAMPLE_TILES_PER_STEP = 2


def _rms(x, w):
    ms = jnp.mean(x * x, axis=-1, keepdims=True)
    return x * lax.rsqrt(ms + EPS) * w


def _sigmoid(x):
    return 1.0 / (1.0 + jnp.exp(-x))


def _silu(x):
    return x * _sigmoid(x)


def _softplus(x):
    t = jnp.exp(-jnp.abs(x))
    u = 1.0 + t
    d = u - 1.0
    log1p_t = jnp.where(d == 0.0, t, jnp.log(u) * (t / jnp.where(d == 0.0, 1.0, d)))
    return jnp.maximum(x, 0.0) + log1p_t


def _dot(a, b):
    return jnp.dot(a.astype(BF16), b.astype(BF16), preferred_element_type=F32)


def _dot_nt(a, b):
    return lax.dot_general(a.astype(BF16), b.astype(BF16), (((1,), (1,)), ((), ())),
                           preferred_element_type=F32)


def _dot_tn(a, b):
    return lax.dot_general(a.astype(BF16), b.astype(BF16), (((0,), (0,)), ((), ())),
                           preferred_element_type=F32)


def _split3(a):
    a1 = a.astype(BF16)
    r = a - a1.astype(F32)
    a2 = r.astype(BF16)
    a3 = (r - a2.astype(F32)).astype(BF16)
    return a1, a2, a3


def _ffn_pieces(h, wg_ref, wu_ref, wd_ref):
    def down(gu, acc):
        g, u, c0 = gu
        a = (_silu(g) * u).astype(BF16)
        d = jnp.dot(a, wd_ref[c0:c0 + FF_PIECE, :], preferred_element_type=F32)
        return d if acc is None else acc + d

    acc, pending = None, None
    for c0 in range(0, D_FF, FF_PIECE):
        g = jnp.dot(h, wg_ref[:, c0:c0 + FF_PIECE], preferred_element_type=F32)
        u = jnp.dot(h, wu_ref[:, c0:c0 + FF_PIECE], preferred_element_type=F32)
        if pending is not None:
            acc = down(pending, acc)
        pending = (g, u, c0)
        yield g
    yield down(pending, acc)


def _ffn(x, nw, wg_ref, wu_ref, wd_ref):
    h = _rms(x, nw).astype(BF16)
    for acc in _ffn_pieces(h, wg_ref, wu_ref, wd_ref):
        pass
    return x + 0.5 * acc


def _ffn_in_body(x_ref, n1_ref, wg_ref, wu_ref, wd_ref, nm_ref, wina_ref, winb_ref,
                 x1_ref, qkv_ref, z_ref, xr_ref, gate_ref, ba_ref):
    x1 = _ffn(x_ref[...], n1_ref[...], wg_ref, wu_ref, wd_ref)
    x1_ref[...] = x1
    hm = _rms(x1, nm_ref[...]).astype(BF16)
    pa = jnp.dot(hm, wina_ref[...], preferred_element_type=F32)
    pb = jnp.dot(hm, winb_ref[...], preferred_element_type=F32)
    qkv_ref[...] = pa[:, :D_QKV]
    z_ref[...] = pa[:, D_QKV:]
    xr_ref[...] = pb[:, :D_R]
    gate_ref[...] = pb[:, D_R:2 * D_R]
    ba_ref[...] = pb[:, 2 * D_R:]


def _out_ffn_body(x_ref, o_ref, y_ref, wo1_ref, wo2_ref, n2_ref, wg_ref, wu_ref, wd_ref, nf_ref, *rest,
                  final, n_cast):
    src_refs, out_ref, dst_refs = rest[:n_cast], rest[n_cast], rest[n_cast + 1:]
    x = x_ref[...] + (_dot(o_ref[...], wo1_ref[...]) + _dot(y_ref[...], wo2_ref[...]))
    x2 = _ffn(x, n2_ref[...], wg_ref, wu_ref, wd_ref)
    if final:
        x2 = _rms(x2, nf_ref[...])
    out_ref[...] = x2
    for s_ref, d_ref in zip(src_refs, dst_refs):
        d_ref[...] = s_ref[...].astype(BF16)


def _resident(shape):
    nd = len(shape)
    return pl.BlockSpec(shape, lambda *_: (0,) * nd, pipeline_mode=pl.Buffered(1))


def _row_block(shape, first):
    return pl.BlockSpec(shape, lambda *_: (first, 0), pipeline_mode=pl.Buffered(1))


def _rows(tm, width):
    return pl.BlockSpec((tm, width), lambda i: (i, 0))


def _ffn_in(x, lw):
    r = x.shape[0]
    tm = min(ROW_TILE, r)
    widths = (D_MODEL, D_QKV, D_A, D_R, D_R, D_BA)
    return pl.pallas_call(
        _ffn_in_body,
        grid=(pl.cdiv(r, tm),),
        in_specs=[_rows(tm, D_MODEL), _resident((1, D_MODEL)),
                  _resident((D_MODEL, D_FF)), _resident((D_MODEL, D_FF)), _resident((D_FF, D_MODEL)),
                  _resident((1, D_MODEL)), _resident((D_MODEL, D_PROJ_A)), _resident((D_MODEL, D_PROJ_B))],
        out_specs=[_rows(tm, w) for w in widths],
        out_shape=[jax.ShapeDtypeStruct((r, w), F32) for w in widths],
        compiler_params=pltpu.CompilerParams(dimension_semantics=("parallel",),
                                             vmem_limit_bytes=VMEM_LIMIT),
        name="ffn_in",
    )(x, lw["n1"], lw["f1g"], lw["f1u"], lw["f1d"], lw["nm"], lw["w_in_a"], lw["w_in_b"])


def _out_ffn(x, o, y, lw, nf, final, cast=()):
    r = x.shape[0]
    tm = min(ROW_TILE, r)
    steps = pl.cdiv(r, tm)
    cast_in, cast_out, cast_shape = [], [], []
    for arr, layer, cols in cast:
        rows = arr.shape[1]
        cr = BF16_ROWS * pl.cdiv(pl.cdiv(rows, steps), BF16_ROWS)
        last = pl.cdiv(rows, cr) - 1
        cast_in.append(pl.BlockSpec((None, cr, cols), lambda i, layer=layer, last=last: (layer, jnp.minimum(i, last), 0)))
        cast_out.append(pl.BlockSpec((cr, cols), lambda i, last=last: (jnp.minimum(i, last), 0)))
        cast_shape.append(jax.ShapeDtypeStruct((rows, cols), BF16))
    res = pl.pallas_call(
        functools.partial(_out_ffn_body, final=final, n_cast=len(cast)),
        grid=(steps,),
        in_specs=[_rows(tm, D_MODEL), _rows(tm, D_A), _rows(tm, D_R),
                  _row_block((D_A, D_MODEL), 0), _row_block((D_R, D_MODEL), 1), _resident((1, D_MODEL)),
                  _resident((D_MODEL, D_FF)), _resident((D_MODEL, D_FF)), _resident((D_FF, D_MODEL)),
                  _resident((1, D_MODEL))] + cast_in,
        out_specs=[_rows(tm, D_MODEL)] + cast_out,
        out_shape=[jax.ShapeDtypeStruct((r, D_MODEL), F32)] + cast_shape,
        compiler_params=pltpu.CompilerParams(dimension_semantics=("arbitrary",),
                                             vmem_limit_bytes=VMEM_LIMIT),
        name="out_ffn",
    )(x, o, y, lw["w_out"], lw["w_out"], lw["n2"], lw["f2g"], lw["f2u"], lw["f2d"], nf, *[c[0] for c in cast])
    return res[0], res[1:]


class _Io:
    def __init__(self, **fns):
        self.after = lambda x: x
        self.__dict__.update(fns)


def _delta_stream(io, cw_ref, alog_ref, dtb_ref, on_ref, *, tiles, seg):
    c = seg
    nb = tiles[0][1]
    r = nb * c
    levels = int(math.log2(c))
    assert 2 ** levels == c
    row = lax.broadcasted_iota(jnp.int32, (r, r), 0)
    col = lax.broadcasted_iota(jnp.int32, (r, r), 1)
    same = lax.shift_right_logical(row, levels) == lax.shift_right_logical(col, levels)
    causal = same & (row >= col)
    strict = same & (row > col)
    tri = causal.astype(BF16)
    ones_seg = same.astype(BF16)
    eye = (row == col).astype(F32)
    sel_r = lax.broadcasted_iota(jnp.int32, (SUBLANES, D_BA), 0)
    sel_c = lax.broadcasted_iota(jnp.int32, (SUBLANES, D_BA), 1)
    sel = ((sel_c == sel_r + H_A) & (sel_r < H_A)).astype(BF16)

    def conv_silu(tile, col0):
        win = io.ext(tile, col0, DK)
        acc = None
        for j in range(CONV_W):
            k = CONV_W - 1 - j
            x = win if k == 0 else pltpu.roll(win, k, 1)
            term = x[:, TAIL:, :].reshape(r, DK) * io.after(cw_ref[j:j + 1, col0:col0 + DK])
            acc = term if acc is None else acc + term
        return _silu(acc)

    def exact_dot(lhs01, x, nt=False):
        out = None
        for p in _split3(x):
            if nt:
                d = lax.dot_general(lhs01, p, (((1,), (1,)), ((), ())), preferred_element_type=F32)
            else:
                d = jnp.dot(lhs01, p, preferred_element_type=F32)
            out = d if out is None else out + d
        return out

    items = [(j, h) for j in range(len(tiles)) for h in range(H_A)]

    gcum, gtot, grow, eg_all, beta_all = [], [], [], [], []
    for tile in tiles:
        ba = io.ba(tile)
        g_all = -jnp.exp(io.after(alog_ref[...])) * _softplus(ba + dtb_ref[...])
        beta_all.append(_sigmoid(ba))
        gcum.append(exact_dot(tri, g_all))
        gtot.append(exact_dot(ones_seg, g_all) if nb > 1 else gcum[-1][r - 1:r, :])
    yield
    for j in range(len(tiles)):
        grow.append(exact_dot(sel, gcum[j], nt=True))
        eg_all.append(jnp.exp(gcum[j]))
    yield

    qg, kn, kb, kdec, decay, rhs = {}, {}, {}, {}, {}, {}
    for (j, h) in items:
        tile = tiles[j]
        qh = conv_silu(tile, h * DK)
        kh = conv_silu(tile, D_A + h * DK)
        vh = conv_silu(tile, 2 * D_A + h * DV)
        qn = qh * lax.rsqrt(jnp.sum(qh * qh, axis=-1, keepdims=True) + EPS) * (DK ** -0.5)
        k_n = kh * lax.rsqrt(jnp.sum(kh * kh, axis=-1, keepdims=True) + EPS)
        beta = beta_all[j][:, h:h + 1]
        gcol = gcum[j][:, H_A + h:H_A + h + 1]
        egcol = eg_all[j][:, H_A + h:H_A + h + 1]
        glast = gtot[j][:, H_A + h:H_A + h + 1]
        dmat = gcol - grow[j][h:h + 1, :]
        decay[j, h] = jnp.where(causal, jnp.exp(jnp.where(causal, dmat, 0.0)), 0.0)
        kn[j, h] = k_n
        kb[j, h] = k_n * beta
        qg[j, h] = (qn, qn * egcol)
        kdec[j, h] = k_n * jnp.exp(glast - gcol)
        rhs[j, h] = jnp.concatenate([vh * beta, kb[j, h] * egcol], axis=-1)
        if h == H_A - 1:
            yield

    kq = {it: _dot_nt(jnp.concatenate([kb[it], qg[it][0]], axis=0), kn[it]) for it in items}
    lmat = {it: jnp.where(strict, kq[it][:r] * decay[it], 0.0) for it in items}
    attn = {it: kq[it][r:] * decay[it] for it in items}
    yield

    tinv = {it: eye - lmat[it] for it in items}
    m = {it: _dot(lmat[it], lmat[it]) for it in items}
    yield
    for _ in range(levels - 2):
        tinv = {it: tinv[it] + _dot(tinv[it], m[it]) for it in items}
        m = {it: _dot(m[it], m[it]) for it in items}
        yield
    tinv = {it: tinv[it] + _dot(tinv[it], m[it]) for it in items}
    yield
    sol = {it: _dot(tinv[it], rhs[it]) for it in items}
    yield

    state = {}

    def get_state(b, h):
        if (b, h) not in state:
            state[b, h] = io.get_state(b, h)
        return state[b, h]

    for j, tile in enumerate(tiles):
        b0 = tile[0]
        segs = [(s, slice(s * c, (s + 1) * c)) for s in range(nb)]
        ws = {}
        for h in range(H_A):
            for s, rs in segs:
                lhs = jnp.concatenate([sol[j, h][rs, DV:], qg[j, h][1][rs]], axis=0)
                ws[h, s] = _dot(lhs, get_state(b0 + s, h))
        yield
        v_new, av = {}, {}
        for h in range(H_A):
            w_s = jnp.concatenate([ws[h, s][:c] for s, _ in segs], axis=0) if nb > 1 else ws[h, 0][:c]
            v_new[h] = sol[j, h][:, :DV] - w_s
        for h in range(H_A):
            av[h] = _dot(attn[j, h], v_new[h])
        for h in range(H_A):
            for s, rs in segs:
                g0 = rs.start if nb > 1 else 0
                scale = jnp.exp(gtot[j][g0:g0 + 1, H_A + h:H_A + h + 1])
                state[b0 + s, h] = state[b0 + s, h] * scale + _dot_tn(kdec[j, h][rs], v_new[h][rs])
        for h in range(H_A):
            q_s = jnp.concatenate([ws[h, s][c:] for s, _ in segs], axis=0) if nb > 1 else ws[h, 0][c:]
            on = _rms(q_s + av[h], io.after(on_ref[...]))
            io.put_o(tile, h, on * _silu(io.z(tile, h * DV)))
        if j == len(tiles) - 1:
            for (b, h), val in state.items():
                io.put_state(b, h, val)
        yield


def _delta_body(qkv_ref, z_ref, ba_ref, cs_ref, s0_ref, cw_ref, alog_ref, dtb_ref, on_ref, *rest,
                tiles, seg, tt, n_t, fill_layers):
    if fill_layers:
        o_ref, sall_ref, csnew_ref, ext_ref = rest
        snew_ref = sall_ref.at[0]
    else:
        _, o_ref, snew_ref, csnew_ref, ext_ref = rest
    c = seg
    nb = tiles[0][1]
    r = nb * c
    t = pl.program_id(1)

    @pl.when(t == 0)
    def _():
        ext_ref[:, TAIL - (CONV_W - 1):TAIL, :] = cs_ref[...]
        snew_ref[...] = s0_ref[...]

    ext_ref[:, TAIL:TAIL + tt, :] = qkv_ref[...]

    def rows(ref, tile, off, col0, width):
        b0, _, t0 = tile
        return ref[b0:b0 + nb, off + t0:off + t0 + c, col0:col0 + width].reshape(r, width)

    def put_o(tile, h, val):
        b0, _, t0 = tile
        o_ref[b0:b0 + nb, t0:t0 + c, h * DV:(h + 1) * DV] = val.reshape(nb, c, DV).astype(BF16)

    def put_state(b, h, val):
        snew_ref[b, h] = val

    def ext(tile, col0, w):
        b0, _, t0 = tile
        return ext_ref[b0:b0 + nb, t0:t0 + TAIL + c, col0:col0 + w]

    io = _Io(ext=ext, ba=lambda tile: rows(ba_ref, tile, 0, 0, D_BA),
             z=lambda tile, col0: rows(z_ref, tile, 0, col0, DV),
             put_o=put_o, get_state=lambda b, h: snew_ref[b, h], put_state=put_state)
    for _ in _delta_stream(io, cw_ref, alog_ref, dtb_ref, on_ref, tiles=tiles, seg=seg):
        pass

    tail = ext_ref[:, tt:tt + TAIL, :]
    ext_ref[:, 0:TAIL, :] = tail

    @pl.when(t == n_t - 1)
    def _():
        csnew_ref[...] = ext_ref[:, TAIL - (CONV_W - 1):TAIL, :]
        if fill_layers:
            for d in range(1, fill_layers):
                sall_ref[d] = sall_ref[0]


def _delta(qkv, z, ba, cs, s0, lw, l, s_all, *, seg, tt, bb, tiles):
    b, t, _ = qkv.shape
    depth = s0.shape[0]
    n_t = t // tt
    assert n_t * tt == t and b % bb == 0 and (l == 0) == (s_all is None)
    st = lambda i, j: (l, i, 0, 0, 0)
    ct = lambda i, j: (l, i, 0, 0)
    tok = lambda w: pl.BlockSpec((bb, tt, w), lambda i, j: (i, j, 0))
    const = lambda shape: pl.BlockSpec(shape, lambda i, j: (0,) * len(shape))
    in_specs = [tok(D_QKV), tok(D_A), tok(D_BA),
                pl.BlockSpec((None, bb, CONV_W - 1, D_QKV), ct),
                pl.BlockSpec((None, bb, H_A, DK, DV), st),
                const((CONV_W, D_QKV)), const((1, D_BA)), const((1, D_BA)), const((1, DV))]
    operands = [qkv, z, ba, cs, s0, lw["cqkv"], lw["alog"], lw["dtb"], lw["onorm"]]
    if s_all is None:
        s_spec = pl.BlockSpec((depth, bb, H_A, DK, DV), lambda i, j: (0, i, 0, 0, 0))
        aliases = {}
    else:
        s_spec = pl.BlockSpec((None, bb, H_A, DK, DV), st)
        in_specs.append(pl.BlockSpec(memory_space=pl.ANY))
        operands.append(s_all)
        aliases = {len(operands) - 1: 1}
    return pl.pallas_call(
        functools.partial(_delta_body, tiles=tiles, seg=seg, tt=tt, n_t=n_t,
                          fill_layers=depth if s_all is None else 0),
        grid=(b // bb, n_t),
        in_specs=in_specs,
        out_specs=[tok(D_A), s_spec,
                   pl.BlockSpec((bb, CONV_W - 1, D_QKV), lambda i, j: (i, 0, 0))],
        out_shape=[jax.ShapeDtypeStruct((b, t, D_A), BF16),
                   jax.ShapeDtypeStruct((depth, b, H_A, DK, DV), F32),
                   jax.ShapeDtypeStruct((b, CONV_W - 1, D_QKV), F32)],
        scratch_shapes=[pltpu.VMEM((bb, tt + TAIL, D_QKV), F32)],
        input_output_aliases=aliases,
        compiler_params=pltpu.CompilerParams(dimension_semantics=("arbitrary", "arbitrary"),
                                             vmem_limit_bytes=VMEM_LIMIT),
        name="delta",
    )(*operands)


def _lru_stream(io, cw_ref, cb_ref, wg_ref, bg_ref, lam_ref, *, tt, bb):
    win = io.ext()
    xc = None
    for j in range(CONV_W):
        k = CONV_W - 1 - j
        x = win if k == 0 else pltpu.roll(win, k, 1)
        term = x[:, TAIL:, :] * io.after(cw_ref[j:j + 1, :])
        xc = term if xc is None else xc + term
    xc = xc + cb_ref[...]
    n = bb * tt
    xc = xc.reshape(n, D_R)
    yield
    gates = jnp.dot(xc.astype(BF16), wg_ref[...], preferred_element_type=F32) + bg_ref[...]
    r = _sigmoid(gates[:, :D_R])
    i = _sigmoid(gates[:, D_R:])
    log_a = -C_RG * r * _softplus(-io.after(lam_ref[...]))
    a = jnp.exp(log_a)
    m2 = -jnp.tanh(log_a) * (a * a + 1.0)
    u = jnp.where(m2 > 0.0, m2 * lax.rsqrt(m2), 0.0) * i * xc
    yield

    ng = n // SUBLANES
    a = a.reshape(ng, SUBLANES, D_R)
    u = u.reshape(ng, SUBLANES, D_R)
    tpos = lax.broadcasted_iota(jnp.int32, (ng, SUBLANES, D_R), 1)
    s = 1
    while s < SUBLANES:
        a_sh = pltpu.roll(a, s, 1)
        u_sh = pltpu.roll(u, s, 1)
        m = tpos >= s
        u = jnp.where(m, u + a * u_sh, u)
        a = jnp.where(m, a * a_sh, a)
        s *= 2
    yield
    gpb = tt // SUBLANES
    if gpb == 1:
        hs = u + a * io.h0()
        h_last = hs[:, SUBLANES - 1:SUBLANES, :]
    else:
        assert bb == 1
        h_in = io.h0()[0]
        groups = []
        for g in range(gpb):
            hg = u[g] + a[g] * h_in
            groups.append(hg)
            h_in = hg[SUBLANES - 1:SUBLANES, :]
        hs = jnp.concatenate(groups, axis=0)
        h_last = h_in.reshape(1, 1, D_R)
    y = hs.reshape(n, D_R) * jax.nn.gelu(io.gate(), approximate=True)
    io.put_y(y.reshape(bb, tt, D_R))
    io.put_h(h_last)
    yield


def _lru_body(xr_ref, gate_ref, cs_ref, h0_ref, cw_ref, cb_ref, wg_ref, bg_ref, lam_ref,
              y_ref, hnew_ref, csnew_ref, ext_ref, *, tt, bb, n_t):
    t = pl.program_id(1)

    @pl.when(t == 0)
    def _():
        ext_ref[:, TAIL - (CONV_W - 1):TAIL, :] = cs_ref[...]
        hnew_ref[...] = h0_ref[...]

    ext_ref[:, TAIL:TAIL + tt, :] = xr_ref[...]

    def put_y(val):
        y_ref[...] = val.astype(BF16)

    def put_h(val):
        hnew_ref[...] = val

    io = _Io(ext=lambda: ext_ref[...], gate=lambda: gate_ref[...].reshape(bb * tt, D_R),
             h0=lambda: hnew_ref[...], put_y=put_y, put_h=put_h)
    for _ in _lru_stream(io, cw_ref, cb_ref, wg_ref, bg_ref, lam_ref, tt=tt, bb=bb):
        pass

    tail = ext_ref[:, tt:tt + TAIL, :]
    ext_ref[:, 0:TAIL, :] = tail

    @pl.when(t == n_t - 1)
    def _():
        csnew_ref[...] = ext_ref[:, TAIL - (CONV_W - 1):TAIL, :]


def _lru(xr, gate, cs, h0, lw, l, *, tt, bb):
    b, t, _ = xr.shape
    n_t = t // tt
    assert n_t * tt == t and b % bb == 0 and tt % SUBLANES == 0
    ct = lambda i, j: (l, i, 0, 0)
    tok = pl.BlockSpec((bb, tt, D_R), lambda i, j: (i, j, 0))
    const = lambda shape: pl.BlockSpec(shape, lambda i, j: (0,) * len(shape))
    return pl.pallas_call(
        functools.partial(_lru_body, tt=tt, bb=bb, n_t=n_t),
        grid=(b // bb, n_t),
        in_specs=[tok, tok,
                  pl.BlockSpec((None, bb, CONV_W - 1, D_R), ct),
                  pl.BlockSpec((None, bb, 1, D_R), ct),
                  const((CONV_W, D_R)), const((1, D_R)), const((D_R, 2 * D_R)), const((1, 2 * D_R)),
                  const((1, D_R))],
        out_specs=[tok,
                   pl.BlockSpec((bb, 1, D_R), lambda i, j: (i, 0, 0)),
                   pl.BlockSpec((bb, CONV_W - 1, D_R), lambda i, j: (i, 0, 0))],
        out_shape=[jax.ShapeDtypeStruct((b, t, D_R), BF16),
                   jax.ShapeDtypeStruct((b, 1, D_R), F32),
                   jax.ShapeDtypeStruct((b, CONV_W - 1, D_R), F32)],
        scratch_shapes=[pltpu.VMEM((bb, tt + TAIL, D_R), F32)],
        compiler_params=pltpu.CompilerParams(dimension_semantics=("arbitrary", "arbitrary"),
                                             vmem_limit_bytes=VMEM_LIMIT),
        name="lru",
    )(xr, gate, cs, h0, lw["clw"], lw["clb"], lw["wgate"], lw["bgate"], lw["lam"])


def _interleave(streams, plan, lead, set_dep):
    live = dict(streams)

    def step(key):
        if key not in live:
            return
        try:
            val = next(live[key])
            if key == lead:
                set_dep(val)
        except StopIteration:
            del live[key]

    for key in plan:
        step(key)
    while live:
        for key in list(live):
            step(key)


def _in_mix_body(flag_ref, x_ref, n1_ref, wg_ref, wu_ref, wd_ref, nm_ref, wina_ref, winb_ref,
                 dcs_ref, ds0_ref, cw_ref, alog_ref, dtb_ref, on_ref,
                 lcs_ref, lh0_ref, lcw_ref, lcb_ref, lwg_ref, lbg_ref, lam_ref,
                 x1_ref, o_ref, y_ref, snew_ref, dcsnew_ref, hnew_ref, lcsnew_ref,
                 qkv_scr, z_scr, ba_scr, xr_scr, gate_scr, dtail_scr, ltail_scr, s_scr, h_scr,
                 *, tiles_per_seq):
    tt = MIX_TILE
    s = pl.program_id(0)
    slot_a = lax.rem(s, 2)
    slot_b = 1 - slot_a
    first = lax.rem(s + tiles_per_seq - 1, tiles_per_seq) == 0
    tails = slice(TAIL - (CONV_W - 1), TAIL)
    keep = flag_ref[0] == 1
    dep = [None]

    def after(x):
        if dep[0] is None:
            return x
        reps = x.shape[-1] // DK
        d = dep[0] if reps == 1 else jnp.concatenate([dep[0]] * reps, axis=-1)
        return jnp.where(keep, x, d)

    def set_dep(val):
        dep[0] = val

    @pl.when(s == 0)
    def _():
        for ref in (qkv_scr, z_scr, ba_scr, xr_scr, gate_scr, dtail_scr, ltail_scr, s_scr, h_scr):
            ref[...] = jnp.zeros(ref.shape, F32)

    def stream_a():
        x = x_ref[...]
        h = _rms(x, n1_ref[...]).astype(BF16)
        pieces = _ffn_pieces(h, wg_ref, wu_ref, wd_ref)
        for _ in range(D_FF // FF_PIECE):
            yield next(pieces)[0:1, 0:DK]
        acc = next(pieces)
        x1 = x + 0.5 * acc
        x1_ref[...] = x1
        hm = _rms(x1, nm_ref[...]).astype(BF16)
        yield acc[0:1, 0:DK]
        dsts = ((qkv_scr, TAIL, wina_ref, 0, D_QKV), (z_scr, 0, wina_ref, D_QKV, D_A),
                (xr_scr, TAIL, winb_ref, 0, D_R), (gate_scr, 0, winb_ref, D_R, D_R),
                (ba_scr, 0, winb_ref, 2 * D_R, D_BA))
        for dst, row0, w_ref, col0, width in dsts:
            for d0 in range(0, width, PROJ_PIECE):
                w = min(PROJ_PIECE, width - d0)
                p = jnp.dot(hm, w_ref[:, col0 + d0:col0 + d0 + w], preferred_element_type=F32)
                dst[slot_a, row0:row0 + tt, d0:d0 + w] = p
                yield p[0:1, 0:DK]

    qkv_scr[slot_b, tails, :] = jnp.where(first, dcs_ref[0], dtail_scr[tails, :])
    xr_scr[slot_b, tails, :] = jnp.where(first, lcs_ref[0], ltail_scr[tails, :])

    def put_o(tile, h, val):
        o_ref[tile[2]:tile[2] + CHUNK, h * DV:(h + 1) * DV] = val.astype(BF16)

    def put_state(b, h, val):
        s_scr[h] = val
        snew_ref[0, h] = val

    def put_y(val):
        y_ref[...] = val.reshape(tt, D_R).astype(BF16)

    def put_h(val):
        h_scr[...] = val
        hnew_ref[...] = val

    dio = _Io(ext=lambda tile, col0, w: qkv_scr[slot_b, tile[2]:tile[2] + TAIL + CHUNK, col0:col0 + w][None],
              ba=lambda tile: ba_scr[slot_b, tile[2]:tile[2] + CHUNK, :],
              z=lambda tile, col0: z_scr[slot_b, tile[2]:tile[2] + CHUNK, col0:col0 + DV],
              put_o=put_o, get_state=lambda b, h: jnp.where(first, ds0_ref[0, h], s_scr[h]),
              put_state=put_state, after=after)
    lio = _Io(ext=lambda: xr_scr[slot_b][None],
              gate=lambda: gate_scr[slot_b], h0=lambda: jnp.where(first, lh0_ref[...], h_scr[...]),
              put_y=put_y, put_h=put_h, after=after)
    tiles = tuple((0, 1, j * CHUNK) for j in range(tt // CHUNK))
    _interleave({"a": stream_a(),
                 "d": _delta_stream(dio, cw_ref, alog_ref, dtb_ref, on_ref, tiles=tiles, seg=CHUNK),
                 "l": _lru_stream(lio, lcw_ref, lcb_ref, lwg_ref, lbg_ref, lam_ref, tt=tt, bb=1)},
                MIX_PLAN, "a", set_dep)

    dtail_scr[...] = qkv_scr[slot_b, tt:tt + TAIL, :]
    ltail_scr[...] = xr_scr[slot_b, tt:tt + TAIL, :]
    dcsnew_ref[0] = qkv_scr[slot_b, tt + TAIL - (CONV_W - 1):tt + TAIL, :]
    lcsnew_ref[0] = xr_scr[slot_b, tt + TAIL - (CONV_W - 1):tt + TAIL, :]


def _in_mix(x, sdc, sd, slc, sl, lw, l, *, batch):
    r = x.shape[0]
    tt = MIX_TILE
    n_tiles = r // tt
    tiles_per_seq = n_tiles // batch
    assert n_tiles * tt == r and tiles_per_seq * batch == n_tiles
    cur = lambda s: (jnp.minimum(s, n_tiles - 1), 0)
    prev = lambda s: (jnp.maximum(s - 1, 0), 0)
    seq3 = lambda s: (jnp.maximum(s - 1, 0) // tiles_per_seq, 0, 0)
    seq4 = lambda s: (jnp.maximum(s - 1, 0) // tiles_per_seq, 0, 0, 0)
    layer = lambda shape: pl.BlockSpec((None,) + shape, lambda s: (l,) + (0,) * len(shape),
                                       pipeline_mode=pl.Buffered(1))
    return pl.pallas_call(
        functools.partial(_in_mix_body, tiles_per_seq=tiles_per_seq),
        grid=(n_tiles + 1,),
        in_specs=[pl.BlockSpec(memory_space=pltpu.SMEM),
                  pl.BlockSpec((tt, D_MODEL), cur), _resident((1, D_MODEL)),
                  _resident((D_MODEL, D_FF)), _resident((D_MODEL, D_FF)), _resident((D_FF, D_MODEL)),
                  _resident((1, D_MODEL)), _resident((D_MODEL, D_PROJ_A)), _resident((D_MODEL, D_PROJ_B)),
                  layer((1, CONV_W - 1, D_QKV)), layer((1, H_A, DK, DV)),
                  _resident((CONV_W, D_QKV)), _resident((1, D_BA)), _resident((1, D_BA)), _resident((1, DV)),
                  layer((1, CONV_W - 1, D_R)), layer((1, 1, D_R)),
                  _resident((CONV_W, D_R)), _resident((1, D_R)), _resident((D_R, 2 * D_R)),
                  _resident((1, 2 * D_R)), _resident((1, D_R))],
        out_specs=[pl.BlockSpec((tt, D_MODEL), cur), pl.BlockSpec((tt, D_A), prev), pl.BlockSpec((tt, D_R), prev),
                   pl.BlockSpec((1, H_A, DK, DV), seq4), pl.BlockSpec((1, CONV_W - 1, D_QKV), seq3),
                   pl.BlockSpec((1, 1, D_R), seq3), pl.BlockSpec((1, CONV_W - 1, D_R), seq3)],
        out_shape=[jax.ShapeDtypeStruct((r, D_MODEL), F32), jax.ShapeDtypeStruct((r, D_A), BF16),
                   jax.ShapeDtypeStruct((r, D_R), BF16),
                   jax.ShapeDtypeStruct((batch, H_A, DK, DV), F32),
                   jax.ShapeDtypeStruct((batch, CONV_W - 1, D_QKV), F32),
                   jax.ShapeDtypeStruct((batch, 1, D_R), F32),
                   jax.ShapeDtypeStruct((batch, CONV_W - 1, D_R), F32)],
        scratch_shapes=[pltpu.VMEM((2, TAIL + tt, D_QKV), F32), pltpu.VMEM((2, tt, D_A), F32),
                        pltpu.VMEM((2, tt, D_BA), F32), pltpu.VMEM((2, TAIL + tt, D_R), F32),
                        pltpu.VMEM((2, tt, D_R), F32), pltpu.VMEM((TAIL, D_QKV), F32),
                        pltpu.VMEM((TAIL, D_R), F32), pltpu.VMEM((H_A, DK, DV), F32),
                        pltpu.VMEM((1, 1, D_R), F32)],
        compiler_params=pltpu.CompilerParams(dimension_semantics=("arbitrary",),
                                             vmem_limit_bytes=VMEM_LIMIT),
        name="in_mix",
    )(jnp.ones((1,), jnp.int32), x, lw["n1"], lw["f1g"], lw["f1u"], lw["f1d"], lw["nm"], lw["w_in_a"], lw["w_in_b"],
      sdc, sd, lw["cqkv"], lw["alog"], lw["dtb"], lw["onorm"],
      slc, sl, lw["clw"], lw["clb"], lw["wgate"], lw["bgate"], lw["lam"])


BIG_WEIGHTS = ("f1g", "f1u", "f1d", "w_in_a", "w_out", "f2g", "f2u", "f2d")


def _prep_layer(l, w, big):
    row = lambda v: v.reshape(1, -1).astype(F32)
    lane_pad = lambda v: jnp.zeros((1, D_BA), F32).at[0, H_A:2 * H_A].set(v)
    eye = jnp.eye(NB_R, dtype=F32)
    bd = lambda wb: jnp.einsum("ncd,nm->ncmd", wb, eye).reshape(D_R, D_R)
    w_in = w["w_in"][l]
    c4 = D_PROJ_A + 2 * H_A
    pad = jnp.zeros((D_MODEL, D_BA - 2 * H_A), w_in.dtype)
    w_in_b = jnp.concatenate([w_in[:, c4:], w_in[:, D_PROJ_A:c4], pad], axis=-1).astype(BF16)
    return dict(
        n1=row(w["norm_ffn1"][l]), f1g=big["f1g"], f1u=big["f1u"], f1d=big["f1d"],
        nm=row(w["norm_mix"][l]), w_in_a=big["w_in_a"], w_in_b=w_in_b,
        cqkv=w["conv_qkv"][l], alog=lane_pad(w["a_log"][l]), dtb=lane_pad(w["dt_bias"][l]),
        onorm=row(w["norm_delta_out"][l]),
        clw=w["conv_lru_w"][l], clb=row(w["conv_lru_b"][l]),
        wgate=jnp.concatenate([bd(w["w_rgate"][l]), bd(w["w_igate"][l])], axis=1).astype(BF16),
        bgate=jnp.concatenate([row(w["b_rgate"][l]), row(w["b_igate"][l])], axis=1),
        lam=row(w["lru_lambda"][l]),
        w_out=big["w_out"], n2=row(w["norm_ffn2"][l]), f2g=big["f2g"], f2u=big["f2u"], f2d=big["f2d"],
    )


def _group_layers(x, states_at, layers, nf, out, *, seg=None, delta_tt=None, delta_bb=None, tiles=None,
                  lru_tt=None, lru_bb=None, fused=False, states_only=False, cast_next=None):
    b, t, _ = x.shape
    xf = x.reshape(b * t, D_MODEL)
    s_all = None
    n_layers = len(layers)
    for l in range(n_layers):
        lw = layers[l]
        sd, sdc, sl, slc = states_at(l)
        if fused:
            x1, o, y, s_new, cs_new, h_new, lcs_new = _in_mix(xf, sdc, sd, slc, sl, lw, l, batch=b)
            out["nd"].append(s_new)
        else:
            x1, qkv, z, xr, gate, ba = _ffn_in(xf, lw)
            r3 = lambda v: v.reshape(b, t, v.shape[-1])
            o, s_all, cs_new = _delta(r3(qkv), r3(z), r3(ba), sdc, sd, lw, l, s_all,
                                      seg=seg, tt=delta_tt, bb=delta_bb, tiles=tiles)
            y, h_new, lcs_new = _lru(r3(xr), r3(gate), slc, sl, lw, l, tt=lru_tt, bb=lru_bb)
            out["s_all"] = s_all
        last = l == n_layers - 1
        if not (last and states_only):
            cast = cast_next(l) if cast_next is not None and not last else ()
            xf, copies = _out_ffn(x1, o.reshape(b * t, D_A), y.reshape(b * t, D_R), lw, nf, last, cast)
            if cast:
                out["cast"][l + 1] = dict(zip(BIG_WEIGHTS, copies))
        out["ndc"].append(cs_new)
        out["nl"].append(h_new)
        out["nlc"].append(lcs_new)
        if last and not states_only:
            out["y"] = xf.reshape(b, t, D_MODEL)
        yield


def kernel(x_prompt, x_sample, state_delta, state_delta_conv, state_lru, state_lru_conv, meta_tokens, norm_ffn1, w_ffn1_gate, w_ffn1_up, w_ffn1_down, norm_mix, w_in, conv_qkv, a_log, dt_bias, norm_delta_out, conv_lru_w, conv_lru_b, w_rgate, b_rgate, w_igate, b_igate, lru_lambda, w_out, norm_ffn2, w_ffn2_gate, w_ffn2_up, w_ffn2_down, norm_final):
    w = dict(norm_ffn1=norm_ffn1, w_ffn1_gate=w_ffn1_gate, w_ffn1_up=w_ffn1_up, w_ffn1_down=w_ffn1_down,
             norm_mix=norm_mix, w_in=w_in, conv_qkv=conv_qkv, a_log=a_log, dt_bias=dt_bias,
             norm_delta_out=norm_delta_out, conv_lru_w=conv_lru_w, conv_lru_b=conv_lru_b,
             w_rgate=w_rgate, b_rgate=b_rgate, w_igate=w_igate, b_igate=b_igate, lru_lambda=lru_lambda,
             w_out=w_out, norm_ffn2=norm_ffn2, w_ffn2_gate=w_ffn2_gate, w_ffn2_up=w_ffn2_up,
             w_ffn2_down=w_ffn2_down)
    depth = norm_ffn1.shape[0]
    big_f32 = dict(f1g=(w_ffn1_gate, D_FF), f1u=(w_ffn1_up, D_FF), f1d=(w_ffn1_down, D_MODEL),
                   w_in_a=(w_in, D_PROJ_A), w_out=(w_out, D_MODEL),
                   f2g=(w_ffn2_gate, D_FF), f2u=(w_ffn2_up, D_FF), f2d=(w_ffn2_down, D_MODEL))
    layers = [None] * depth
    layers[0] = _prep_layer(0, w, {k: big_f32[k][0][0, :, :big_f32[k][1]].astype(BF16) for k in BIG_WEIGHTS})
    big_next = lambda k, l: ((w_in[l + 1:l + 2, :, :D_PROJ_A], 0) if k == "w_in_a" else (big_f32[k][0], l + 1))
    cast_next = lambda l: tuple(big_next(k, l) + (big_f32[k][1],) for k in BIG_WEIGHTS)
    nf = norm_final.reshape(1, D_MODEL).astype(F32)
    bp, seq, _ = x_prompt.shape
    bs, dseq, _ = x_sample.shape
    seg_b = CHUNK // dseq
    new_out = lambda: dict(nd=[], ndc=[], nl=[], nlc=[], cast={}, s_all=None, y=None)
    stack = jnp.stack

    zeros = lambda *s: jnp.zeros((depth,) + s, F32)
    m_states = (zeros(1, H_A, DK, DV), zeros(1, CONV_W - 1, D_QKV), zeros(1, 1, D_R), zeros(1, CONV_W - 1, D_R))
    m_out, p_out, s_out = new_out(), new_out(), new_out()
    meta = _group_layers(meta_tokens.astype(F32)[None], lambda l: m_states, layers, nf, m_out,
                         seg=N_META, delta_tt=N_META, delta_bb=1, tiles=((0, 1, 0),), lru_tt=N_META, lru_bb=1,
                         states_only=True)
    after_meta = lambda l: (m_out["s_all"], stack(m_out["ndc"]), stack(m_out["nl"]), stack(m_out["nlc"]))
    prompt = _group_layers(x_prompt, after_meta, layers, nf, p_out, fused=True, cast_next=cast_next)
    s_states = (state_delta, state_delta_conv, state_lru.reshape(depth, bs, 1, D_R), state_lru_conv)
    sample = _group_layers(x_sample, lambda l: s_states, layers, nf, s_out,
                           seg=dseq, delta_tt=dseq, delta_bb=SAMPLE_TILES_PER_STEP * seg_b,
                           tiles=tuple((j * seg_b, seg_b, 0) for j in range(SAMPLE_TILES_PER_STEP)),
                           lru_tt=dseq, lru_bb=2 * seg_b)
    for l in range(depth):
        next(meta)
        next(prompt)
        if l + 1 < depth:
            layers[l + 1] = _prep_layer(l + 1, w, p_out["cast"][l + 1])
        next(sample)

    return (p_out["y"], s_out["y"], stack(p_out["nd"]), stack(p_out["ndc"]),
            stack(p_out["nl"]).reshape(depth, bp, D_R), stack(p_out["nlc"]),
            s_out["s_all"], stack(s_out["ndc"]), stack(s_out["nl"]).reshape(depth, bs, D_R), stack(s_out["nlc"]))
```

```python
import functools
import math

import jax
import jax.numpy as jnp
from jax import lax
from jax.experimental import pallas as pl
from jax.experimental.pallas import tpu as pltpu

F32 = jnp.float32
BF16 = jnp.bfloat16

D_MODEL = 1024
D_FF = 2816
D_A = 512
D_R = 512
H_A = 4
DK = 128
DV = 128
CHUNK = 64
CONV_W = 4
NB_R = 8
C_RG = 8.0
EPS = 1e-6
N_META = 16
D_QKV = 3 * D_A
D_BA = 128
D_PROJ_A = D_QKV + D_A
D_PROJ_B = 2 * D_R + D_BA

SUBLANES = 8
BF16_ROWS = 16
TAIL = SUBLANES
VMEM_BYTES_V7X = 64 * 1024 * 1024
VMEM_LIMIT = VMEM_BYTES_V7X - 8 * 1024 * 1024
ROW_TILE = 512
MIX_TILE = 256
FF_PIECE = 256
PROJ_PIECE = 256
MIX_PLAN = "ad" * 7 + ("adl" + "ad") * 3 + "adl" + "ad" * 8
SAMPLE_TILES_PER_STEP = 2


def _rms(x, w):
    ms = jnp.mean(x * x, axis=-1, keepdims=True)
    return x * lax.rsqrt(ms + EPS) * w


def _sigmoid(x):
    return 1.0 / (1.0 + jnp.exp(-x))


def _silu(x):
    return x * _sigmoid(x)


def _softplus(x):
    t = jnp.exp(-jnp.abs(x))
    u = 1.0 + t
    d = u - 1.0
    log1p_t = jnp.where(d == 0.0, t, jnp.log(u) * (t / jnp.where(d == 0.0, 1.0, d)))
    return jnp.maximum(x, 0.0) + log1p_t


def _dot(a, b):
    return jnp.dot(a.astype(BF16), b.astype(BF16), preferred_element_type=F32)


def _dot_nt(a, b):
    return lax.dot_general(a.astype(BF16), b.astype(BF16), (((1,), (1,)), ((), ())),
                           preferred_element_type=F32)


def _dot_tn(a, b):
    return lax.dot_general(a.astype(BF16), b.astype(BF16), (((0,), (0,)), ((), ())),
                           preferred_element_type=F32)


def _split3(a):
    a1 = a.astype(BF16)
    r = a - a1.astype(F32)
    a2 = r.astype(BF16)
    a3 = (r - a2.astype(F32)).astype(BF16)
    return a1, a2, a3


def _ffn_pieces(h, wg_ref, wu_ref, wd_ref):
    def down(gu, acc):
        g, u, c0 = gu
        a = (_silu(g) * u).astype(BF16)
        d = jnp.dot(a, wd_ref[c0:c0 + FF_PIECE, :], preferred_element_type=F32)
        return d if acc is None else acc + d

    acc, pending = None, None
    for c0 in range(0, D_FF, FF_PIECE):
        g = jnp.dot(h, wg_ref[:, c0:c0 + FF_PIECE], preferred_element_type=F32)
        u = jnp.dot(h, wu_ref[:, c0:c0 + FF_PIECE], preferred_element_type=F32)
        if pending is not None:
            acc = down(pending, acc)
        pending = (g, u, c0)
        yield g
    yield down(pending, acc)


def _ffn(x, nw, wg_ref, wu_ref, wd_ref):
    h = _rms(x, nw).astype(BF16)
    for acc in _ffn_pieces(h, wg_ref, wu_ref, wd_ref):
        pass
    return x + 0.5 * acc


def _ffn_piece_step(j, h_scr, acc_scr, wg_ref, wu_ref, wd_ref):
    g = jnp.dot(h_scr[...], wg_ref[...], preferred_element_type=F32)
    u = jnp.dot(h_scr[...], wu_ref[...], preferred_element_type=F32)
    d = jnp.dot((_silu(g) * u).astype(BF16), wd_ref[...], preferred_element_type=F32)

    @pl.when(j == 0)
    def _():
        acc_scr[...] = d

    @pl.when(j > 0)
    def _():
        acc_scr[...] += d


def _ffn_in_body(x_ref, n1_ref, wg_ref, wu_ref, wd_ref, nm_ref, wina_ref, winb_ref,
                 x1_ref, qkv_ref, z_ref, xr_ref, gate_ref, ba_ref, h_scr, acc_scr, *, n_f):
    j = pl.program_id(1)

    @pl.when(j == 0)
    def _():
        h_scr[...] = _rms(x_ref[...], n1_ref[...]).astype(BF16)

    _ffn_piece_step(j, h_scr, acc_scr, wg_ref, wu_ref, wd_ref)

    @pl.when(j == n_f - 1)
    def _():
        x1 = x_ref[...] + 0.5 * acc_scr[...]
        x1_ref[...] = x1
        hm = _rms(x1, nm_ref[...]).astype(BF16)
        pa = jnp.dot(hm, wina_ref[...], preferred_element_type=F32)
        pb = jnp.dot(hm, winb_ref[...], preferred_element_type=F32)
        qkv_ref[...] = pa[:, :D_QKV]
        z_ref[...] = pa[:, D_QKV:]
        xr_ref[...] = pb[:, :D_R]
        gate_ref[...] = pb[:, D_R:2 * D_R]
        ba_ref[...] = pb[:, 2 * D_R:]


def _out_ffn_stream_body(x_ref, o_ref, y_ref, wo1_ref, wo2_ref, n2_ref, wg_ref, wu_ref, wd_ref, nf_ref,
                         out_ref, x_scr, h_scr, acc_scr, *, n_f, final):
    j = pl.program_id(1)

    @pl.when(j == 0)
    def _():
        x = x_ref[...] + (_dot(o_ref[...], wo1_ref[...]) + _dot(y_ref[...], wo2_ref[...]))
        x_scr[...] = x
        h_scr[...] = _rms(x, n2_ref[...]).astype(BF16)

    _ffn_piece_step(j, h_scr, acc_scr, wg_ref, wu_ref, wd_ref)

    @pl.when(j == n_f - 1)
    def _():
        x2 = x_scr[...] + 0.5 * acc_scr[...]
        if final:
            x2 = _rms(x2, nf_ref[...])
        out_ref[...] = x2


def _out_ffn_body(x_ref, o_ref, y_ref, wo1_ref, wo2_ref, n2_ref, wg_ref, wu_ref, wd_ref, nf_ref, *rest,
                  final, n_cast):
    src_refs, out_ref, dst_refs = rest[:n_cast], rest[n_cast], rest[n_cast + 1:]
    x = x_ref[...] + (_dot(o_ref[...], wo1_ref[...]) + _dot(y_ref[...], wo2_ref[...]))
    x2 = _ffn(x, n2_ref[...], wg_ref, wu_ref, wd_ref)
    if final:
        x2 = _rms(x2, nf_ref[...])
    out_ref[...] = x2
    for s_ref, d_ref in zip(src_refs, dst_refs):
        d_ref[...] = s_ref[...].astype(BF16)


def _resident(shape):
    nd = len(shape)
    return pl.BlockSpec(shape, lambda *_: (0,) * nd, pipeline_mode=pl.Buffered(1))


def _row_block(shape, first):
    return pl.BlockSpec(shape, lambda *_: (first, 0), pipeline_mode=pl.Buffered(1))


def _rows(tm, width):
    return pl.BlockSpec((tm, width), lambda i: (i, 0))


def _ffn_slices():
    return [pl.BlockSpec((D_MODEL, FF_PIECE), lambda i, j: (0, j)),
            pl.BlockSpec((D_MODEL, FF_PIECE), lambda i, j: (0, j)),
            pl.BlockSpec((FF_PIECE, D_MODEL), lambda i, j: (j, 0))]


def _ffn_in(x, lw):
    r = x.shape[0]
    tm = min(ROW_TILE, r)
    n_f = D_FF // FF_PIECE
    widths = (D_MODEL, D_QKV, D_A, D_R, D_R, D_BA)
    rows = lambda w: pl.BlockSpec((tm, w), lambda i, j: (i, 0))
    return pl.pallas_call(
        functools.partial(_ffn_in_body, n_f=n_f),
        grid=(pl.cdiv(r, tm), n_f),
        in_specs=[rows(D_MODEL), _resident((1, D_MODEL))] + _ffn_slices() +
                 [_resident((1, D_MODEL)), _resident((D_MODEL, D_PROJ_A)), _resident((D_MODEL, D_PROJ_B))],
        out_specs=[rows(w) for w in widths],
        out_shape=[jax.ShapeDtypeStruct((r, w), F32) for w in widths],
        scratch_shapes=[pltpu.VMEM((tm, D_MODEL), BF16), pltpu.VMEM((tm, D_MODEL), F32)],
        compiler_params=pltpu.CompilerParams(dimension_semantics=("arbitrary", "arbitrary"),
                                             vmem_limit_bytes=VMEM_LIMIT),
        name="ffn_in",
    )(x, lw["n1"], lw["f1g"], lw["f1u"], lw["f1d"], lw["nm"], lw["w_in_a"], lw["w_in_b"])


def _out_ffn_stream(x, o, y, lw, nf, final):
    r = x.shape[0]
    tm = min(ROW_TILE, r)
    n_f = D_FF // FF_PIECE
    rows = lambda w: pl.BlockSpec((tm, w), lambda i, j: (i, 0))
    return pl.pallas_call(
        functools.partial(_out_ffn_stream_body, n_f=n_f, final=final),
        grid=(pl.cdiv(r, tm), n_f),
        in_specs=[rows(D_MODEL), rows(D_A), rows(D_R),
                  _row_block((D_A, D_MODEL), 0), _row_block((D_R, D_MODEL), 1), _resident((1, D_MODEL))] +
                 _ffn_slices() + [_resident((1, D_MODEL))],
        out_specs=rows(D_MODEL),
        out_shape=jax.ShapeDtypeStruct((r, D_MODEL), F32),
        scratch_shapes=[pltpu.VMEM((tm, D_MODEL), F32), pltpu.VMEM((tm, D_MODEL), BF16),
                        pltpu.VMEM((tm, D_MODEL), F32)],
        compiler_params=pltpu.CompilerParams(dimension_semantics=("arbitrary", "arbitrary"),
                                             vmem_limit_bytes=VMEM_LIMIT),
        name="out_ffn_stream",
    )(x, o, y, lw["w_out"], lw["w_out"], lw["n2"], lw["f2g"], lw["f2u"], lw["f2d"], nf)


def _out_ffn(x, o, y, lw, nf, final, cast=()):
    r = x.shape[0]
    tm = min(ROW_TILE, r)
    steps = pl.cdiv(r, tm)
    cast_in, cast_out, cast_shape = [], [], []
    for arr, layer, cols in cast:
        rows = arr.shape[1]
        cr = BF16_ROWS * pl.cdiv(pl.cdiv(rows, steps), BF16_ROWS)
        last = pl.cdiv(rows, cr) - 1
        cast_in.append(pl.BlockSpec((None, cr, cols), lambda i, layer=layer, last=last: (layer, jnp.minimum(i, last), 0)))
        cast_out.append(pl.BlockSpec((cr, cols), lambda i, last=last: (jnp.minimum(i, last), 0)))
        cast_shape.append(jax.ShapeDtypeStruct((rows, cols), BF16))
    res = pl.pallas_call(
        functools.partial(_out_ffn_body, final=final, n_cast=len(cast)),
        grid=(steps,),
        in_specs=[_rows(tm, D_MODEL), _rows(tm, D_A), _rows(tm, D_R),
                  _row_block((D_A, D_MODEL), 0), _row_block((D_R, D_MODEL), 1), _resident((1, D_MODEL)),
                  _resident((D_MODEL, D_FF)), _resident((D_MODEL, D_FF)), _resident((D_FF, D_MODEL)),
                  _resident((1, D_MODEL))] + cast_in,
        out_specs=[_rows(tm, D_MODEL)] + cast_out,
        out_shape=[jax.ShapeDtypeStruct((r, D_MODEL), F32)] + cast_shape,
        compiler_params=pltpu.CompilerParams(dimension_semantics=("arbitrary",),
                                             vmem_limit_bytes=VMEM_LIMIT),
        name="out_ffn",
    )(x, o, y, lw["w_out"], lw["w_out"], lw["n2"], lw["f2g"], lw["f2u"], lw["f2d"], nf, *[c[0] for c in cast])
    return res[0], res[1:]


class _Io:
    def __init__(self, **fns):
        self.after = lambda x: x
        self.__dict__.update(fns)


def _delta_stream(io, cw_ref, alog_ref, dtb_ref, on_ref, *, tiles, seg):
    c = seg
    nb = tiles[0][1]
    r = nb * c
    levels = int(math.log2(c))
    assert 2 ** levels == c
    row = lax.broadcasted_iota(jnp.int32, (r, r), 0)
    col = lax.broadcasted_iota(jnp.int32, (r, r), 1)
    same = lax.shift_right_logical(row, levels) == lax.shift_right_logical(col, levels)
    causal = same & (row >= col)
    strict = same & (row > col)
    tri = causal.astype(BF16)
    ones_seg = same.astype(BF16)
    eye = (row == col).astype(F32)
    sel_r = lax.broadcasted_iota(jnp.int32, (SUBLANES, D_BA), 0)
    sel_c = lax.broadcasted_iota(jnp.int32, (SUBLANES, D_BA), 1)
    sel = ((sel_c == sel_r + H_A) & (sel_r < H_A)).astype(BF16)

    def conv_silu(tile, col0):
        win = io.ext(tile, col0, DK)
        acc = None
        for j in range(CONV_W):
            k = CONV_W - 1 - j
            x = win if k == 0 else pltpu.roll(win, k, 1)
            term = x[:, TAIL:, :].reshape(r, DK) * io.after(cw_ref[j:j + 1, col0:col0 + DK])
            acc = term if acc is None else acc + term
        return _silu(acc)

    def exact_dot(lhs01, x, nt=False):
        out = None
        for p in _split3(x):
            if nt:
                d = lax.dot_general(lhs01, p, (((1,), (1,)), ((), ())), preferred_element_type=F32)
            else:
                d = jnp.dot(lhs01, p, preferred_element_type=F32)
            out = d if out is None else out + d
        return out

    items = [(j, h) for j in range(len(tiles)) for h in range(H_A)]

    gcum, gtot, grow, eg_all, beta_all = [], [], [], [], []
    for tile in tiles:
        ba = io.ba(tile)
        g_all = -jnp.exp(io.after(alog_ref[...])) * _softplus(ba + dtb_ref[...])
        beta_all.append(_sigmoid(ba))
        gcum.append(exact_dot(tri, g_all))
        gtot.append(exact_dot(ones_seg, g_all) if nb > 1 else gcum[-1][r - 1:r, :])
    yield
    for j in range(len(tiles)):
        grow.append(exact_dot(sel, gcum[j], nt=True))
        eg_all.append(jnp.exp(gcum[j]))
    yield

    qg, kn, kb, kdec, decay, rhs = {}, {}, {}, {}, {}, {}
    for (j, h) in items:
        tile = tiles[j]
        qh = conv_silu(tile, h * DK)
        kh = conv_silu(tile, D_A + h * DK)
        vh = conv_silu(tile, 2 * D_A + h * DV)
        qn = qh * lax.rsqrt(jnp.sum(qh * qh, axis=-1, keepdims=True) + EPS) * (DK ** -0.5)
        k_n = kh * lax.rsqrt(jnp.sum(kh * kh, axis=-1, keepdims=True) + EPS)
        beta = beta_all[j][:, h:h + 1]
        gcol = gcum[j][:, H_A + h:H_A + h + 1]
        egcol = eg_all[j][:, H_A + h:H_A + h + 1]
        glast = gtot[j][:, H_A + h:H_A + h + 1]
        dmat = gcol - grow[j][h:h + 1, :]
        decay[j, h] = jnp.where(causal, jnp.exp(jnp.where(causal, dmat, 0.0)), 0.0)
        kn[j, h] = k_n
        kb[j, h] = k_n * beta
        qg[j, h] = (qn, qn * egcol)
        kdec[j, h] = k_n * jnp.exp(glast - gcol)
        rhs[j, h] = jnp.concatenate([vh * beta, kb[j, h] * egcol], axis=-1)
        if h == H_A - 1:
            yield

    kq = {it: _dot_nt(jnp.concatenate([kb[it], qg[it][0]], axis=0), kn[it]) for it in items}
    lmat = {it: jnp.where(strict, kq[it][:r] * decay[it], 0.0) for it in items}
    attn = {it: kq[it][r:] * decay[it] for it in items}
    yield

    tinv = {it: eye - lmat[it] for it in items}
    m = {it: _dot(lmat[it], lmat[it]) for it in items}
    yield
    for _ in range(levels - 2):
        tinv = {it: tinv[it] + _dot(tinv[it], m[it]) for it in items}
        m = {it: _dot(m[it], m[it]) for it in items}
        yield
    tinv = {it: tinv[it] + _dot(tinv[it], m[it]) for it in items}
    yield
    sol = {it: _dot(tinv[it], rhs[it]) for it in items}
    yield

    state = {}

    def get_state(b, h):
        if (b, h) not in state:
            state[b, h] = io.get_state(b, h)
        return state[b, h]

    for j, tile in enumerate(tiles):
        b0 = tile[0]
        segs = [(s, slice(s * c, (s + 1) * c)) for s in range(nb)]
        ws = {}
        for h in range(H_A):
            for s, rs in segs:
                lhs = jnp.concatenate([sol[j, h][rs, DV:], qg[j, h][1][rs]], axis=0)
                ws[h, s] = _dot(lhs, get_state(b0 + s, h))
        yield
        v_new, av = {}, {}
        for h in range(H_A):
            w_s = jnp.concatenate([ws[h, s][:c] for s, _ in segs], axis=0) if nb > 1 else ws[h, 0][:c]
            v_new[h] = sol[j, h][:, :DV] - w_s
        for h in range(H_A):
            av[h] = _dot(attn[j, h], v_new[h])
        for h in range(H_A):
            for s, rs in segs:
                g0 = rs.start if nb > 1 else 0
                scale = jnp.exp(gtot[j][g0:g0 + 1, H_A + h:H_A + h + 1])
                state[b0 + s, h] = state[b0 + s, h] * scale + _dot_tn(kdec[j, h][rs], v_new[h][rs])
        for h in range(H_A):
            q_s = jnp.concatenate([ws[h, s][c:] for s, _ in segs], axis=0) if nb > 1 else ws[h, 0][c:]
            on = _rms(q_s + av[h], io.after(on_ref[...]))
            io.put_o(tile, h, on * _silu(io.z(tile, h * DV)))
        if j == len(tiles) - 1:
            for (b, h), val in state.items():
                io.put_state(b, h, val)
        yield


def _delta_body(qkv_ref, z_ref, ba_ref, cs_ref, s0_ref, cw_ref, alog_ref, dtb_ref, on_ref, *rest,
                tiles, seg, tt, n_t, fill_layers):
    if fill_layers:
        o_ref, sall_ref, csnew_ref, ext_ref = rest
        snew_ref = sall_ref.at[0]
    else:
        _, o_ref, snew_ref, csnew_ref, ext_ref = rest
    c = seg
    nb = tiles[0][1]
    r = nb * c
    t = pl.program_id(1)

    @pl.when(t == 0)
    def _():
        ext_ref[:, TAIL - (CONV_W - 1):TAIL, :] = cs_ref[...]
        snew_ref[...] = s0_ref[...]

    ext_ref[:, TAIL:TAIL + tt, :] = qkv_ref[...]

    def rows(ref, tile, off, col0, width):
        b0, _, t0 = tile
        return ref[b0:b0 + nb, off + t0:off + t0 + c, col0:col0 + width].reshape(r, width)

    def put_o(tile, h, val):
        b0, _, t0 = tile
        o_ref[b0:b0 + nb, t0:t0 + c, h * DV:(h + 1) * DV] = val.reshape(nb, c, DV).astype(BF16)

    def put_state(b, h, val):
        snew_ref[b, h] = val

    def ext(tile, col0, w):
        b0, _, t0 = tile
        return ext_ref[b0:b0 + nb, t0:t0 + TAIL + c, col0:col0 + w]

    io = _Io(ext=ext, ba=lambda tile: rows(ba_ref, tile, 0, 0, D_BA),
             z=lambda tile, col0: rows(z_ref, tile, 0, col0, DV),
             put_o=put_o, get_state=lambda b, h: snew_ref[b, h], put_state=put_state)
    for _ in _delta_stream(io, cw_ref, alog_ref, dtb_ref, on_ref, tiles=tiles, seg=seg):
        pass

    tail = ext_ref[:, tt:tt + TAIL, :]
    ext_ref[:, 0:TAIL, :] = tail

    @pl.when(t == n_t - 1)
    def _():
        csnew_ref[...] = ext_ref[:, TAIL - (CONV_W - 1):TAIL, :]
        if fill_layers:
            for d in range(1, fill_layers):
                sall_ref[d] = sall_ref[0]


def _delta(qkv, z, ba, cs, s0, lw, l, s_all, *, seg, tt, bb, tiles):
    b, t, _ = qkv.shape
    depth = s0.shape[0]
    n_t = t // tt
    assert n_t * tt == t and b % bb == 0 and (l == 0) == (s_all is None)
    st = lambda i, j: (l, i, 0, 0, 0)
    ct = lambda i, j: (l, i, 0, 0)
    tok = lambda w: pl.BlockSpec((bb, tt, w), lambda i, j: (i, j, 0))
    const = lambda shape: pl.BlockSpec(shape, lambda i, j: (0,) * len(shape))
    in_specs = [tok(D_QKV), tok(D_A), tok(D_BA),
                pl.BlockSpec((None, bb, CONV_W - 1, D_QKV), ct),
                pl.BlockSpec((None, bb, H_A, DK, DV), st),
                const((CONV_W, D_QKV)), const((1, D_BA)), const((1, D_BA)), const((1, DV))]
    operands = [qkv, z, ba, cs, s0, lw["cqkv"], lw["alog"], lw["dtb"], lw["onorm"]]
    if s_all is None:
        s_spec = pl.BlockSpec((depth, bb, H_A, DK, DV), lambda i, j: (0, i, 0, 0, 0))
        aliases = {}
    else:
        s_spec = pl.BlockSpec((None, bb, H_A, DK, DV), st)
        in_specs.append(pl.BlockSpec(memory_space=pl.ANY))
        operands.append(s_all)
        aliases = {len(operands) - 1: 1}
    return pl.pallas_call(
        functools.partial(_delta_body, tiles=tiles, seg=seg, tt=tt, n_t=n_t,
                          fill_layers=depth if s_all is None else 0),
        grid=(b // bb, n_t),
        in_specs=in_specs,
        out_specs=[tok(D_A), s_spec,
                   pl.BlockSpec((bb, CONV_W - 1, D_QKV), lambda i, j: (i, 0, 0))],
        out_shape=[jax.ShapeDtypeStruct((b, t, D_A), BF16),
                   jax.ShapeDtypeStruct((depth, b, H_A, DK, DV), F32),
                   jax.ShapeDtypeStruct((b, CONV_W - 1, D_QKV), F32)],
        scratch_shapes=[pltpu.VMEM((bb, tt + TAIL, D_QKV), F32)],
        input_output_aliases=aliases,
        compiler_params=pltpu.CompilerParams(dimension_semantics=("arbitrary", "arbitrary"),
                                             vmem_limit_bytes=VMEM_LIMIT),
        name="delta",
    )(*operands)


def _lru_stream(io, cw_ref, cb_ref, wg_ref, bg_ref, lam_ref, *, tt, bb):
    win = io.ext()
    xc = None
    for j in range(CONV_W):
        k = CONV_W - 1 - j
        x = win if k == 0 else pltpu.roll(win, k, 1)
        term = x[:, TAIL:, :] * io.after(cw_ref[j:j + 1, :])
        xc = term if xc is None else xc + term
    xc = xc + cb_ref[...]
    n = bb * tt
    xc = xc.reshape(n, D_R)
    yield
    gates = jnp.dot(xc.astype(BF16), wg_ref[...], preferred_element_type=F32) + bg_ref[...]
    r = _sigmoid(gates[:, :D_R])
    i = _sigmoid(gates[:, D_R:])
    log_a = -C_RG * r * _softplus(-io.after(lam_ref[...]))
    a = jnp.exp(log_a)
    m2 = -jnp.tanh(log_a) * (a * a + 1.0)
    u = jnp.where(m2 > 0.0, m2 * lax.rsqrt(m2), 0.0) * i * xc
    yield

    ng = n // SUBLANES
    a = a.reshape(ng, SUBLANES, D_R)
    u = u.reshape(ng, SUBLANES, D_R)
    tpos = lax.broadcasted_iota(jnp.int32, (ng, SUBLANES, D_R), 1)
    s = 1
    while s < SUBLANES:
        a_sh = pltpu.roll(a, s, 1)
        u_sh = pltpu.roll(u, s, 1)
        m = tpos >= s
        u = jnp.where(m, u + a * u_sh, u)
        a = jnp.where(m, a * a_sh, a)
        s *= 2
    yield
    gpb = tt // SUBLANES
    if gpb == 1:
        hs = u + a * io.h0()
        h_last = hs[:, SUBLANES - 1:SUBLANES, :]
    else:
        assert bb == 1
        h_in = io.h0()[0]
        groups = []
        for g in range(gpb):
            hg = u[g] + a[g] * h_in
            groups.append(hg)
            h_in = hg[SUBLANES - 1:SUBLANES, :]
        hs = jnp.concatenate(groups, axis=0)
        h_last = h_in.reshape(1, 1, D_R)
    y = hs.reshape(n, D_R) * jax.nn.gelu(io.gate(), approximate=True)
    io.put_y(y.reshape(bb, tt, D_R))
    io.put_h(h_last)
    yield


def _lru_body(xr_ref, gate_ref, cs_ref, h0_ref, cw_ref, cb_ref, wg_ref, bg_ref, lam_ref,
              y_ref, hnew_ref, csnew_ref, ext_ref, *, tt, bb, n_t):
    t = pl.program_id(1)

    @pl.when(t == 0)
    def _():
        ext_ref[:, TAIL - (CONV_W - 1):TAIL, :] = cs_ref[...]
        hnew_ref[...] = h0_ref[...]

    ext_ref[:, TAIL:TAIL + tt, :] = xr_ref[...]

    def put_y(val):
        y_ref[...] = val.astype(BF16)

    def put_h(val):
        hnew_ref[...] = val

    io = _Io(ext=lambda: ext_ref[...], gate=lambda: gate_ref[...].reshape(bb * tt, D_R),
             h0=lambda: hnew_ref[...], put_y=put_y, put_h=put_h)
    for _ in _lru_stream(io, cw_ref, cb_ref, wg_ref, bg_ref, lam_ref, tt=tt, bb=bb):
        pass

    tail = ext_ref[:, tt:tt + TAIL, :]
    ext_ref[:, 0:TAIL, :] = tail

    @pl.when(t == n_t - 1)
    def _():
        csnew_ref[...] = ext_ref[:, TAIL - (CONV_W - 1):TAIL, :]


def _lru(xr, gate, cs, h0, lw, l, *, tt, bb):
    b, t, _ = xr.shape
    n_t = t // tt
    assert n_t * tt == t and b % bb == 0 and tt % SUBLANES == 0
    ct = lambda i, j: (l, i, 0, 0)
    tok = pl.BlockSpec((bb, tt, D_R), lambda i, j: (i, j, 0))
    const = lambda shape: pl.BlockSpec(shape, lambda i, j: (0,) * len(shape))
    return pl.pallas_call(
        functools.partial(_lru_body, tt=tt, bb=bb, n_t=n_t),
        grid=(b // bb, n_t),
        in_specs=[tok, tok,
                  pl.BlockSpec((None, bb, CONV_W - 1, D_R), ct),
                  pl.BlockSpec((None, bb, 1, D_R), ct),
                  const((CONV_W, D_R)), const((1, D_R)), const((D_R, 2 * D_R)), const((1, 2 * D_R)),
                  const((1, D_R))],
        out_specs=[tok,
                   pl.BlockSpec((bb, 1, D_R), lambda i, j: (i, 0, 0)),
                   pl.BlockSpec((bb, CONV_W - 1, D_R), lambda i, j: (i, 0, 0))],
        out_shape=[jax.ShapeDtypeStruct((b, t, D_R), BF16),
                   jax.ShapeDtypeStruct((b, 1, D_R), F32),
                   jax.ShapeDtypeStruct((b, CONV_W - 1, D_R), F32)],
        scratch_shapes=[pltpu.VMEM((bb, tt + TAIL, D_R), F32)],
        compiler_params=pltpu.CompilerParams(dimension_semantics=("arbitrary", "arbitrary"),
                                             vmem_limit_bytes=VMEM_LIMIT),
        name="lru",
    )(xr, gate, cs, h0, lw["clw"], lw["clb"], lw["wgate"], lw["bgate"], lw["lam"])


def _interleave(streams, plan, lead, set_dep):
    live = dict(streams)

    def step(key):
        if key not in live:
            return
        try:
            val = next(live[key])
            if key == lead:
                set_dep(val)
        except StopIteration:
            del live[key]

    for key in plan:
        step(key)
    while live:
        for key in list(live):
            step(key)


def _in_mix_body(flag_ref, x_ref, n1_ref, wg_ref, wu_ref, wd_ref, nm_ref, wina_ref, winb_ref,
                 dcs_ref, ds0_ref, cw_ref, alog_ref, dtb_ref, on_ref,
                 lcs_ref, lh0_ref, lcw_ref, lcb_ref, lwg_ref, lbg_ref, lam_ref,
                 x1_ref, o_ref, y_ref, snew_ref, dcsnew_ref, hnew_ref, lcsnew_ref,
                 qkv_scr, z_scr, ba_scr, xr_scr, gate_scr, dtail_scr, ltail_scr, s_scr, h_scr,
                 *, tiles_per_seq):
    tt = MIX_TILE
    s = pl.program_id(0)
    slot_a = lax.rem(s, 2)
    slot_b = 1 - slot_a
    first = lax.rem(s + tiles_per_seq - 1, tiles_per_seq) == 0
    tails = slice(TAIL - (CONV_W - 1), TAIL)
    keep = flag_ref[0] == 1
    dep = [None]

    def after(x):
        if dep[0] is None:
            return x
        reps = x.shape[-1] // DK
        d = dep[0] if reps == 1 else jnp.concatenate([dep[0]] * reps, axis=-1)
        return jnp.where(keep, x, d)

    def set_dep(val):
        dep[0] = val

    @pl.when(s == 0)
    def _():
        for ref in (qkv_scr, z_scr, ba_scr, xr_scr, gate_scr, dtail_scr, ltail_scr, s_scr, h_scr):
            ref[...] = jnp.zeros(ref.shape, F32)

    def stream_a():
        x = x_ref[...]
        h = _rms(x, n1_ref[...]).astype(BF16)
        pieces = _ffn_pieces(h, wg_ref, wu_ref, wd_ref)
        for _ in range(D_FF // FF_PIECE):
            yield next(pieces)[0:1, 0:DK]
        acc = next(pieces)
        x1 = x + 0.5 * acc
        x1_ref[...] = x1
        hm = _rms(x1, nm_ref[...]).astype(BF16)
        yield acc[0:1, 0:DK]
        dsts = ((qkv_scr, TAIL, wina_ref, 0, D_QKV), (z_scr, 0, wina_ref, D_QKV, D_A),
                (xr_scr, TAIL, winb_ref, 0, D_R), (gate_scr, 0, winb_ref, D_R, D_R),
                (ba_scr, 0, winb_ref, 2 * D_R, D_BA))
        for dst, row0, w_ref, col0, width in dsts:
            for d0 in range(0, width, PROJ_PIECE):
                w = min(PROJ_PIECE, width - d0)
                p = jnp.dot(hm, w_ref[:, col0 + d0:col0 + d0 + w], preferred_element_type=F32)
                dst[slot_a, row0:row0 + tt, d0:d0 + w] = p
                yield p[0:1, 0:DK]

    qkv_scr[slot_b, tails, :] = jnp.where(first, dcs_ref[0], dtail_scr[tails, :])
    xr_scr[slot_b, tails, :] = jnp.where(first, lcs_ref[0], ltail_scr[tails, :])

    def put_o(tile, h, val):
        o_ref[tile[2]:tile[2] + CHUNK, h * DV:(h + 1) * DV] = val.astype(BF16)

    def put_state(b, h, val):
        s_scr[h] = val
        snew_ref[0, h] = val

    def put_y(val):
        y_ref[...] = val.reshape(tt, D_R).astype(BF16)

    def put_h(val):
        h_scr[...] = val
        hnew_ref[...] = val

    dio = _Io(ext=lambda tile, col0, w: qkv_scr[slot_b, tile[2]:tile[2] + TAIL + CHUNK, col0:col0 + w][None],
              ba=lambda tile: ba_scr[slot_b, tile[2]:tile[2] + CHUNK, :],
              z=lambda tile, col0: z_scr[slot_b, tile[2]:tile[2] + CHUNK, col0:col0 + DV],
              put_o=put_o, get_state=lambda b, h: jnp.where(first, ds0_ref[0, h], s_scr[h]),
              put_state=put_state, after=after)
    lio = _Io(ext=lambda: xr_scr[slot_b][None],
              gate=lambda: gate_scr[slot_b], h0=lambda: jnp.where(first, lh0_ref[...], h_scr[...]),
              put_y=put_y, put_h=put_h, after=after)
    tiles = tuple((0, 1, j * CHUNK) for j in range(tt // CHUNK))
    _interleave({"a": stream_a(),
                 "d": _delta_stream(dio, cw_ref, alog_ref, dtb_ref, on_ref, tiles=tiles, seg=CHUNK),
                 "l": _lru_stream(lio, lcw_ref, lcb_ref, lwg_ref, lbg_ref, lam_ref, tt=tt, bb=1)},
                MIX_PLAN, "a", set_dep)

    dtail_scr[...] = qkv_scr[slot_b, tt:tt + TAIL, :]
    ltail_scr[...] = xr_scr[slot_b, tt:tt + TAIL, :]
    dcsnew_ref[0] = qkv_scr[slot_b, tt + TAIL - (CONV_W - 1):tt + TAIL, :]
    lcsnew_ref[0] = xr_scr[slot_b, tt + TAIL - (CONV_W - 1):tt + TAIL, :]


def _in_mix(x, sdc, sd, slc, sl, lw, l, *, batch):
    r = x.shape[0]
    tt = MIX_TILE
    n_tiles = r // tt
    tiles_per_seq = n_tiles // batch
    assert n_tiles * tt == r and tiles_per_seq * batch == n_tiles
    cur = lambda s: (jnp.minimum(s, n_tiles - 1), 0)
    prev = lambda s: (jnp.maximum(s - 1, 0), 0)
    seq3 = lambda s: (jnp.maximum(s - 1, 0) // tiles_per_seq, 0, 0)
    seq4 = lambda s: (jnp.maximum(s - 1, 0) // tiles_per_seq, 0, 0, 0)
    layer = lambda shape: pl.BlockSpec((None,) + shape, lambda s: (l,) + (0,) * len(shape),
                                       pipeline_mode=pl.Buffered(1))
    return pl.pallas_call(
        functools.partial(_in_mix_body, tiles_per_seq=tiles_per_seq),
        grid=(n_tiles + 1,),
        in_specs=[pl.BlockSpec(memory_space=pltpu.SMEM),
                  pl.BlockSpec((tt, D_MODEL), cur), _resident((1, D_MODEL)),
                  _resident((D_MODEL, D_FF)), _resident((D_MODEL, D_FF)), _resident((D_FF, D_MODEL)),
                  _resident((1, D_MODEL)), _resident((D_MODEL, D_PROJ_A)), _resident((D_MODEL, D_PROJ_B)),
                  layer((1, CONV_W - 1, D_QKV)), layer((1, H_A, DK, DV)),
                  _resident((CONV_W, D_QKV)), _resident((1, D_BA)), _resident((1, D_BA)), _resident((1, DV)),
                  layer((1, CONV_W - 1, D_R)), layer((1, 1, D_R)),
                  _resident((CONV_W, D_R)), _resident((1, D_R)), _resident((D_R, 2 * D_R)),
                  _resident((1, 2 * D_R)), _resident((1, D_R))],
        out_specs=[pl.BlockSpec((tt, D_MODEL), cur), pl.BlockSpec((tt, D_A), prev), pl.BlockSpec((tt, D_R), prev),
                   pl.BlockSpec((1, H_A, DK, DV), seq4), pl.BlockSpec((1, CONV_W - 1, D_QKV), seq3),
                   pl.BlockSpec((1, 1, D_R), seq3), pl.BlockSpec((1, CONV_W - 1, D_R), seq3)],
        out_shape=[jax.ShapeDtypeStruct((r, D_MODEL), F32), jax.ShapeDtypeStruct((r, D_A), BF16),
                   jax.ShapeDtypeStruct((r, D_R), BF16),
                   jax.ShapeDtypeStruct((batch, H_A, DK, DV), F32),
                   jax.ShapeDtypeStruct((batch, CONV_W - 1, D_QKV), F32),
                   jax.ShapeDtypeStruct((batch, 1, D_R), F32),
                   jax.ShapeDtypeStruct((batch, CONV_W - 1, D_R), F32)],
        scratch_shapes=[pltpu.VMEM((2, TAIL + tt, D_QKV), F32), pltpu.VMEM((2, tt, D_A), F32),
                        pltpu.VMEM((2, tt, D_BA), F32), pltpu.VMEM((2, TAIL + tt, D_R), F32),
                        pltpu.VMEM((2, tt, D_R), F32), pltpu.VMEM((TAIL, D_QKV), F32),
                        pltpu.VMEM((TAIL, D_R), F32), pltpu.VMEM((H_A, DK, DV), F32),
                        pltpu.VMEM((1, 1, D_R), F32)],
        compiler_params=pltpu.CompilerParams(dimension_semantics=("arbitrary",),
                                             vmem_limit_bytes=VMEM_LIMIT),
        name="in_mix",
    )(jnp.ones((1,), jnp.int32), x, lw["n1"], lw["f1g"], lw["f1u"], lw["f1d"], lw["nm"], lw["w_in_a"], lw["w_in_b"],
      sdc, sd, lw["cqkv"], lw["alog"], lw["dtb"], lw["onorm"],
      slc, sl, lw["clw"], lw["clb"], lw["wgate"], lw["bgate"], lw["lam"])


BIG_WEIGHTS = ("f1g", "f1u", "f1d", "w_in_a", "w_out", "f2g", "f2u", "f2d")


def _prep_layer(l, w, big):
    row = lambda v: v.reshape(1, -1).astype(F32)
    lane_pad = lambda v: jnp.zeros((1, D_BA), F32).at[0, H_A:2 * H_A].set(v)
    eye = jnp.eye(NB_R, dtype=F32)
    bd = lambda wb: jnp.einsum("ncd,nm->ncmd", wb, eye).reshape(D_R, D_R)
    w_in = w["w_in"][l]
    c4 = D_PROJ_A + 2 * H_A
    pad = jnp.zeros((D_MODEL, D_BA - 2 * H_A), w_in.dtype)
    w_in_b = jnp.concatenate([w_in[:, c4:], w_in[:, D_PROJ_A:c4], pad], axis=-1).astype(BF16)
    return dict(
        n1=row(w["norm_ffn1"][l]), f1g=big["f1g"], f1u=big["f1u"], f1d=big["f1d"],
        nm=row(w["norm_mix"][l]), w_in_a=big["w_in_a"], w_in_b=w_in_b,
        cqkv=w["conv_qkv"][l], alog=lane_pad(w["a_log"][l]), dtb=lane_pad(w["dt_bias"][l]),
        onorm=row(w["norm_delta_out"][l]),
        clw=w["conv_lru_w"][l], clb=row(w["conv_lru_b"][l]),
        wgate=jnp.concatenate([bd(w["w_rgate"][l]), bd(w["w_igate"][l])], axis=1).astype(BF16),
        bgate=jnp.concatenate([row(w["b_rgate"][l]), row(w["b_igate"][l])], axis=1),
        lam=row(w["lru_lambda"][l]),
        w_out=big["w_out"], n2=row(w["norm_ffn2"][l]), f2g=big["f2g"], f2u=big["f2u"], f2d=big["f2d"],
    )


def _group_layers(x, states_at, layers, nf, out, *, seg=None, delta_tt=None, delta_bb=None, tiles=None,
                  lru_tt=None, lru_bb=None, fused=False, states_only=False, cast_next=None):
    b, t, _ = x.shape
    xf = x.reshape(b * t, D_MODEL)
    s_all = None
    n_layers = len(layers)
    for l in range(n_layers):
        lw = layers[l]
        sd, sdc, sl, slc = states_at(l)
        if fused:
            x1, o, y, s_new, cs_new, h_new, lcs_new = _in_mix(xf, sdc, sd, slc, sl, lw, l, batch=b)
            out["nd"].append(s_new)
        else:
            x1, qkv, z, xr, gate, ba = _ffn_in(xf, lw)
            r3 = lambda v: v.reshape(b, t, v.shape[-1])
            o, s_all, cs_new = _delta(r3(qkv), r3(z), r3(ba), sdc, sd, lw, l, s_all,
                                      seg=seg, tt=delta_tt, bb=delta_bb, tiles=tiles)
            y, h_new, lcs_new = _lru(r3(xr), r3(gate), slc, sl, lw, l, tt=lru_tt, bb=lru_bb)
            out["s_all"] = s_all
        last = l == n_layers - 1
        if not (last and states_only):
            o2, y2 = o.reshape(b * t, D_A), y.reshape(b * t, D_R)
            if fused:
                cast = cast_next(l) if cast_next is not None and not last else ()
                xf, copies = _out_ffn(x1, o2, y2, lw, nf, last, cast)
                if cast:
                    out["cast"][l + 1] = dict(zip(BIG_WEIGHTS, copies))
            else:
                xf = _out_ffn_stream(x1, o2, y2, lw, nf, last)
        out["ndc"].append(cs_new)
        out["nl"].append(h_new)
        out["nlc"].append(lcs_new)
        if last and not states_only:
            out["y"] = xf.reshape(b, t, D_MODEL)
        yield


def kernel(x_prompt, x_sample, state_delta, state_delta_conv, state_lru, state_lru_conv, meta_tokens, norm_ffn1, w_ffn1_gate, w_ffn1_up, w_ffn1_down, norm_mix, w_in, conv_qkv, a_log, dt_bias, norm_delta_out, conv_lru_w, conv_lru_b, w_rgate, b_rgate, w_igate, b_igate, lru_lambda, w_out, norm_ffn2, w_ffn2_gate, w_ffn2_up, w_ffn2_down, norm_final):
    w = dict(norm_ffn1=norm_ffn1, w_ffn1_gate=w_ffn1_gate, w_ffn1_up=w_ffn1_up, w_ffn1_down=w_ffn1_down,
             norm_mix=norm_mix, w_in=w_in, conv_qkv=conv_qkv, a_log=a_log, dt_bias=dt_bias,
             norm_delta_out=norm_delta_out, conv_lru_w=conv_lru_w, conv_lru_b=conv_lru_b,
             w_rgate=w_rgate, b_rgate=b_rgate, w_igate=w_igate, b_igate=b_igate, lru_lambda=lru_lambda,
             w_out=w_out, norm_ffn2=norm_ffn2, w_ffn2_gate=w_ffn2_gate, w_ffn2_up=w_ffn2_up,
             w_ffn2_down=w_ffn2_down)
    depth = norm_ffn1.shape[0]
    big_f32 = dict(f1g=(w_ffn1_gate, D_FF), f1u=(w_ffn1_up, D_FF), f1d=(w_ffn1_down, D_MODEL),
                   w_in_a=(w_in, D_PROJ_A), w_out=(w_out, D_MODEL),
                   f2g=(w_ffn2_gate, D_FF), f2u=(w_ffn2_up, D_FF), f2d=(w_ffn2_down, D_MODEL))
    layers = [None] * depth
    layers[0] = _prep_layer(0, w, {k: big_f32[k][0][0, :, :big_f32[k][1]].astype(BF16) for k in BIG_WEIGHTS})
    cast_next = lambda l: tuple((big_f32[k][0], l + 1, big_f32[k][1]) for k in BIG_WEIGHTS)
    nf = norm_final.reshape(1, D_MODEL).astype(F32)
    bp, seq, _ = x_prompt.shape
    bs, dseq, _ = x_sample.shape
    seg_b = CHUNK // dseq
    new_out = lambda: dict(nd=[], ndc=[], nl=[], nlc=[], cast={}, s_all=None, y=None)
    stack = jnp.stack

    zeros = lambda *s: jnp.zeros((depth,) + s, F32)
    m_states = (zeros(1, H_A, DK, DV), zeros(1, CONV_W - 1, D_QKV), zeros(1, 1, D_R), zeros(1, CONV_W - 1, D_R))
    m_out, p_out, s_out = new_out(), new_out(), new_out()
    meta = _group_layers(meta_tokens.astype(F32)[None], lambda l: m_states, layers, nf, m_out,
                         seg=N_META, delta_tt=N_META, delta_bb=1, tiles=((0, 1, 0),), lru_tt=N_META, lru_bb=1,
                         states_only=True)
    after_meta = lambda l: (m_out["s_all"], stack(m_out["ndc"]), stack(m_out["nl"]), stack(m_out["nlc"]))
    prompt = _group_layers(x_prompt, after_meta, layers, nf, p_out, fused=True, cast_next=cast_next)
    s_states = (state_delta, state_delta_conv, state_lru.reshape(depth, bs, 1, D_R), state_lru_conv)
    sample = _group_layers(x_sample, lambda l: s_states, layers, nf, s_out,
                           seg=dseq, delta_tt=dseq, delta_bb=SAMPLE_TILES_PER_STEP * seg_b,
                           tiles=tuple((j * seg_b, seg_b, 0) for j in range(SAMPLE_TILES_PER_STEP)),
                           lru_tt=dseq, lru_bb=2 * seg_b)
    for l in range(depth):
        next(meta)
        next(prompt)
        if l + 1 < depth:
            layers[l + 1] = _prep_layer(l + 1, w, p_out["cast"][l + 1])
        next(sample)

    return (p_out["y"], s_out["y"], stack(p_out["nd"]), stack(p_out["ndc"]),
            stack(p_out["nl"]).reshape(depth, bp, D_R), stack(p_out["nlc"]),
            s_out["s_all"], stack(s_out["ndc"]), stack(s_out["nl"]).reshape(depth, bs, D_R), stack(s_out["nlc"]))
```

```python
import functools
import math

import jax
import jax.numpy as jnp
from jax import lax
from jax.experimental import pallas as pl
from jax.experimental.pallas import tpu as pltpu

F32 = jnp.float32
BF16 = jnp.bfloat16

D_MODEL = 1024
D_FF = 2816
D_A = 512
D_R = 512
H_A = 4
DK = 128
DV = 128
CHUNK = 64
CONV_W = 4
NB_R = 8
C_RG = 8.0
EPS = 1e-6
N_META = 16
D_QKV = 3 * D_A
D_BA = 128
D_PROJ_A = D_QKV + D_A
D_PROJ_B = 2 * D_R + D_BA

SUBLANES = 8
BF16_ROWS = 16
TAIL = SUBLANES
VMEM_BYTES_V7X = 64 * 1024 * 1024
VMEM_LIMIT = VMEM_BYTES_V7X - 8 * 1024 * 1024
ROW_TILE = 512
MIX_TILE = 256
FF_PIECE = 256
PROJ_PIECE = 256
MIX_PLAN = "ad" * 7 + ("adl" + "ad") * 3 + "adl" + "ad" * 8
SAMPLE_TILES_PER_STEP = 2


def _rms(x, w):
    ms = jnp.mean(x * x, axis=-1, keepdims=True)
    return x * lax.rsqrt(ms + EPS) * w


def _sigmoid(x):
    return 1.0 / (1.0 + jnp.exp(-x))


def _silu(x):
    return x * _sigmoid(x)


def _softplus(x):
    t = jnp.exp(-jnp.abs(x))
    u = 1.0 + t
    d = u - 1.0
    log1p_t = jnp.where(d == 0.0, t, jnp.log(u) * (t / jnp.where(d == 0.0, 1.0, d)))
    return jnp.maximum(x, 0.0) + log1p_t


def _dot(a, b):
    return jnp.dot(a.astype(BF16), b.astype(BF16), preferred_element_type=F32)


def _dot_nt(a, b):
    return lax.dot_general(a.astype(BF16), b.astype(BF16), (((1,), (1,)), ((), ())),
                           preferred_element_type=F32)


def _dot_tn(a, b):
    return lax.dot_general(a.astype(BF16), b.astype(BF16), (((0,), (0,)), ((), ())),
                           preferred_element_type=F32)


def _split3(a):
    a1 = a.astype(BF16)
    r = a - a1.astype(F32)
    a2 = r.astype(BF16)
    a3 = (r - a2.astype(F32)).astype(BF16)
    return a1, a2, a3


def _ffn_pieces(h, wg_ref, wu_ref, wd_ref):
    def down(gu, acc):
        g, u, c0 = gu
        a = (_silu(g) * u).astype(BF16)
        d = jnp.dot(a, wd_ref[c0:c0 + FF_PIECE, :], preferred_element_type=F32)
        return d if acc is None else acc + d

    acc, pending = None, None
    for c0 in range(0, D_FF, FF_PIECE):
        g = jnp.dot(h, wg_ref[:, c0:c0 + FF_PIECE], preferred_element_type=F32)
        u = jnp.dot(h, wu_ref[:, c0:c0 + FF_PIECE], preferred_element_type=F32)
        if pending is not None:
            acc = down(pending, acc)
        pending = (g, u, c0)
        yield g
    yield down(pending, acc)


def _ffn(x, nw, wg_ref, wu_ref, wd_ref):
    h = _rms(x, nw).astype(BF16)
    for acc in _ffn_pieces(h, wg_ref, wu_ref, wd_ref):
        pass
    return x + 0.5 * acc


def _ffn_in_body(x_ref, n1_ref, wg_ref, wu_ref, wd_ref, nm_ref, wina_ref, winb_ref,
                 x1_ref, qkv_ref, z_ref, xr_ref, gate_ref, ba_ref):
    x1 = _ffn(x_ref[...], n1_ref[...], wg_ref, wu_ref, wd_ref)
    x1_ref[...] = x1
    hm = _rms(x1, nm_ref[...]).astype(BF16)
    pa = jnp.dot(hm, wina_ref[...], preferred_element_type=F32)
    pb = jnp.dot(hm, winb_ref[...], preferred_element_type=F32)
    qkv_ref[...] = pa[:, :D_QKV]
    z_ref[...] = pa[:, D_QKV:]
    xr_ref[...] = pb[:, :D_R]
    gate_ref[...] = pb[:, D_R:2 * D_R]
    ba_ref[...] = pb[:, 2 * D_R:]


def _out_ffn_body(x_ref, o_ref, y_ref, wo1_ref, wo2_ref, n2_ref, wg_ref, wu_ref, wd_ref, nf_ref, *rest,
                  final, n_cast):
    src_refs, out_ref, dst_refs = rest[:n_cast], rest[n_cast], rest[n_cast + 1:]
    x = x_ref[...] + (_dot(o_ref[...], wo1_ref[...]) + _dot(y_ref[...], wo2_ref[...]))
    x2 = _ffn(x, n2_ref[...], wg_ref, wu_ref, wd_ref)
    if final:
        x2 = _rms(x2, nf_ref[...])
    out_ref[...] = x2
    for s_ref, d_ref in zip(src_refs, dst_refs):
        d_ref[...] = s_ref[...].astype(BF16)


def _resident(shape):
    nd = len(shape)
    return pl.BlockSpec(shape, lambda *_: (0,) * nd, pipeline_mode=pl.Buffered(1))


def _row_block(shape, first):
    return pl.BlockSpec(shape, lambda *_: (first, 0), pipeline_mode=pl.Buffered(1))


def _rows(tm, width):
    return pl.BlockSpec((tm, width), lambda i: (i, 0))


def _ffn_in(x, lw):
    r = x.shape[0]
    tm = min(ROW_TILE, r)
    widths = (D_MODEL, D_QKV, D_A, D_R, D_R, D_BA)
    return pl.pallas_call(
        _ffn_in_body,
        grid=(pl.cdiv(r, tm),),
        in_specs=[_rows(tm, D_MODEL), _resident((1, D_MODEL)),
                  _resident((D_MODEL, D_FF)), _resident((D_MODEL, D_FF)), _resident((D_FF, D_MODEL)),
                  _resident((1, D_MODEL)), _resident((D_MODEL, D_PROJ_A)), _resident((D_MODEL, D_PROJ_B))],
        out_specs=[_rows(tm, w) for w in widths],
        out_shape=[jax.ShapeDtypeStruct((r, w), F32) for w in widths],
        compiler_params=pltpu.CompilerParams(dimension_semantics=("parallel",),
                                             vmem_limit_bytes=VMEM_LIMIT),
        name="ffn_in",
    )(x, lw["n1"], lw["f1g"], lw["f1u"], lw["f1d"], lw["nm"], lw["w_in_a"], lw["w_in_b"])


def _out_ffn(x, o, y, lw, nf, final, cast=()):
    r = x.shape[0]
    tm = min(ROW_TILE, r)
    steps = pl.cdiv(r, tm)
    cast_in, cast_out, cast_shape = [], [], []
    for arr, layer, cols in cast:
        rows = arr.shape[1]
        cr = BF16_ROWS * pl.cdiv(pl.cdiv(rows, steps), BF16_ROWS)
        last = pl.cdiv(rows, cr) - 1
        cast_in.append(pl.BlockSpec((None, cr, cols), lambda i, layer=layer, last=last: (layer, jnp.minimum(i, last), 0)))
        cast_out.append(pl.BlockSpec((cr, cols), lambda i, last=last: (jnp.minimum(i, last), 0)))
        cast_shape.append(jax.ShapeDtypeStruct((rows, cols), BF16))
    res = pl.pallas_call(
        functools.partial(_out_ffn_body, final=final, n_cast=len(cast)),
        grid=(steps,),
        in_specs=[_rows(tm, D_MODEL), _rows(tm, D_A), _rows(tm, D_R),
                  _row_block((D_A, D_MODEL), 0), _row_block((D_R, D_MODEL), 1), _resident((1, D_MODEL)),
                  _resident((D_MODEL, D_FF)), _resident((D_MODEL, D_FF)), _resident((D_FF, D_MODEL)),
                  _resident((1, D_MODEL))] + cast_in,
        out_specs=[_rows(tm, D_MODEL)] + cast_out,
        out_shape=[jax.ShapeDtypeStruct((r, D_MODEL), F32)] + cast_shape,
        compiler_params=pltpu.CompilerParams(dimension_semantics=("arbitrary",),
                                             vmem_limit_bytes=VMEM_LIMIT),
        name="out_ffn",
    )(x, o, y, lw["w_out"], lw["w_out"], lw["n2"], lw["f2g"], lw["f2u"], lw["f2d"], nf, *[c[0] for c in cast])
    return res[0], res[1:]


class _Io:
    def __init__(self, **fns):
        self.after = lambda x: x
        self.__dict__.update(fns)


def _delta_stream(io, cw_ref, alog_ref, dtb_ref, on_ref, *, tiles, seg):
    c = seg
    nb = tiles[0][1]
    r = nb * c
    levels = int(math.log2(c))
    assert 2 ** levels == c
    row = lax.broadcasted_iota(jnp.int32, (r, r), 0)
    col = lax.broadcasted_iota(jnp.int32, (r, r), 1)
    same = lax.shift_right_logical(row, levels) == lax.shift_right_logical(col, levels)
    causal = same & (row >= col)
    strict = same & (row > col)
    tri = causal.astype(BF16)
    ones_seg = same.astype(BF16)
    eye = (row == col).astype(F32)
    sel_r = lax.broadcasted_iota(jnp.int32, (SUBLANES, D_BA), 0)
    sel_c = lax.broadcasted_iota(jnp.int32, (SUBLANES, D_BA), 1)
    sel = ((sel_c == sel_r + H_A) & (sel_r < H_A)).astype(BF16)

    def conv_silu(tile, col0):
        win = io.ext(tile, col0, DK)
        acc = None
        for j in range(CONV_W):
            k = CONV_W - 1 - j
            x = win if k == 0 else pltpu.roll(win, k, 1)
            term = x[:, TAIL:, :].reshape(r, DK) * io.after(cw_ref[j:j + 1, col0:col0 + DK])
            acc = term if acc is None else acc + term
        return _silu(acc)

    def exact_dot(lhs01, x, nt=False):
        out = None
        for p in _split3(x):
            if nt:
                d = lax.dot_general(lhs01, p, (((1,), (1,)), ((), ())), preferred_element_type=F32)
            else:
                d = jnp.dot(lhs01, p, preferred_element_type=F32)
            out = d if out is None else out + d
        return out

    items = [(j, h) for j in range(len(tiles)) for h in range(H_A)]

    gcum, gtot, grow, eg_all, beta_all = [], [], [], [], []
    for tile in tiles:
        ba = io.ba(tile)
        g_all = -jnp.exp(io.after(alog_ref[...])) * _softplus(ba + dtb_ref[...])
        beta_all.append(_sigmoid(ba))
        gcum.append(exact_dot(tri, g_all))
        gtot.append(exact_dot(ones_seg, g_all) if nb > 1 else gcum[-1][r - 1:r, :])
    yield
    for j in range(len(tiles)):
        grow.append(exact_dot(sel, gcum[j], nt=True))
        eg_all.append(jnp.exp(gcum[j]))
    yield

    qg, kn, kb, kdec, decay, rhs = {}, {}, {}, {}, {}, {}
    for (j, h) in items:
        tile = tiles[j]
        qh = conv_silu(tile, h * DK)
        kh = conv_silu(tile, D_A + h * DK)
        vh = conv_silu(tile, 2 * D_A + h * DV)
        qn = qh * lax.rsqrt(jnp.sum(qh * qh, axis=-1, keepdims=True) + EPS) * (DK ** -0.5)
        k_n = kh * lax.rsqrt(jnp.sum(kh * kh, axis=-1, keepdims=True) + EPS)
        beta = beta_all[j][:, h:h + 1]
        gcol = gcum[j][:, H_A + h:H_A + h + 1]
        egcol = eg_all[j][:, H_A + h:H_A + h + 1]
        glast = gtot[j][:, H_A + h:H_A + h + 1]
        dmat = gcol - grow[j][h:h + 1, :]
        decay[j, h] = jnp.where(causal, jnp.exp(jnp.where(causal, dmat, 0.0)), 0.0)
        kn[j, h] = k_n
        kb[j, h] = k_n * beta
        qg[j, h] = (qn, qn * egcol)
        kdec[j, h] = k_n * jnp.exp(glast - gcol)
        rhs[j, h] = jnp.concatenate([vh * beta, kb[j, h] * egcol], axis=-1)
        if h == H_A - 1:
            yield

    kq = {it: _dot_nt(jnp.concatenate([kb[it], qg[it][0]], axis=0), kn[it]) for it in items}
    lmat = {it: jnp.where(strict, kq[it][:r] * decay[it], 0.0) for it in items}
    attn = {it: kq[it][r:] * decay[it] for it in items}
    yield

    tinv = {it: eye - lmat[it] for it in items}
    m = {it: _dot(lmat[it], lmat[it]) for it in items}
    yield
    for _ in range(levels - 2):
        tinv = {it: tinv[it] + _dot(tinv[it], m[it]) for it in items}
        m = {it: _dot(m[it], m[it]) for it in items}
        yield
    tinv = {it: tinv[it] + _dot(tinv[it], m[it]) for it in items}
    yield
    sol = {it: _dot(tinv[it], rhs[it]) for it in items}
    yield

    state = {}

    def get_state(b, h):
        if (b, h) not in state:
            state[b, h] = io.get_state(b, h)
        return state[b, h]

    for j, tile in enumerate(tiles):
        b0 = tile[0]
        segs = [(s, slice(s * c, (s + 1) * c)) for s in range(nb)]
        ws = {}
        for h in range(H_A):
            for s, rs in segs:
                lhs = jnp.concatenate([sol[j, h][rs, DV:], qg[j, h][1][rs]], axis=0)
                ws[h, s] = _dot(lhs, get_state(b0 + s, h))
        yield
        v_new, av = {}, {}
        for h in range(H_A):
            w_s = jnp.concatenate([ws[h, s][:c] for s, _ in segs], axis=0) if nb > 1 else ws[h, 0][:c]
            v_new[h] = sol[j, h][:, :DV] - w_s
        for h in range(H_A):
            av[h] = _dot(attn[j, h], v_new[h])
        for h in range(H_A):
            for s, rs in segs:
                g0 = rs.start if nb > 1 else 0
                scale = jnp.exp(gtot[j][g0:g0 + 1, H_A + h:H_A + h + 1])
                state[b0 + s, h] = state[b0 + s, h] * scale + _dot_tn(kdec[j, h][rs], v_new[h][rs])
        for h in range(H_A):
            q_s = jnp.concatenate([ws[h, s][c:] for s, _ in segs], axis=0) if nb > 1 else ws[h, 0][c:]
            on = _rms(q_s + av[h], io.after(on_ref[...]))
            io.put_o(tile, h, on * _silu(io.z(tile, h * DV)))
        if j == len(tiles) - 1:
            for (b, h), val in state.items():
                io.put_state(b, h, val)
        yield


def _delta_body(qkv_ref, z_ref, ba_ref, cs_ref, s0_ref, cw_ref, alog_ref, dtb_ref, on_ref, *rest,
                tiles, seg, tt, n_t, aliased):
    o_ref, snew_ref, csnew_ref, ext_ref = rest[1:] if aliased else rest
    c = seg
    nb = tiles[0][1]
    r = nb * c
    t = pl.program_id(1)

    @pl.when(t == 0)
    def _():
        ext_ref[:, TAIL - (CONV_W - 1):TAIL, :] = cs_ref[...]
        snew_ref[...] = s0_ref[...]

    ext_ref[:, TAIL:TAIL + tt, :] = qkv_ref[...]

    def rows(ref, tile, off, col0, width):
        b0, _, t0 = tile
        return ref[b0:b0 + nb, off + t0:off + t0 + c, col0:col0 + width].reshape(r, width)

    def put_o(tile, h, val):
        b0, _, t0 = tile
        o_ref[b0:b0 + nb, t0:t0 + c, h * DV:(h + 1) * DV] = val.reshape(nb, c, DV).astype(BF16)

    def put_state(b, h, val):
        snew_ref[b, h] = val

    def ext(tile, col0, w):
        b0, _, t0 = tile
        return ext_ref[b0:b0 + nb, t0:t0 + TAIL + c, col0:col0 + w]

    io = _Io(ext=ext, ba=lambda tile: rows(ba_ref, tile, 0, 0, D_BA),
             z=lambda tile, col0: rows(z_ref, tile, 0, col0, DV),
             put_o=put_o, get_state=lambda b, h: snew_ref[b, h], put_state=put_state)
    for _ in _delta_stream(io, cw_ref, alog_ref, dtb_ref, on_ref, tiles=tiles, seg=seg):
        pass

    tail = ext_ref[:, tt:tt + TAIL, :]
    ext_ref[:, 0:TAIL, :] = tail

    @pl.when(t == n_t - 1)
    def _():
        csnew_ref[...] = ext_ref[:, TAIL - (CONV_W - 1):TAIL, :]


def _delta(qkv, z, ba, cs, s0, lw, l, s_all, *, seg, tt, bb, tiles):
    b, t, _ = qkv.shape
    depth = s0.shape[0]
    n_t = t // tt
    assert n_t * tt == t and b % bb == 0 and (l == 0) == (s_all is None)
    st = lambda i, j: (l, i, 0, 0, 0)
    ct = lambda i, j: (l, i, 0, 0)
    tok = lambda w: pl.BlockSpec((bb, tt, w), lambda i, j: (i, j, 0))
    const = lambda shape: pl.BlockSpec(shape, lambda i, j: (0,) * len(shape))
    in_specs = [tok(D_QKV), tok(D_A), tok(D_BA),
                pl.BlockSpec((None, bb, CONV_W - 1, D_QKV), ct),
                pl.BlockSpec((None, bb, H_A, DK, DV), st),
                const((CONV_W, D_QKV)), const((1, D_BA)), const((1, D_BA)), const((1, DV))]
    operands = [qkv, z, ba, cs, s0, lw["cqkv"], lw["alog"], lw["dtb"], lw["onorm"]]
    s_spec = pl.BlockSpec((None, bb, H_A, DK, DV), st)
    aliases = {}
    if s_all is not None:
        in_specs.append(pl.BlockSpec(memory_space=pl.ANY))
        operands.append(s_all)
        aliases = {len(operands) - 1: 1}
    return pl.pallas_call(
        functools.partial(_delta_body, tiles=tiles, seg=seg, tt=tt, n_t=n_t, aliased=s_all is not None),
        grid=(b // bb, n_t),
        in_specs=in_specs,
        out_specs=[tok(D_A), s_spec,
                   pl.BlockSpec((bb, CONV_W - 1, D_QKV), lambda i, j: (i, 0, 0))],
        out_shape=[jax.ShapeDtypeStruct((b, t, D_A), BF16),
                   jax.ShapeDtypeStruct((depth, b, H_A, DK, DV), F32),
                   jax.ShapeDtypeStruct((b, CONV_W - 1, D_QKV), F32)],
        scratch_shapes=[pltpu.VMEM((bb, tt + TAIL, D_QKV), F32)],
        input_output_aliases=aliases,
        compiler_params=pltpu.CompilerParams(dimension_semantics=("arbitrary", "arbitrary"),
                                             vmem_limit_bytes=VMEM_LIMIT),
        name="delta",
    )(*operands)


def _lru_stream(io, cw_ref, cb_ref, wg_ref, bg_ref, lam_ref, *, tt, bb):
    win = io.ext()
    xc = None
    for j in range(CONV_W):
        k = CONV_W - 1 - j
        x = win if k == 0 else pltpu.roll(win, k, 1)
        term = x[:, TAIL:, :] * io.after(cw_ref[j:j + 1, :])
        xc = term if xc is None else xc + term
    xc = xc + cb_ref[...]
    n = bb * tt
    xc = xc.reshape(n, D_R)
    yield
    gates = jnp.dot(xc.astype(BF16), wg_ref[...], preferred_element_type=F32) + bg_ref[...]
    r = _sigmoid(gates[:, :D_R])
    i = _sigmoid(gates[:, D_R:])
    log_a = -C_RG * r * _softplus(-io.after(lam_ref[...]))
    a = jnp.exp(log_a)
    m2 = -jnp.tanh(log_a) * (a * a + 1.0)
    u = jnp.where(m2 > 0.0, m2 * lax.rsqrt(m2), 0.0) * i * xc
    yield

    ng = n // SUBLANES
    a = a.reshape(ng, SUBLANES, D_R)
    u = u.reshape(ng, SUBLANES, D_R)
    tpos = lax.broadcasted_iota(jnp.int32, (ng, SUBLANES, D_R), 1)
    s = 1
    while s < SUBLANES:
        a_sh = pltpu.roll(a, s, 1)
        u_sh = pltpu.roll(u, s, 1)
        m = tpos >= s
        u = jnp.where(m, u + a * u_sh, u)
        a = jnp.where(m, a * a_sh, a)
        s *= 2
    yield
    gpb = tt // SUBLANES
    if gpb == 1:
        hs = u + a * io.h0()
        h_last = hs[:, SUBLANES - 1:SUBLANES, :]
    else:
        assert bb == 1
        h_in = io.h0()[0]
        groups = []
        for g in range(gpb):
            hg = u[g] + a[g] * h_in
            groups.append(hg)
            h_in = hg[SUBLANES - 1:SUBLANES, :]
        hs = jnp.concatenate(groups, axis=0)
        h_last = h_in.reshape(1, 1, D_R)
    y = hs.reshape(n, D_R) * jax.nn.gelu(io.gate(), approximate=True)
    io.put_y(y.reshape(bb, tt, D_R))
    io.put_h(h_last)
    yield


def _lru_body(xr_ref, gate_ref, cs_ref, h0_ref, cw_ref, cb_ref, wg_ref, bg_ref, lam_ref,
              y_ref, hnew_ref, csnew_ref, ext_ref, *, tt, bb, n_t):
    t = pl.program_id(1)

    @pl.when(t == 0)
    def _():
        ext_ref[:, TAIL - (CONV_W - 1):TAIL, :] = cs_ref[...]
        hnew_ref[...] = h0_ref[...]

    ext_ref[:, TAIL:TAIL + tt, :] = xr_ref[...]

    def put_y(val):
        y_ref[...] = val.astype(BF16)

    def put_h(val):
        hnew_ref[...] = val

    io = _Io(ext=lambda: ext_ref[...], gate=lambda: gate_ref[...].reshape(bb * tt, D_R),
             h0=lambda: hnew_ref[...], put_y=put_y, put_h=put_h)
    for _ in _lru_stream(io, cw_ref, cb_ref, wg_ref, bg_ref, lam_ref, tt=tt, bb=bb):
        pass

    tail = ext_ref[:, tt:tt + TAIL, :]
    ext_ref[:, 0:TAIL, :] = tail

    @pl.when(t == n_t - 1)
    def _():
        csnew_ref[...] = ext_ref[:, TAIL - (CONV_W - 1):TAIL, :]


def _lru(xr, gate, cs, h0, lw, l, *, tt, bb):
    b, t, _ = xr.shape
    n_t = t // tt
    assert n_t * tt == t and b % bb == 0 and tt % SUBLANES == 0
    ct = lambda i, j: (l, i, 0, 0)
    tok = pl.BlockSpec((bb, tt, D_R), lambda i, j: (i, j, 0))
    const = lambda shape: pl.BlockSpec(shape, lambda i, j: (0,) * len(shape))
    return pl.pallas_call(
        functools.partial(_lru_body, tt=tt, bb=bb, n_t=n_t),
        grid=(b // bb, n_t),
        in_specs=[tok, tok,
                  pl.BlockSpec((None, bb, CONV_W - 1, D_R), ct),
                  pl.BlockSpec((None, bb, 1, D_R), ct),
                  const((CONV_W, D_R)), const((1, D_R)), const((D_R, 2 * D_R)), const((1, 2 * D_R)),
                  const((1, D_R))],
        out_specs=[tok,
                   pl.BlockSpec((bb, 1, D_R), lambda i, j: (i, 0, 0)),
                   pl.BlockSpec((bb, CONV_W - 1, D_R), lambda i, j: (i, 0, 0))],
        out_shape=[jax.ShapeDtypeStruct((b, t, D_R), BF16),
                   jax.ShapeDtypeStruct((b, 1, D_R), F32),
                   jax.ShapeDtypeStruct((b, CONV_W - 1, D_R), F32)],
        scratch_shapes=[pltpu.VMEM((bb, tt + TAIL, D_R), F32)],
        compiler_params=pltpu.CompilerParams(dimension_semantics=("arbitrary", "arbitrary"),
                                             vmem_limit_bytes=VMEM_LIMIT),
        name="lru",
    )(xr, gate, cs, h0, lw["clw"], lw["clb"], lw["wgate"], lw["bgate"], lw["lam"])


def _interleave(streams, plan, lead, set_dep):
    live = dict(streams)

    def step(key):
        if key not in live:
            return
        try:
            val = next(live[key])
            if key == lead:
                set_dep(val)
        except StopIteration:
            del live[key]

    for key in plan:
        step(key)
    while live:
        for key in list(live):
            step(key)


def _in_mix_body(flag_ref, x_ref, n1_ref, wg_ref, wu_ref, wd_ref, nm_ref, wina_ref, winb_ref,
                 dcs_ref, ds0_ref, cw_ref, alog_ref, dtb_ref, on_ref,
                 lcs_ref, lh0_ref, lcw_ref, lcb_ref, lwg_ref, lbg_ref, lam_ref,
                 x1_ref, o_ref, y_ref, snew_ref, dcsnew_ref, hnew_ref, lcsnew_ref,
                 qkv_scr, z_scr, ba_scr, xr_scr, gate_scr, dtail_scr, ltail_scr, s_scr, h_scr,
                 *, tiles_per_seq):
    tt = MIX_TILE
    s = pl.program_id(0)
    slot_a = lax.rem(s, 2)
    slot_b = 1 - slot_a
    first = lax.rem(s + tiles_per_seq - 1, tiles_per_seq) == 0
    tails = slice(TAIL - (CONV_W - 1), TAIL)
    keep = flag_ref[0] == 1
    dep = [None]

    def after(x):
        if dep[0] is None:
            return x
        reps = x.shape[-1] // DK
        d = dep[0] if reps == 1 else jnp.concatenate([dep[0]] * reps, axis=-1)
        return jnp.where(keep, x, d)

    def set_dep(val):
        dep[0] = val

    @pl.when(s == 0)
    def _():
        for ref in (qkv_scr, z_scr, ba_scr, xr_scr, gate_scr, dtail_scr, ltail_scr, s_scr, h_scr):
            ref[...] = jnp.zeros(ref.shape, F32)

    def stream_a():
        x = x_ref[...]
        h = _rms(x, n1_ref[...]).astype(BF16)
        pieces = _ffn_pieces(h, wg_ref, wu_ref, wd_ref)
        for _ in range(D_FF // FF_PIECE):
            yield next(pieces)[0:1, 0:DK]
        acc = next(pieces)
        x1 = x + 0.5 * acc
        x1_ref[...] = x1
        hm = _rms(x1, nm_ref[...]).astype(BF16)
        yield acc[0:1, 0:DK]
        dsts = ((qkv_scr, TAIL, wina_ref, 0, D_QKV), (z_scr, 0, wina_ref, D_QKV, D_A),
                (xr_scr, TAIL, winb_ref, 0, D_R), (gate_scr, 0, winb_ref, D_R, D_R),
                (ba_scr, 0, winb_ref, 2 * D_R, D_BA))
        for dst, row0, w_ref, col0, width in dsts:
            for d0 in range(0, width, PROJ_PIECE):
                w = min(PROJ_PIECE, width - d0)
                p = jnp.dot(hm, w_ref[:, col0 + d0:col0 + d0 + w], preferred_element_type=F32)
                dst[slot_a, row0:row0 + tt, d0:d0 + w] = p
                yield p[0:1, 0:DK]

    qkv_scr[slot_b, tails, :] = jnp.where(first, dcs_ref[0], dtail_scr[tails, :])
    xr_scr[slot_b, tails, :] = jnp.where(first, lcs_ref[0], ltail_scr[tails, :])

    def put_o(tile, h, val):
        o_ref[tile[2]:tile[2] + CHUNK, h * DV:(h + 1) * DV] = val.astype(BF16)

    def put_state(b, h, val):
        s_scr[h] = val
        snew_ref[0, h] = val

    def put_y(val):
        y_ref[...] = val.reshape(tt, D_R).astype(BF16)

    def put_h(val):
        h_scr[...] = val
        hnew_ref[...] = val

    dio = _Io(ext=lambda tile, col0, w: qkv_scr[slot_b, tile[2]:tile[2] + TAIL + CHUNK, col0:col0 + w][None],
              ba=lambda tile: ba_scr[slot_b, tile[2]:tile[2] + CHUNK, :],
              z=lambda tile, col0: z_scr[slot_b, tile[2]:tile[2] + CHUNK, col0:col0 + DV],
              put_o=put_o, get_state=lambda b, h: jnp.where(first, ds0_ref[0, h], s_scr[h]),
              put_state=put_state, after=after)
    lio = _Io(ext=lambda: xr_scr[slot_b][None],
              gate=lambda: gate_scr[slot_b], h0=lambda: jnp.where(first, lh0_ref[...], h_scr[...]),
              put_y=put_y, put_h=put_h, after=after)
    tiles = tuple((0, 1, j * CHUNK) for j in range(tt // CHUNK))
    _interleave({"a": stream_a(),
                 "d": _delta_stream(dio, cw_ref, alog_ref, dtb_ref, on_ref, tiles=tiles, seg=CHUNK),
                 "l": _lru_stream(lio, lcw_ref, lcb_ref, lwg_ref, lbg_ref, lam_ref, tt=tt, bb=1)},
                MIX_PLAN, "a", set_dep)

    dtail_scr[...] = qkv_scr[slot_b, tt:tt + TAIL, :]
    ltail_scr[...] = xr_scr[slot_b, tt:tt + TAIL, :]
    dcsnew_ref[0] = qkv_scr[slot_b, tt + TAIL - (CONV_W - 1):tt + TAIL, :]
    lcsnew_ref[0] = xr_scr[slot_b, tt + TAIL - (CONV_W - 1):tt + TAIL, :]


def _in_mix(x, sdc, sd, slc, sl, lw, l, *, batch):
    r = x.shape[0]
    tt = MIX_TILE
    n_tiles = r // tt
    tiles_per_seq = n_tiles // batch
    assert n_tiles * tt == r and tiles_per_seq * batch == n_tiles
    cur = lambda s: (jnp.minimum(s, n_tiles - 1), 0)
    prev = lambda s: (jnp.maximum(s - 1, 0), 0)
    seq3 = lambda s: (jnp.maximum(s - 1, 0) // tiles_per_seq, 0, 0)
    seq4 = lambda s: (jnp.maximum(s - 1, 0) // tiles_per_seq, 0, 0, 0)
    layer = lambda shape: pl.BlockSpec((None,) + shape, lambda s: (l,) + (0,) * len(shape),
                                       pipeline_mode=pl.Buffered(1))
    return pl.pallas_call(
        functools.partial(_in_mix_body, tiles_per_seq=tiles_per_seq),
        grid=(n_tiles + 1,),
        in_specs=[pl.BlockSpec(memory_space=pltpu.SMEM),
                  pl.BlockSpec((tt, D_MODEL), cur), _resident((1, D_MODEL)),
                  _resident((D_MODEL, D_FF)), _resident((D_MODEL, D_FF)), _resident((D_FF, D_MODEL)),
                  _resident((1, D_MODEL)), _resident((D_MODEL, D_PROJ_A)), _resident((D_MODEL, D_PROJ_B)),
                  layer((1, CONV_W - 1, D_QKV)), layer((1, H_A, DK, DV)),
                  _resident((CONV_W, D_QKV)), _resident((1, D_BA)), _resident((1, D_BA)), _resident((1, DV)),
                  layer((1, CONV_W - 1, D_R)), layer((1, 1, D_R)),
                  _resident((CONV_W, D_R)), _resident((1, D_R)), _resident((D_R, 2 * D_R)),
                  _resident((1, 2 * D_R)), _resident((1, D_R))],
        out_specs=[pl.BlockSpec((tt, D_MODEL), cur), pl.BlockSpec((tt, D_A), prev), pl.BlockSpec((tt, D_R), prev),
                   pl.BlockSpec((1, H_A, DK, DV), seq4), pl.BlockSpec((1, CONV_W - 1, D_QKV), seq3),
                   pl.BlockSpec((1, 1, D_R), seq3), pl.BlockSpec((1, CONV_W - 1, D_R), seq3)],
        out_shape=[jax.ShapeDtypeStruct((r, D_MODEL), F32), jax.ShapeDtypeStruct((r, D_A), BF16),
                   jax.ShapeDtypeStruct((r, D_R), BF16),
                   jax.ShapeDtypeStruct((batch, H_A, DK, DV), F32),
                   jax.ShapeDtypeStruct((batch, CONV_W - 1, D_QKV), F32),
                   jax.ShapeDtypeStruct((batch, 1, D_R), F32),
                   jax.ShapeDtypeStruct((batch, CONV_W - 1, D_R), F32)],
        scratch_shapes=[pltpu.VMEM((2, TAIL + tt, D_QKV), F32), pltpu.VMEM((2, tt, D_A), F32),
                        pltpu.VMEM((2, tt, D_BA), F32), pltpu.VMEM((2, TAIL + tt, D_R), F32),
                        pltpu.VMEM((2, tt, D_R), F32), pltpu.VMEM((TAIL, D_QKV), F32),
                        pltpu.VMEM((TAIL, D_R), F32), pltpu.VMEM((H_A, DK, DV), F32),
                        pltpu.VMEM((1, 1, D_R), F32)],
        compiler_params=pltpu.CompilerParams(dimension_semantics=("arbitrary",),
                                             vmem_limit_bytes=VMEM_LIMIT),
        name="in_mix",
    )(jnp.ones((1,), jnp.int32), x, lw["n1"], lw["f1g"], lw["f1u"], lw["f1d"], lw["nm"], lw["w_in_a"], lw["w_in_b"],
      sdc, sd, lw["cqkv"], lw["alog"], lw["dtb"], lw["onorm"],
      slc, sl, lw["clw"], lw["clb"], lw["wgate"], lw["bgate"], lw["lam"])


BIG_WEIGHTS = ("f1g", "f1u", "f1d", "w_in_a", "w_out", "f2g", "f2u", "f2d")


def _prep_layer(l, w, big):
    row = lambda v: v.reshape(1, -1).astype(F32)
    lane_pad = lambda v: jnp.zeros((1, D_BA), F32).at[0, H_A:2 * H_A].set(v)
    eye = jnp.eye(NB_R, dtype=F32)
    bd = lambda wb: jnp.einsum("ncd,nm->ncmd", wb, eye).reshape(D_R, D_R)
    w_in = w["w_in"][l]
    c4 = D_PROJ_A + 2 * H_A
    pad = jnp.zeros((D_MODEL, D_BA - 2 * H_A), w_in.dtype)
    w_in_b = jnp.concatenate([w_in[:, c4:], w_in[:, D_PROJ_A:c4], pad], axis=-1).astype(BF16)
    return dict(
        n1=row(w["norm_ffn1"][l]), f1g=big["f1g"], f1u=big["f1u"], f1d=big["f1d"],
        nm=row(w["norm_mix"][l]), w_in_a=big["w_in_a"], w_in_b=w_in_b,
        cqkv=w["conv_qkv"][l], alog=lane_pad(w["a_log"][l]), dtb=lane_pad(w["dt_bias"][l]),
        onorm=row(w["norm_delta_out"][l]),
        clw=w["conv_lru_w"][l], clb=row(w["conv_lru_b"][l]),
        wgate=jnp.concatenate([bd(w["w_rgate"][l]), bd(w["w_igate"][l])], axis=1).astype(BF16),
        bgate=jnp.concatenate([row(w["b_rgate"][l]), row(w["b_igate"][l])], axis=1),
        lam=row(w["lru_lambda"][l]),
        w_out=big["w_out"], n2=row(w["norm_ffn2"][l]), f2g=big["f2g"], f2u=big["f2u"], f2d=big["f2d"],
    )


def _group_layers(x, states_at, layers, nf, out, *, seg=None, delta_tt=None, delta_bb=None, tiles=None,
                  lru_tt=None, lru_bb=None, fused=False, states_only=False, cast_next=None):
    b, t, _ = x.shape
    xf = x.reshape(b * t, D_MODEL)
    s_all = None
    n_layers = len(layers)
    for l in range(n_layers):
        lw = layers[l]
        sd, sdc, sl, slc = states_at(l)
        if fused:
            x1, o, y, s_new, cs_new, h_new, lcs_new = _in_mix(xf, sdc, sd, slc, sl, lw, l, batch=b)
            out["nd"].append(s_new)
        else:
            x1, qkv, z, xr, gate, ba = _ffn_in(xf, lw)
            r3 = lambda v: v.reshape(b, t, v.shape[-1])
            o, s_all, cs_new = _delta(r3(qkv), r3(z), r3(ba), sdc, sd, lw, l, s_all,
                                      seg=seg, tt=delta_tt, bb=delta_bb, tiles=tiles)
            y, h_new, lcs_new = _lru(r3(xr), r3(gate), slc, sl, lw, l, tt=lru_tt, bb=lru_bb)
            out["s_all"] = s_all
        last = l == n_layers - 1
        if not (last and states_only):
            cast = cast_next(l) if cast_next is not None and not last else ()
            xf, copies = _out_ffn(x1, o.reshape(b * t, D_A), y.reshape(b * t, D_R), lw, nf, last, cast)
            if cast:
                out["cast"][l + 1] = dict(zip(BIG_WEIGHTS, copies))
        out["ndc"].append(cs_new)
        out["nl"].append(h_new)
        out["nlc"].append(lcs_new)
        if last and not states_only:
            out["y"] = xf.reshape(b, t, D_MODEL)
        yield


def kernel(x_prompt, x_sample, state_delta, state_delta_conv, state_lru, state_lru_conv, meta_tokens, norm_ffn1, w_ffn1_gate, w_ffn1_up, w_ffn1_down, norm_mix, w_in, conv_qkv, a_log, dt_bias, norm_delta_out, conv_lru_w, conv_lru_b, w_rgate, b_rgate, w_igate, b_igate, lru_lambda, w_out, norm_ffn2, w_ffn2_gate, w_ffn2_up, w_ffn2_down, norm_final):
    w = dict(norm_ffn1=norm_ffn1, w_ffn1_gate=w_ffn1_gate, w_ffn1_up=w_ffn1_up, w_ffn1_down=w_ffn1_down,
             norm_mix=norm_mix, w_in=w_in, conv_qkv=conv_qkv, a_log=a_log, dt_bias=dt_bias,
             norm_delta_out=norm_delta_out, conv_lru_w=conv_lru_w, conv_lru_b=conv_lru_b,
             w_rgate=w_rgate, b_rgate=b_rgate, w_igate=w_igate, b_igate=b_igate, lru_lambda=lru_lambda,
             w_out=w_out, norm_ffn2=norm_ffn2, w_ffn2_gate=w_ffn2_gate, w_ffn2_up=w_ffn2_up,
             w_ffn2_down=w_ffn2_down)
    depth = norm_ffn1.shape[0]
    big_f32 = dict(f1g=(w_ffn1_gate, D_FF), f1u=(w_ffn1_up, D_FF), f1d=(w_ffn1_down, D_MODEL),
                   w_in_a=(w_in, D_PROJ_A), w_out=(w_out, D_MODEL),
                   f2g=(w_ffn2_gate, D_FF), f2u=(w_ffn2_up, D_FF), f2d=(w_ffn2_down, D_MODEL))
    layers = [None] * depth
    layers[0] = _prep_layer(0, w, {k: big_f32[k][0][0, :, :big_f32[k][1]].astype(BF16) for k in BIG_WEIGHTS})
    cast_next = lambda l: tuple((big_f32[k][0], l + 1, big_f32[k][1]) for k in BIG_WEIGHTS)
    nf = norm_final.reshape(1, D_MODEL).astype(F32)
    bp, seq, _ = x_prompt.shape
    bs, dseq, _ = x_sample.shape
    seg_b = CHUNK // dseq
    new_out = lambda: dict(nd=[], ndc=[], nl=[], nlc=[], cast={}, s_all=None, y=None)
    stack = jnp.stack

    zeros = lambda *s: jnp.zeros((depth,) + s, F32)
    m_states = (zeros(1, H_A, DK, DV), zeros(1, CONV_W - 1, D_QKV), zeros(1, 1, D_R), zeros(1, CONV_W - 1, D_R))
    m_out, p_out, s_out = new_out(), new_out(), new_out()
    meta = _group_layers(meta_tokens.astype(F32)[None], lambda l: m_states, layers, nf, m_out,
                         seg=N_META, delta_tt=N_META, delta_bb=1, tiles=((0, 1, 0),), lru_tt=N_META, lru_bb=1,
                         states_only=True)
    after_meta = lambda l: (m_out["s_all"], stack(m_out["ndc"]), stack(m_out["nl"]), stack(m_out["nlc"]))
    prompt = _group_layers(x_prompt, after_meta, layers, nf, p_out, fused=True, cast_next=cast_next)
    s_states = (state_delta, state_delta_conv, state_lru.reshape(depth, bs, 1, D_R), state_lru_conv)
    sample = _group_layers(x_sample, lambda l: s_states, layers, nf, s_out,
                           seg=dseq, delta_tt=dseq, delta_bb=SAMPLE_TILES_PER_STEP * seg_b,
                           tiles=tuple((j * seg_b, seg_b, 0) for j in range(SAMPLE_TILES_PER_STEP)),
                           lru_tt=dseq, lru_bb=2 * seg_b)
    for l in range(depth):
        next(meta)
        next(prompt)
        if l + 1 < depth:
            layers[l + 1] = _prep_layer(l + 1, w, p_out["cast"][l + 1])
        next(sample)

    return (p_out["y"], s_out["y"], stack(p_out["nd"]), stack(p_out["ndc"]),
            stack(p_out["nl"]).reshape(depth, bp, D_R), stack(p_out["nlc"]),
            s_out["s_all"], stack(s_out["ndc"]), stack(s_out["nl"]).reshape(depth, bs, D_R), stack(s_out["nlc"]))
```
